```python
import math
import jax
import jax.numpy as jnp
from jax import lax
import numpy as np

D_MODEL = 1024
BATCH = 2
SEQ = 8192
DEPTH = 2

D_MIX = D_MODEL
N_MIXERS = 4
D_BRANCH = D_MIX // N_MIXERS
HEAD_DIM = 64
N_HEADS = D_BRANCH // HEAD_DIM
NORM_EPS = 1e-6
Q_BLOCK = 128
NEG_INF = -1e30
LRU_CONV = 4
LRU_C = 8.0
CMP_BLOCK = 32
CMP_STRIDE = 16
SEL_BLOCK = 64
N_SELECT = 16
WINDOW = 512
NSA_KV_DIM = HEAD_DIM
N_NSA_BRANCH = 3
CMP_HIDDEN = HEAD_DIM
FORCE_SCORE = 1e9
RWKV_DECAY_RANK = 32
RWKV_AAA_RANK = 32
RWKV_GN_EPS = 64e-5
DIFF_QK_DIM = HEAD_DIM // 2
DIFF_EPS = 1e-5
GATE_IN = D_MIX
LRU_IN = D_BRANCH
NSA_IN = D_BRANCH + 6 * NSA_KV_DIM + N_NSA_BRANCH * N_HEADS
RWKV_IN = 3 * D_BRANCH + RWKV_DECAY_RANK + RWKV_AAA_RANK
DIFF_IN = 3 * D_BRANCH
N_IN = GATE_IN + LRU_IN + NSA_IN + RWKV_IN + DIFF_IN

kernel_name = 'hymba_style_lru_nsa_rwkv7_diffattn'


def rms_norm(x, g, eps=NORM_EPS):
    xf = x.astype(jnp.float32)
    y = xf * lax.rsqrt(jnp.mean(xf * xf, axis=-1, keepdims=True) + eps)
    return (y * g.astype(jnp.float32)).astype(x.dtype)


def split_cols(t, sizes):
    out, off = [], 0
    for s in sizes:
        out.append(t[..., off:off + s])
        off += s
    return out


def masked_softmax(s, mask):
    return jax.nn.softmax(jnp.where(mask, s, NEG_INF), axis=-1)


def rg_lru_mixer(u, conv_w, conv_b, wa, ba, wx, bx, lam):
    B, S, C = u.shape
    u = u.astype(jnp.float32)
    xc = lax.conv_general_dilated(u, conv_w.astype(jnp.float32)[:, None, :], window_strides=(1,),
                                  padding=[(LRU_CONV - 1, 0)], dimension_numbers=('NWC', 'WIO', 'NWC'),
                                  feature_group_count=C) + conv_b
    xh = xc.reshape(B, S, N_HEADS, HEAD_DIM)
    r = jax.nn.sigmoid(jnp.einsum('bshi,hij->bshj', xh, wa).reshape(B, S, C) + ba)
    i = jax.nn.sigmoid(jnp.einsum('bshi,hij->bshj', xh, wx).reshape(B, S, C) + bx)
    log_a = -LRU_C * r * jax.nn.softplus(-lam)
    a = jnp.exp(log_a)
    b = jnp.sqrt(-jnp.expm1(2.0 * log_a)) * (i * xc)

    def combine(left, right):
        a1, b1 = left
        a2, b2 = right
        return a1 * a2, a2 * b1 + b2

    _, h = lax.associative_scan(combine, (a, b), axis=1)
    return h


def nsa_mixer(p, cmp_pos, cmp_w1, cmp_w2, gate_b):
    B, S, _ = p.shape
    p = p.astype(jnp.float32)
    q, kc_raw, vc_raw, k_sel, v_sel, k_win, v_win, g_logit = split_cols(
        p, (D_BRANCH,) + (NSA_KV_DIM,) * 6 + (N_NSA_BRANCH * N_HEADS,))
    q = q.reshape(B, S, N_HEADS, HEAD_DIM) * HEAD_DIM ** -0.5
    gates = jax.nn.sigmoid(g_logit + gate_b).reshape(B, S, N_HEADS, N_NSA_BRANCH)

    n_cmp = (S - CMP_BLOCK) // CMP_STRIDE + 1
    blk_idx = jnp.arange(n_cmp)[:, None] * CMP_STRIDE + jnp.arange(CMP_BLOCK)[None, :]

    def compress(t, pos, w1, w2):
        blk = t[:, blk_idx] + pos
        return jax.nn.gelu(blk.reshape(B, n_cmp, CMP_BLOCK * NSA_KV_DIM) @ w1) @ w2

    k_cmp = compress(kc_raw, cmp_pos[0], cmp_w1[0], cmp_w2[0])
    v_cmp = compress(vc_raw, cmp_pos[1], cmp_w1[1], cmp_w2[1])
    cmp_end = jnp.arange(n_cmp) * CMP_STRIDE + (CMP_BLOCK - 1)

    n_sel = S // SEL_BLOCK
    n_top = min(N_SELECT, n_sel)
    ratio = SEL_BLOCK // CMP_STRIDE
    span = CMP_BLOCK // CMP_STRIDE
    pad_r = ratio * n_sel - n_cmp
    sel_blocks = jnp.arange(n_sel)

    k_win_pad = jnp.pad(k_win, ((0, 0), (WINDOW, 0), (0, 0)))
    v_win_pad = jnp.pad(v_win, ((0, 0), (WINDOW, 0), (0, 0)))
    win_off = jnp.arange(WINDOW + Q_BLOCK) - WINDOW
    gather = jax.vmap(lambda kv, idx: kv[idx])

    def block(qb):
        q0 = qb * Q_BLOCK
        t = q0 + jnp.arange(Q_BLOCK)
        qq = lax.dynamic_slice_in_dim(q, q0, Q_BLOCK, axis=1)
        gg = lax.dynamic_slice_in_dim(gates, q0, Q_BLOCK, axis=1)
        valid_c = cmp_end[None, :] <= t[:, None]
        p_c = masked_softmax(jnp.einsum('bqhd,bcd->bhqc', qq, k_cmp), valid_c)
        o_c = jnp.einsum('bhqc,bcd->bqhd', p_c, v_cmp) * jnp.any(valid_c, axis=-1)[None, :, None, None]
        imp = jnp.sum(p_c, axis=1) * valid_c
        imp = jnp.pad(imp, ((0, 0), (0, 0), (span - 1, pad_r)))
        p_slc = sum(imp[..., span - 1 + m - n: span - 1 + m - n + ratio * n_sel: ratio]
                    for m in range(ratio) for n in range(span))
        cur = (t // SEL_BLOCK)[:, None]
        forced = (sel_blocks == 0) | (sel_blocks == cur) | (sel_blocks == cur - 1)
        score = jnp.where(sel_blocks > cur, -1.0, jnp.where(forced, FORCE_SCORE, p_slc))
        _, top = lax.top_k(score, n_top)
        kpos = (top[..., None] * SEL_BLOCK + jnp.arange(SEL_BLOCK)).reshape(B, Q_BLOCK, n_top * SEL_BLOCK)
        ks = gather(k_sel, kpos)
        vs = gather(v_sel, kpos)
        p_s = masked_softmax(jnp.einsum('bqhd,bqkd->bhqk', qq, ks), (kpos <= t[None, :, None])[:, None])
        o_s = jnp.einsum('bhqk,bqkd->bqhd', p_s, vs)
        kw = lax.dynamic_slice_in_dim(k_win_pad, q0, WINDOW + Q_BLOCK, axis=1)
        vw = lax.dynamic_slice_in_dim(v_win_pad, q0, WINDOW + Q_BLOCK, axis=1)
        wpos = (q0 + win_off)[None, :]
        valid_w = (wpos >= 0) & (wpos <= t[:, None]) & (wpos > t[:, None] - WINDOW)
        p_w = masked_softmax(jnp.einsum('bqhd,bkd->bhqk', qq, kw), valid_w)
        o_w = jnp.einsum('bhqk,bkd->bqhd', p_w, vw)
        return gg[..., 0:1] * o_c + gg[..., 1:2] * o_s + gg[..., 2:3] * o_w

    out = lax.map(block, jnp.arange(S // Q_BLOCK))
    return jnp.moveaxis(out, 0, 1).reshape(B, S, D_BRANCH)


def rwkv7_scan(r, w, k, v, a, b):
    def step(state, inp):
        r_t, w_t, k_t, v_t, a_t, b_t = inp
        sa = jnp.einsum('bhvk,bhk->bhv', state, a_t)
        state = (state * w_t[:, :, None, :] + sa[..., None] * b_t[:, :, None, :]
                 + v_t[..., None] * k_t[:, :, None, :])
        return state, jnp.einsum('bhvk,bhk->bhv', state, r_t)

    B, S, H, N = r.shape
    xs = tuple(jnp.moveaxis(t, 1, 0) for t in (r, w, k, v, a, b))
    _, y = lax.scan(step, jnp.zeros((B, H, N, N), jnp.float32), xs)
    return jnp.moveaxis(y, 0, 1)


def rwkv7_mixer(p, mu, w0, w_up, a0, a_up, k_k, k_a, r_k, ln_g, ln_b):
    B, S, _ = p.shape
    p = p.astype(jnp.float32)
    p_prev = jnp.pad(p, ((0, 0), (1, 0), (0, 0)))[:, :-1]
    p = p + (p_prev - p) * mu
    r, k, v, wd, ad = split_cols(p, (D_BRANCH,) * 3 + (RWKV_DECAY_RANK, RWKV_AAA_RANK))
    w = -jax.nn.softplus(-(w0 + jnp.tanh(wd) @ w_up)) - 0.5
    a = jax.nn.sigmoid(a0 + ad @ a_up)
    heads = lambda t: t.reshape(B, S, N_HEADS, HEAD_DIM)
    kk = heads(k * k_k)
    kk = kk / jnp.maximum(jnp.sqrt(jnp.sum(kk * kk, axis=-1, keepdims=True)), 1e-12)
    k = k * (1.0 + (a - 1.0) * k_a)
    decay = jnp.exp(-jnp.exp(w))
    rh, kh, vh = heads(r), heads(k), heads(v)
    y = rwkv7_scan(rh, heads(decay), kh, vh, -kk, kk * heads(a))
    mean = jnp.mean(y, axis=-1, keepdims=True)
    var = jnp.mean(jnp.square(y - mean), axis=-1, keepdims=True)
    y = ((y - mean) * lax.rsqrt(var + RWKV_GN_EPS)).reshape(B, S, D_BRANCH) * ln_g + ln_b
    bonus = jnp.sum(rh * kh * r_k, axis=-1, keepdims=True) * vh
    return y + bonus.reshape(B, S, D_BRANCH)


def diff_mixer(p, lam, subln_g, lambda_init):
    B, S, _ = p.shape
    q, k, v = split_cols(p.astype(jnp.float32), (D_BRANCH,) * 3)
    q = q.reshape(B, S, N_HEADS, 2, DIFF_QK_DIM) * DIFF_QK_DIM ** -0.5
    k = k.reshape(B, S, N_HEADS, 2, DIFF_QK_DIM)
    v = v.reshape(B, S, N_HEADS, HEAD_DIM)
    lam_full = jnp.exp(jnp.sum(lam[0] * lam[1])) - jnp.exp(jnp.sum(lam[2] * lam[3])) + lambda_init
    kpos = jnp.arange(S)

    def block(qb):
        q0 = qb * Q_BLOCK
        t = q0 + jnp.arange(Q_BLOCK)
        qq = lax.dynamic_slice_in_dim(q, q0, Q_BLOCK, axis=1)
        pr = masked_softmax(jnp.einsum('bqhcd,bkhcd->bhcqk', qq, k), kpos[None, :] <= t[:, None])
        attn = pr[:, :, 0] - lam_full * pr[:, :, 1]
        return jnp.einsum('bhqk,bkhd->bqhd', attn, v)

    o = jnp.moveaxis(lax.map(block, jnp.arange(S // Q_BLOCK)), 0, 1).reshape(B, S, N_HEADS, HEAD_DIM)
    o = o * lax.rsqrt(jnp.mean(o * o, axis=-1, keepdims=True) + DIFF_EPS) * subln_g * (1.0 - lambda_init)
    return o.reshape(B, S, D_BRANCH)


def setup_inputs(seed: int = 0) -> dict:
    key = jax.random.key(seed)
    ks = jax.random.split(key, 28)
    f32 = jnp.float32
    nrm = lambda k, shape, scale: jax.random.normal(k, shape, f32) * scale
    L = DEPTH
    x = jax.random.normal(ks[0], (BATCH, SEQ, D_MODEL), f32)
    norm_gain = 1.0 + nrm(ks[1], (L, D_MODEL), 0.02)
    w_in = nrm(ks[2], (L, D_MODEL, N_IN), D_MODEL ** -0.5)
    w_out = nrm(ks[3], (L, D_MIX, D_MODEL), D_MIX ** -0.5)
    final_gain = 1.0 + nrm(ks[4], (D_MODEL,), 0.02)
    lru_conv_w = nrm(ks[5], (L, LRU_CONV, D_BRANCH), LRU_CONV ** -0.5)
    lru_conv_b = nrm(ks[6], (L, D_BRANCH), 0.01)
    lru_wa = nrm(ks[7], (L, N_HEADS, HEAD_DIM, HEAD_DIM), HEAD_DIM ** -0.5)
    lru_ba = nrm(ks[8], (L, D_BRANCH), 0.01)
    lru_wx = nrm(ks[9], (L, N_HEADS, HEAD_DIM, HEAD_DIM), HEAD_DIM ** -0.5)
    lru_bx = nrm(ks[10], (L, D_BRANCH), 0.01)
    u = jax.random.uniform(ks[11], (L, D_BRANCH), f32, 0.9, 0.999)
    a_base = u ** (1.0 / LRU_C)
    lru_lambda = jnp.log(a_base) - jnp.log1p(-a_base)
    nsa_cmp_pos = nrm(ks[12], (L, 2, CMP_BLOCK, NSA_KV_DIM), 0.02)
    nsa_cmp_w1 = nrm(ks[13], (L, 2, CMP_BLOCK * NSA_KV_DIM, CMP_HIDDEN), (CMP_BLOCK * NSA_KV_DIM) ** -0.5)
    nsa_cmp_w2 = nrm(ks[14], (L, 2, CMP_HIDDEN, NSA_KV_DIM), CMP_HIDDEN ** -0.5)
    nsa_gate_b = nrm(ks[15], (L, N_NSA_BRANCH * N_HEADS), 0.01)
    rwkv_mu = jax.random.uniform(ks[16], (L, RWKV_IN), f32)
    rwkv_w0 = jax.random.uniform(ks[17], (L, D_BRANCH), f32, -6.0, -1.0)
    rwkv_w_up = nrm(ks[18], (L, RWKV_DECAY_RANK, D_BRANCH), 0.5 * RWKV_DECAY_RANK ** -0.5)
    rwkv_a0 = nrm(ks[19], (L, D_BRANCH), 0.1)
    rwkv_a_up = nrm(ks[20], (L, RWKV_AAA_RANK, D_BRANCH), 0.5 * RWKV_AAA_RANK ** -0.5)
    rwkv_k_k = 0.85 + nrm(ks[21], (L, D_BRANCH), 0.05)
    rwkv_k_a = 1.0 + nrm(ks[22], (L, D_BRANCH), 0.05)
    rwkv_r_k = nrm(ks[23], (L, N_HEADS, HEAD_DIM), 0.1)
    rwkv_ln_g = 1.0 + nrm(ks[24], (L, D_BRANCH), 0.02)
    rwkv_ln_b = nrm(ks[25], (L, D_BRANCH), 0.01)
    diff_lambda = nrm(ks[26], (L, 4, DIFF_QK_DIM), 0.1)
    diff_subln_g = 1.0 + nrm(ks[27], (L, HEAD_DIM), 0.02)
    return {'x': x, 'norm_gain': norm_gain, 'w_in': w_in, 'w_out': w_out, 'final_gain': final_gain,
            'lru_conv_w': lru_conv_w, 'lru_conv_b': lru_conv_b, 'lru_wa': lru_wa, 'lru_ba': lru_ba,
            'lru_wx': lru_wx, 'lru_bx': lru_bx, 'lru_lambda': lru_lambda,
            'nsa_cmp_pos': nsa_cmp_pos, 'nsa_cmp_w1': nsa_cmp_w1, 'nsa_cmp_w2': nsa_cmp_w2,
            'nsa_gate_b': nsa_gate_b,
            'rwkv_mu': rwkv_mu, 'rwkv_w0': rwkv_w0, 'rwkv_w_up': rwkv_w_up, 'rwkv_a0': rwkv_a0,
            'rwkv_a_up': rwkv_a_up, 'rwkv_k_k': rwkv_k_k, 'rwkv_k_a': rwkv_k_a, 'rwkv_r_k': rwkv_r_k,
            'rwkv_ln_g': rwkv_ln_g, 'rwkv_ln_b': rwkv_ln_b,
            'diff_lambda': diff_lambda, 'diff_subln_g': diff_subln_g}


def reference(x, norm_gain, w_in, w_out, final_gain,
              lru_conv_w, lru_conv_b, lru_wa, lru_ba, lru_wx, lru_bx, lru_lambda,
              nsa_cmp_pos, nsa_cmp_w1, nsa_cmp_w2, nsa_gate_b,
              rwkv_mu, rwkv_w0, rwkv_w_up, rwkv_a0, rwkv_a_up, rwkv_k_k, rwkv_k_a, rwkv_r_k,
              rwkv_ln_g, rwkv_ln_b,
              diff_lambda, diff_subln_g):
    for l in range(DEPTH):
        h = rms_norm(x, norm_gain[l])
        proj = h @ w_in[l]
        gate, p_lru, p_nsa, p_rwkv, p_diff = split_cols(proj, (GATE_IN, LRU_IN, NSA_IN, RWKV_IN, DIFF_IN))
        o_lru = rg_lru_mixer(p_lru, lru_conv_w[l], lru_conv_b[l], lru_wa[l], lru_ba[l],
                             lru_wx[l], lru_bx[l], lru_lambda[l])
        o_nsa = nsa_mixer(p_nsa, nsa_cmp_pos[l], nsa_cmp_w1[l], nsa_cmp_w2[l], nsa_gate_b[l])
        o_rwkv = rwkv7_mixer(p_rwkv, rwkv_mu[l], rwkv_w0[l], rwkv_w_up[l], rwkv_a0[l], rwkv_a_up[l],
                             rwkv_k_k[l], rwkv_k_a[l], rwkv_r_k[l], rwkv_ln_g[l], rwkv_ln_b[l])
        lambda_init = 0.8 - 0.6 * math.exp(-0.3 * l)
        o_diff = diff_mixer(p_diff, diff_lambda[l], diff_subln_g[l], lambda_init)
        mix = jnp.concatenate([o_lru, o_nsa, o_rwkv, o_diff], axis=-1)
        y = (mix * jax.nn.silu(gate.astype(jnp.float32))).astype(x.dtype) @ w_out[l]
        x = x + y
    return rms_norm(x, final_gain)
```

```python
import functools
import math

import jax
import jax.numpy as jnp
from jax import lax
from jax.experimental import pallas as pl
from jax.experimental.pallas import tpu as pltpu

F32 = jnp.float32
BF16 = jnp.bfloat16

N_MIX = 4
HEAD_DIM = 64
N_HEADS = 4
D_BRANCH = N_HEADS * HEAD_DIM
NORM_EPS = 1e-6
NEG_INF = -1e30
LRU_CONV = 4
LRU_C = 8.0
CMP_BLOCK = 32
CMP_STRIDE = 16
SEL_BLOCK = 64
N_SELECT = 16
WINDOW = 512
N_NSA_BRANCH = 3
FORCE_SCORE = 1e9
RWKV_RANK = 32
RWKV_GN_EPS = 64e-5
DIFF_QK_DIM = HEAD_DIM // 2
DIFF_EPS = 1e-5
N_PROJ_BLOCKS = 15
Q_BLOCK = 128
KV_TILE = 512
RWKV_CHUNK = 64
VMEM_LIMIT = 56 * 1024 * 1024


def _dg(a, b, ca=1, cb=0):
    return lax.dot_general(a, b, (((ca,), (cb,)), ((), ())), preferred_element_type=F32)


def _split2(a):
    hi = a.astype(BF16)
    lo = (a - hi.astype(F32)).astype(BF16)
    return hi, lo


def _dot3(a, b, ca=1, cb=0):
    ah, al = _split2(a)
    bh, bl = _split2(b)
    return _dg(ah, bh, ca, cb) + (_dg(ah, bl, ca, cb) + _dg(al, bh, ca, cb))


def _dot3_pre(a, bh, bl, ca=1, cb=0):
    ah, al = _split2(a)
    return _dg(ah, bh, ca, cb) + (_dg(ah, bl, ca, cb) + _dg(al, bh, ca, cb))


def _dot_lx(a, b_exact, ca=1, cb=0):
    a1 = a.astype(BF16)
    r = a - a1.astype(F32)
    a2 = r.astype(BF16)
    a3 = (r - a2.astype(F32)).astype(BF16)
    return _dg(a1, b_exact, ca, cb) + (_dg(a2, b_exact, ca, cb) + _dg(a3, b_exact, ca, cb))


def _dot_xl(a_exact, b, ca=1, cb=0):
    b1 = b.astype(BF16)
    r = b - b1.astype(F32)
    b2 = r.astype(BF16)
    b3 = (r - b2.astype(F32)).astype(BF16)
    return _dg(a_exact, b1, ca, cb) + (_dg(a_exact, b2, ca, cb) + _dg(a_exact, b3, ca, cb))


def _head_masks(width=D_BRANCH, group=HEAD_DIM):
    lane = lax.broadcasted_iota(jnp.int32, (1, width), 1)
    return [((lane >= h * group) & (lane < (h + 1) * group)).astype(F32) for h in range(width // group)]


def _block_ones(scale=1.0):
    r = lax.broadcasted_iota(jnp.int32, (D_BRANCH, D_BRANCH), 0) // HEAD_DIM
    c = lax.broadcasted_iota(jnp.int32, (D_BRANCH, D_BRANCH), 1) // HEAD_DIM
    return jnp.where(r == c, scale, 0.0).astype(BF16)


def _sigmoid(x):
    return 1.0 / (1.0 + jnp.exp(-x))


def _softplus(x):
    return jnp.maximum(x, 0.0) + jnp.log1p(jnp.exp(-jnp.abs(x)))


def _params(sem):
    return pltpu.CompilerParams(dimension_semantics=sem, vmem_limit_bytes=VMEM_LIMIT)


def _proj_kernel(x_ref, g_ref, wh_ref, wl_ref, o_ref):
    x = x_ref[...]
    ms = jnp.mean(x * x, axis=-1, keepdims=True)
    h = x * lax.rsqrt(ms + NORM_EPS) * g_ref[...]
    o_ref[...] = _dot3_pre(h, wh_ref[...], wl_ref[...])


def _in_proj(x2d, gain, w_hi, w_lo, tm=512, tn=768):
    t, d = x2d.shape
    n = w_hi.shape[1]
    return pl.pallas_call(
        _proj_kernel,
        grid=(t // tm, n // tn),
        in_specs=[pl.BlockSpec((tm, d), lambda i, j: (i, 0)),
                  pl.BlockSpec((1, d), lambda i, j: (0, 0)),
                  pl.BlockSpec((d, tn), lambda i, j: (0, j)),
                  pl.BlockSpec((d, tn), lambda i, j: (0, j))],
        out_specs=pl.BlockSpec((tm, tn), lambda i, j: (i, j)),
        out_shape=jax.ShapeDtypeStruct((t, n), F32),
        compiler_params=_params(("parallel", "arbitrary")),
        name="in_proj",
    )(x2d, gain, w_hi, w_lo)


def _out_kernel(final, o1_ref, o2_ref, o3_ref, o4_ref, gate_ref, x_ref, wh_ref, wl_ref, fg_ref, y_ref):
    mix = jnp.concatenate([o1_ref[...], o2_ref[...], o3_ref[...], o4_ref[...]], axis=-1)
    g = gate_ref[...]
    z = mix * (g * _sigmoid(g))
    y = x_ref[...] + _dot3_pre(z, wh_ref[...], wl_ref[...])
    if final:
        ms = jnp.mean(y * y, axis=-1, keepdims=True)
        y = y * lax.rsqrt(ms + NORM_EPS) * fg_ref[...]
    y_ref[...] = y


def _out_proj(mixers, proj2d, x2d, w_hi, w_lo, final_gain, final, tm=512):
    t, d = x2d.shape
    dm = w_hi.shape[0]
    branch = pl.BlockSpec((tm, D_BRANCH), lambda i: (i, 0))
    return pl.pallas_call(
        functools.partial(_out_kernel, final),
        grid=(t // tm,),
        in_specs=[branch, branch, branch, branch,
                  pl.BlockSpec((tm, dm), lambda i: (i, 0)),
                  pl.BlockSpec((tm, d), lambda i: (i, 0)),
                  pl.BlockSpec((dm, d), lambda i: (0, 0)),
                  pl.BlockSpec((dm, d), lambda i: (0, 0)),
                  pl.BlockSpec((1, d), lambda i: (0, 0))],
        out_specs=pl.BlockSpec((tm, d), lambda i: (i, 0)),
        out_shape=jax.ShapeDtypeStruct((t, d), F32),
        compiler_params=_params(("parallel",)),
        name="out_proj",
    )(*mixers, proj2d, x2d, w_hi, w_lo, final_gain)


def _lru_kernel(u_ref, cw_ref, cb_ref, wah_ref, wal_ref, ba_ref, wxh_ref, wxl_ref, bx_ref, lam_ref,
                o_ref, ubuf, hcar):
    ts = u_ref.shape[1]

    @pl.when(pl.program_id(1) == 0)
    def _():
        ubuf[0:8, :] = jnp.zeros((8, D_BRANCH), F32)
        hcar[...] = jnp.zeros_like(hcar)

    u = u_ref[0]
    ubuf[8:8 + ts, :] = u
    xc = cb_ref[...] + cw_ref[0:1, :] * ubuf[5:5 + ts, :]
    for j in range(1, LRU_CONV):
        xc = xc + cw_ref[j:j + 1, :] * ubuf[5 + j:5 + j + ts, :]
    ubuf[0:8, :] = u[ts - 8:ts, :]

    r = _sigmoid(_dot3_pre(xc, wah_ref[...], wal_ref[...]) + ba_ref[...])
    gi = _sigmoid(_dot3_pre(xc, wxh_ref[...], wxl_ref[...]) + bx_ref[...])
    log_a = (-LRU_C) * r * _softplus(-lam_ref[...])
    a = jnp.exp(log_a)
    b = jnp.sqrt(jnp.tanh(-log_a) * (a * a + 1.0)) * (gi * xc)

    row = lax.broadcasted_iota(jnp.int32, (ts, 1), 0)
    d = 1
    while d < ts:
        keep = row >= d
        a_s = jnp.where(keep, pltpu.roll(a, d, axis=0), 1.0)
        b_s = jnp.where(keep, pltpu.roll(b, d, axis=0), 0.0)
        b = a * b_s + b
        a = a * a_s
        d *= 2
    h = b + a * hcar[...]
    o_ref[0] = h
    hcar[...] = h[ts - 1:ts, :]


def _lru_mixer(proj, conv_w, conv_b, wa_bd, ba, wx_bd, bx, lam, ts=512):
    bsz, s, _ = proj.shape
    wah, wal = _split2(wa_bd)
    wxh, wxl = _split2(wx_bd)
    row = pl.BlockSpec((1, D_BRANCH), lambda b, j: (0, 0))
    mat = pl.BlockSpec((D_BRANCH, D_BRANCH), lambda b, j: (0, 0))
    return pl.pallas_call(
        _lru_kernel,
        grid=(bsz, s // ts),
        in_specs=[pl.BlockSpec((1, ts, D_BRANCH), lambda b, j: (b, j, 4)),
                  pl.BlockSpec((LRU_CONV, D_BRANCH), lambda b, j: (0, 0)),
                  row, mat, mat, row, mat, mat, row, row],
        out_specs=pl.BlockSpec((1, ts, D_BRANCH), lambda b, j: (b, j, 0)),
        out_shape=jax.ShapeDtypeStruct((bsz, s, D_BRANCH), F32),
        scratch_shapes=[pltpu.VMEM((ts + 8, D_BRANCH), F32), pltpu.VMEM((1, D_BRANCH), F32)],
        compiler_params=_params(("parallel", "arbitrary")),
        name="rg_lru",
    )(proj, conv_w, conv_b, wah, wal, ba, wxh, wxl, bx, lam)


def _cmp_kernel(rk_ref, rv_ref, pos_ref, w1_ref, w2_ref, kc_ref, vc_ref):
    half = (CMP_BLOCK // 2) * HEAD_DIM
    nrow = rk_ref.shape[1]
    for idx, (r_ref, o_ref) in enumerate(((rk_ref, kc_ref), (rv_ref, vc_ref))):
        r = r_ref[0]
        w1 = w1_ref[idx]
        lo_half = _dot3(r, w1[0:half, :])
        hi_half = _dot3(r, w1[half:2 * half, :])
        bias = _dot3(jnp.broadcast_to(pos_ref[idx], (8, 2 * half)), w1)[0:1, :]
        hid = lo_half + pltpu.roll(hi_half, nrow - 1, axis=0) + bias
        o_ref[0] = _dot3(jax.nn.gelu(hid), w2_ref[idx])


def _nsa_compress(rk, rv, pos_flat, w1, w2_rep):
    bsz, nrow, width = rk.shape
    blk = pl.BlockSpec((1, nrow, width), lambda b: (b, 0, 0))
    out = pl.BlockSpec((1, nrow, D_BRANCH), lambda b: (b, 0, 0))
    return pl.pallas_call(
        _cmp_kernel,
        grid=(bsz,),
        in_specs=[blk, blk,
                  pl.BlockSpec(pos_flat.shape, lambda b: (0, 0, 0)),
                  pl.BlockSpec(w1.shape, lambda b: (0, 0, 0)),
                  pl.BlockSpec(w2_rep.shape, lambda b: (0, 0, 0))],
        out_specs=[out, out],
        out_shape=[jax.ShapeDtypeStruct((bsz, nrow, D_BRANCH), F32)] * 2,
        compiler_params=_params(("parallel",)),
        name="nsa_compress",
    )(rk, rv, pos_flat, w1, w2_rep)


def _nsa_kernel(n_top, q_ref, g_ref, kc_ref, vc_ref, ks_ref, vs_ref,
                kw0, kw1, kw2, kw3, kw4, vw0, vw1, vw2, vw3, vw4,
                e_ref, m_ref, gb_ref, o_ref, acc_ref, m_scr, l_scr):
    qb = q_ref.shape[1]
    n_cmp = kc_ref.shape[1]
    n_sel = m_ref.shape[1]
    tk = e_ref.shape[2]
    i = pl.program_id(1)
    q0 = i * qb
    hm = _head_masks()
    t = q0 + lax.broadcasted_iota(jnp.int32, (qb, 1), 0)

    q = q_ref[0] * (HEAD_DIM ** -0.5)
    qm = jnp.concatenate([q * hm[h] for h in range(N_HEADS)], axis=0)
    qm_bf = qm.astype(BF16)
    t4 = jnp.concatenate([t] * N_HEADS, axis=0)

    cend = lax.broadcasted_iota(jnp.int32, (1, n_cmp), 1) * CMP_STRIDE + (CMP_BLOCK - 1)
    valid4 = cend <= t4
    s_c = jnp.where(valid4, _dot3(qm, kc_ref[0], 1, 1), NEG_INF)
    mx = jnp.max(s_c, axis=-1, keepdims=True)
    p_c = jnp.exp(s_c - mx)
    p_c = p_c / jnp.sum(p_c, axis=-1, keepdims=True)
    anyv = (t4 >= CMP_BLOCK - 1).astype(F32)
    oc4 = _dot3(p_c, vc_ref[0]) * anyv
    p_cv = jnp.where(valid4, p_c, 0.0)
    imp = p_cv[0:qb] + p_cv[qb:2 * qb] + p_cv[2 * qb:3 * qb] + p_cv[3 * qb:4 * qb]
    p_slc = _dot_lx(imp, m_ref[...])

    blk = lax.broadcasted_iota(jnp.int32, (1, n_sel), 1)
    blk_f = blk.astype(F32)
    cur = t // SEL_BLOCK
    forced = (blk == 0) | (blk == cur) | (blk == cur - 1)
    score = jnp.where(blk > cur, -1.0, jnp.where(forced, FORCE_SCORE, p_slc))
    sel = jnp.zeros((qb, n_sel), F32)
    for _ in range(n_top):
        best = jnp.max(score, axis=-1, keepdims=True)
        idx = jnp.min(jnp.where(score == best, blk_f, float(n_sel)), axis=-1, keepdims=True)
        pick = blk_f == idx
        sel = jnp.where(pick, 1.0, sel)
        score = jnp.where(pick, -3e38, score)
    sel_bf = sel.astype(BF16)

    m_scr[...] = jnp.full(m_scr.shape, NEG_INF, F32)
    l_scr[...] = jnp.zeros(l_scr.shape, F32)
    acc_ref[...] = jnp.zeros(acc_ref.shape, F32)
    n_tiles = (q0 + qb + tk - 1) // tk

    def body(j, carry):
        start = pl.multiple_of(j * tk, tk)
        k = ks_ref[0, pl.ds(start, tk), :]
        v = vs_ref[0, pl.ds(start, tk), :]
        s = _dg(qm_bf, k, 1, 1)
        kpos = start + lax.broadcasted_iota(jnp.int32, (1, tk), 1)
        allow = (_dg(sel_bf, e_ref[j]) > 0.5) & (kpos <= t)
        allow4 = jnp.concatenate([allow] * N_HEADS, axis=0)
        s = jnp.where(allow4, s, NEG_INF)
        m_old = m_scr[...]
        m_new = jnp.maximum(m_old, jnp.max(s, axis=-1, keepdims=True))
        alpha = jnp.exp(m_old - m_new)
        p = jnp.exp(s - m_new)
        l_scr[...] = alpha * l_scr[...] + jnp.sum(p, axis=-1, keepdims=True)
        acc_ref[...] = alpha * acc_ref[...] + _dg(p.astype(BF16), v)
        m_scr[...] = m_new
        return carry

    lax.fori_loop(0, n_tiles, body, 0)
    os4 = acc_ref[...] / l_scr[...]

    kw = jnp.concatenate([r[0] for r in (kw0, kw1, kw2, kw3, kw4)], axis=0)
    vw = jnp.concatenate([r[0] for r in (vw0, vw1, vw2, vw3, vw4)], axis=0)
    nw = kw.shape[0]
    wpos = (q0 - WINDOW) + lax.broadcasted_iota(jnp.int32, (1, nw), 1)
    valid_w = (wpos >= 0) & (wpos <= t4) & (wpos > t4 - WINDOW)
    s_w = jnp.where(valid_w, _dg(qm_bf, kw, 1, 1), NEG_INF)
    mw = jnp.max(s_w, axis=-1, keepdims=True)
    p_w = jnp.exp(s_w - mw)
    ow4 = _dg(p_w.astype(BF16), vw) / jnp.sum(p_w, axis=-1, keepdims=True)

    gates = _sigmoid(g_ref[0] + gb_ref[...])
    out = jnp.zeros((qb, D_BRANCH), F32)
    for h in range(N_HEADS):
        rows = slice(h * qb, (h + 1) * qb)
        base = 128 + h * N_NSA_BRANCH
        mixed = (gates[:, base:base + 1] * oc4[rows]
                 + gates[:, base + 1:base + 2] * os4[rows]
                 + gates[:, base + 2:base + 3] * ow4[rows])
        out = out + hm[h] * mixed
    o_ref[0] = out


def _nsa_attention(proj, kc_rep, vc_rep, ks_rep, vs_rep, kw_rep, vw_rep, e_mat, m_mat, gate_b_pad, n_top):
    bsz, s, _ = proj.shape
    qb = Q_BLOCK
    n_cmp = kc_rep.shape[1]
    nwin = WINDOW // qb
    full = lambda arr: pl.BlockSpec((1,) + arr.shape[1:], lambda b, i: (b, 0, 0))
    win = [pl.BlockSpec((1, qb, D_BRANCH), (lambda b, i, j=j: (b, jnp.maximum(i - nwin + j, 0), 0)))
           for j in range(nwin + 1)]
    return pl.pallas_call(
        functools.partial(_nsa_kernel, n_top),
        grid=(bsz, s // qb),
        in_specs=[pl.BlockSpec((1, qb, D_BRANCH), lambda b, i: (b, i, 5)),
                  pl.BlockSpec((1, qb, D_BRANCH), lambda b, i: (b, i, 7)),
                  full(kc_rep), full(vc_rep), full(ks_rep), full(vs_rep)]
                 + win + win
                 + [pl.BlockSpec(e_mat.shape, lambda b, i: (0, 0, 0)),
                    pl.BlockSpec(m_mat.shape, lambda b, i: (0, 0)),
                    pl.BlockSpec((1, D_BRANCH), lambda b, i: (0, 0))],
        out_specs=pl.BlockSpec((1, qb, D_BRANCH), lambda b, i: (b, i, 0)),
        out_shape=jax.ShapeDtypeStruct((bsz, s, D_BRANCH), F32),
        scratch_shapes=[pltpu.VMEM((N_HEADS * qb, D_BRANCH), F32),
                        pltpu.VMEM((N_HEADS * qb, 1), F32),
                        pltpu.VMEM((N_HEADS * qb, 1), F32)],
        compiler_params=_params(("parallel", "arbitrary")),
        name="nsa_attention",
    )(proj, proj, kc_rep, vc_rep, ks_rep, vs_rep, *([kw_rep] * (nwin + 1)), *([vw_rep] * (nwin + 1)),
      e_mat, m_mat, gate_b_pad)


def _nsa_mixer(proj, cmp_pos, cmp_w1, cmp_w2, gate_b):
    bsz, s, _ = proj.shape
    n_row = s // CMP_STRIDE
    n_sel = s // SEL_BLOCK
    n_top = min(N_SELECT, n_sel)
    base = 6 * D_BRANCH
    col = lambda k: proj[:, :, base + k * HEAD_DIM: base + (k + 1) * HEAD_DIM]
    rk = col(0).reshape(bsz, n_row, CMP_STRIDE * HEAD_DIM)
    rv = col(1).reshape(bsz, n_row, CMP_STRIDE * HEAD_DIM)
    rep = lambda a: jnp.tile(a.astype(BF16), (1, 1, N_HEADS))
    ks_rep, vs_rep, kw_rep, vw_rep = rep(col(2)), rep(col(3)), rep(col(4)), rep(col(5))
    pos_flat = cmp_pos.reshape(2, 1, CMP_BLOCK * HEAD_DIM)
    w2_rep = jnp.tile(cmp_w2, (1, 1, N_HEADS))
    kc_rep, vc_rep = _nsa_compress(rk, rv, pos_flat, cmp_w1, w2_rep)

    tk = min(KV_TILE, s)
    key_blk = jnp.arange(s, dtype=jnp.int32) // SEL_BLOCK
    e_full = (jnp.arange(n_sel, dtype=jnp.int32)[:, None] == key_blk[None, :]).astype(BF16)
    e_mat = e_full.reshape(n_sel, s // tk, tk).transpose(1, 0, 2)
    ratio = SEL_BLOCK // CMP_STRIDE
    off = jnp.arange(n_row, dtype=jnp.int32)[:, None] - ratio * jnp.arange(n_sel, dtype=jnp.int32)[None, :]
    m_mat = jnp.where((off == -1) | (off == 3), 1.0, jnp.where((off >= 0) & (off <= 2), 2.0, 0.0)).astype(BF16)
    gate_b_pad = jnp.zeros((1, D_BRANCH), F32).at[0, 128:128 + N_NSA_BRANCH * N_HEADS].set(gate_b)
    return _nsa_attention(proj, kc_rep, vc_rep, ks_rep, vs_rep, kw_rep, vw_rep, e_mat, m_mat, gate_b_pad, n_top)


def _rwkv_kernel(pr_ref, pk_ref, pv_ref, pw_ref, mu_ref, w0_ref, wup_ref, a0_ref, aup_ref,
                 kk_ref, ka_ref, rk_ref, lng_ref, lnb_ref, o_ref, prev_ref, st_ref):
    nb, c, _ = pr_ref.shape
    n = nb * c

    @pl.when(pl.program_id(0) == 0)
    def _():
        prev_ref[...] = jnp.zeros_like(prev_ref)
        st_ref[...] = jnp.zeros_like(st_ref)

    row = lax.broadcasted_iota(jnp.int32, (n, 1), 0)
    col = lax.broadcasted_iota(jnp.int32, (1, n), 1)
    rowb = row // c
    colb = col // c

    def shifted(ref, k):
        x = ref[...].reshape(n, D_BRANCH)
        prev = pltpu.roll(x, 1, axis=0)
        for b in range(nb):
            prev = jnp.where(row == b * c, prev_ref[k, b:b + 1, :], prev)
        for b in range(nb):
            prev_ref[k, b:b + 1, :] = x[(b + 1) * c - 1:(b + 1) * c, :]
        return x + (prev - x) * mu_ref[k:k + 1, :]

    r = shifted(pr_ref, 0)
    k = shifted(pk_ref, 1)
    v = shifted(pv_ref, 2)
    xw = shifted(pw_ref, 3)

    ones_bd = _block_ones()
    wlog = w0_ref[...] + _dot3(jnp.tanh(xw), wup_ref[...])
    w = -_softplus(-wlog) - 0.5
    ld = -jnp.exp(w)
    a_lr = _sigmoid(a0_ref[...] + _dot3(xw, aup_ref[...]))
    kk = k * kk_ref[...]
    kk = kk / jnp.maximum(jnp.sqrt(_dot_lx(kk * kk, ones_bd)), 1e-12)
    k = k * (1.0 + (a_lr - 1.0) * ka_ref[...])
    a = -kk
    b = kk * a_lr

    same = rowb == colb
    tri = jnp.where(same & (col <= row), 1.0, 0.0).astype(BF16)
    allm = jnp.where(same, 1.0, 0.0).astype(BF16)
    cs = _dot_xl(tri, ld)
    tot = _dot_xl(allm, ld)
    a_t = a * jnp.exp(cs - ld)
    r_t = r * jnp.exp(cs)
    inv = jnp.exp(-cs)
    b_t = b * inv
    k_t = k * inv
    fin = jnp.exp(tot - cs)
    b_f = b * fin
    k_f = k * fin
    g_tot = jnp.exp(tot)

    hm = _head_masks()
    lhs = jnp.concatenate([b_t * hm[h] for h in range(N_HEADS)] + [k_t * hm[h] for h in range(N_HEADS)], axis=0)
    rhs = jnp.concatenate([a_t, r_t], axis=0)
    small = _dot3(lhs, rhs, 1, 1)
    upper = same & (row < col)
    upper_eq = same & (row <= col)
    eye = jnp.where(row == col, 1.0, 0.0)

    at_t = a_t.T
    rt_t = r_t.T
    v_t = v.T
    lane_b = [(colb == bb).astype(F32) for bb in range(nb)]
    x_t = jnp.zeros((D_BRANCH, n), F32)
    y_t = jnp.zeros((D_BRANCH, n), F32)
    for bb in range(nb):
        s_b = st_ref[bb]
        x_t = x_t + _dot3(s_b, at_t * lane_b[bb])
        y_t = y_t + _dot3(s_b, rt_t * lane_b[bb])

    u_rows = []
    y_rows = []
    for h in range(N_HEADS):
        ba = jnp.where(upper, small[h * n:(h + 1) * n, 0:n], 0.0)
        br = jnp.where(upper_eq, small[h * n:(h + 1) * n, n:2 * n], 0.0)
        ka = jnp.where(upper, small[(N_HEADS + h) * n:(N_HEADS + h + 1) * n, 0:n], 0.0)
        kr = jnp.where(upper_eq, small[(N_HEADS + h) * n:(N_HEADS + h + 1) * n, n:2 * n], 0.0)
        tinv = eye + ba
        pw = ba
        span = 2
        while span < c:
            pw = _dot3(pw, pw)
            tinv = tinv + _dot3(tinv, pw)
            span *= 2
        vh = v_t[h * HEAD_DIM:(h + 1) * HEAD_DIM, :]
        xh = x_t[h * HEAD_DIM:(h + 1) * HEAD_DIM, :] + _dot3(vh, ka)
        uh = _dot3(xh, tinv)
        u_rows.append(uh)
        y_rows.append(y_t[h * HEAD_DIM:(h + 1) * HEAD_DIM, :] + _dot3(uh, br) + _dot3(vh, kr))
    u_t = jnp.concatenate(u_rows, axis=0)
    y = jnp.concatenate(y_rows, axis=0).T

    bd = ones_bd.astype(F32)
    uv = jnp.concatenate([u_t, v_t], axis=1)
    for bb in range(nb):
        rmask = (rowb == bb).astype(F32)
        bk = jnp.concatenate([b_f * rmask, k_f * rmask], axis=0)
        g_row = g_tot[bb * c:bb * c + 1, :]
        st_ref[bb] = st_ref[bb] * g_row + bd * _dot3(uv, bk)

    mean_m = _block_ones(1.0 / HEAD_DIM)
    mean = _dot_lx(y, mean_m)
    yc = y - mean
    var = _dot_lx(yc * yc, mean_m)
    out = yc * lax.rsqrt(var + RWKV_GN_EPS) * lng_ref[...] + lnb_ref[...]
    bonus = _dot_lx(r * k * rk_ref[...], ones_bd) * v
    o_ref[...] = (out + bonus).reshape(nb, c, D_BRANCH)


def _rwkv_mixer(proj, mu, w0, w_up, a0, a_up, k_k, k_a, r_k, ln_g, ln_b):
    bsz, s, _ = proj.shape
    c = min(RWKV_CHUNK, s)
    mu_p = jnp.zeros((4, D_BRANCH), F32)
    mu_p = mu_p.at[0:3, :].set(mu[:3 * D_BRANCH].reshape(3, D_BRANCH))
    mu_p = mu_p.at[3, :2 * RWKV_RANK].set(mu[3 * D_BRANCH:])
    wup_p = jnp.zeros((D_BRANCH, D_BRANCH), F32).at[:RWKV_RANK, :].set(w_up)
    aup_p = jnp.zeros((D_BRANCH, D_BRANCH), F32).at[RWKV_RANK:2 * RWKV_RANK, :].set(a_up)
    row = lambda a: a.reshape(1, D_BRANCH)
    blk = lambda cidx: pl.BlockSpec((bsz, c, D_BRANCH), lambda i, cidx=cidx: (0, i, cidx))
    rowspec = pl.BlockSpec((1, D_BRANCH), lambda i: (0, 0))
    matspec = pl.BlockSpec((D_BRANCH, D_BRANCH), lambda i: (0, 0))
    return pl.pallas_call(
        _rwkv_kernel,
        grid=(s // c,),
        in_specs=[blk(8), blk(9), blk(10), blk(11),
                  pl.BlockSpec((4, D_BRANCH), lambda i: (0, 0)),
                  rowspec, matspec, rowspec, matspec, rowspec, rowspec, rowspec, rowspec, rowspec],
        out_specs=pl.BlockSpec((bsz, c, D_BRANCH), lambda i: (0, i, 0)),
        out_shape=jax.ShapeDtypeStruct((bsz, s, D_BRANCH), F32),
        scratch_shapes=[pltpu.VMEM((4, 8, D_BRANCH), F32), pltpu.VMEM((bsz, D_BRANCH, D_BRANCH), F32)],
        compiler_params=_params(("arbitrary",)),
        name="rwkv7",
    )(proj, proj, proj, proj, mu_p, row(w0), wup_p, row(a0), aup_p, row(k_k), row(k_a),
      row(r_k), row(ln_g), row(ln_b))


def _diff_kernel(lambda_init, q_ref, k_ref, v_ref, lam_ref, g_ref, o_ref, acc_ref, m_scr, l_scr):
    qb = q_ref.shape[1]
    tk = min(KV_TILE, k_ref.shape[1])
    i = pl.program_id(1)
    q0 = i * qb
    lane = lax.broadcasted_iota(jnp.int32, (1, D_BRANCH), 1)
    hm = _head_masks()
    q = q_ref[0] * (DIFF_QK_DIM ** -0.5)
    parts = []
    for h in range(N_HEADS):
        for cc in range(2):
            lo = h * HEAD_DIM + cc * DIFF_QK_DIM
            parts.append(q * ((lane >= lo) & (lane < lo + DIFF_QK_DIM)).astype(F32))
    qm = jnp.concatenate(parts, axis=0).astype(BF16)
    nrow = 2 * N_HEADS * qb
    t = q0 + lax.broadcasted_iota(jnp.int32, (qb, 1), 0)
    t8 = jnp.concatenate([t] * (2 * N_HEADS), axis=0)

    m_scr[...] = jnp.full(m_scr.shape, NEG_INF, F32)
    l_scr[...] = jnp.zeros(l_scr.shape, F32)
    acc_ref[...] = jnp.zeros(acc_ref.shape, F32)
    n_tiles = (q0 + qb + tk - 1) // tk

    def body(j, carry):
        start = pl.multiple_of(j * tk, tk)
        k = k_ref[0, pl.ds(start, tk), :]
        v = v_ref[0, pl.ds(start, tk), :]
        s = _dg(qm, k, 1, 1)
        kpos = start + lax.broadcasted_iota(jnp.int32, (1, tk), 1)
        s = jnp.where(kpos <= t8, s, NEG_INF)
        m_old = m_scr[...]
        m_new = jnp.maximum(m_old, jnp.max(s, axis=-1, keepdims=True))
        alpha = jnp.exp(m_old - m_new)
        p = jnp.exp(s - m_new)
        l_scr[...] = alpha * l_scr[...] + jnp.sum(p, axis=-1, keepdims=True)
        acc_ref[...] = alpha * acc_ref[...] + _dg(p.astype(BF16), v)
        m_scr[...] = m_new
        return carry

    lax.fori_loop(0, n_tiles, body, 0)
    o8 = acc_ref[...] / l_scr[...]

    lam = lam_ref[...]
    lam_full = (jnp.exp(jnp.sum(lam[0:1] * lam[1:2], axis=-1, keepdims=True))
                - jnp.exp(jnp.sum(lam[2:3] * lam[3:4], axis=-1, keepdims=True)) + lambda_init)
    out = jnp.zeros((qb, D_BRANCH), F32)
    for h in range(N_HEADS):
        p0 = o8[(2 * h) * qb:(2 * h + 1) * qb]
        p1 = o8[(2 * h + 1) * qb:(2 * h + 2) * qb]
        out = out + hm[h] * (p0 - lam_full * p1)
    ms = _dot_lx(out * out, _block_ones(1.0 / HEAD_DIM))
    o_ref[0] = out * lax.rsqrt(ms + DIFF_EPS) * g_ref[...] * (1.0 - lambda_init)
    del nrow


def _diff_mixer(proj, lam, subln_g, lambda_init):
    bsz, s, _ = proj.shape
    qb = Q_BLOCK
    k_bf = proj[:, :, 13 * D_BRANCH:14 * D_BRANCH].astype(BF16)
    v_bf = proj[:, :, 14 * D_BRANCH:15 * D_BRANCH].astype(BF16)
    g_rep = jnp.tile(subln_g, N_HEADS).reshape(1, D_BRANCH)
    full = pl.BlockSpec((1, s, D_BRANCH), lambda b, i: (b, 0, 0))
    return pl.pallas_call(
        functools.partial(_diff_kernel, lambda_init),
        grid=(bsz, s // qb),
        in_specs=[pl.BlockSpec((1, qb, D_BRANCH), lambda b, i: (b, i, 12)), full, full,
                  pl.BlockSpec(lam.shape, lambda b, i: (0, 0)),
                  pl.BlockSpec((1, D_BRANCH), lambda b, i: (0, 0))],
        out_specs=pl.BlockSpec((1, qb, D_BRANCH), lambda b, i: (b, i, 0)),
        out_shape=jax.ShapeDtypeStruct((bsz, s, D_BRANCH), F32),
        scratch_shapes=[pltpu.VMEM((2 * N_HEADS * qb, D_BRANCH), F32),
                        pltpu.VMEM((2 * N_HEADS * qb, 1), F32),
                        pltpu.VMEM((2 * N_HEADS * qb, 1), F32)],
        compiler_params=_params(("parallel", "arbitrary")),
        name="diff_attention",
    )(proj, k_bf, v_bf, lam, g_rep)


def _block_diag(w):
    n = w.shape[0] * w.shape[1]
    out = jnp.zeros((n, n), F32)
    for h in range(w.shape[0]):
        out = out.at[h * HEAD_DIM:(h + 1) * HEAD_DIM, h * HEAD_DIM:(h + 1) * HEAD_DIM].set(w[h])
    return out


def _layout_w_in(w):
    d = w.shape[0]
    nsa_end = 5 * D_BRANCH + D_BRANCH + 6 * HEAD_DIM + N_NSA_BRANCH * N_HEADS
    rwkv_end = nsa_end + 3 * D_BRANCH + 2 * RWKV_RANK
    pad1 = 8 * D_BRANCH - nsa_end
    pad2 = 12 * D_BRANCH - (rwkv_end + pad1)
    return jnp.concatenate([w[:, :nsa_end], jnp.zeros((d, pad1), F32), w[:, nsa_end:rwkv_end],
                            jnp.zeros((d, pad2), F32), w[:, rwkv_end:]], axis=1)


def kernel(x, norm_gain, w_in, w_out, final_gain, lru_conv_w, lru_conv_b, lru_wa, lru_ba, lru_wx, lru_bx, lru_lambda, nsa_cmp_pos, nsa_cmp_w1, nsa_cmp_w2, nsa_gate_b, rwkv_mu, rwkv_w0, rwkv_w_up, rwkv_a0, rwkv_a_up, rwkv_k_k, rwkv_k_a, rwkv_r_k, rwkv_ln_g, rwkv_ln_b, diff_lambda, diff_subln_g):
    bsz, s, d = x.shape
    depth = w_in.shape[0]
    t = bsz * s
    x2d = x.reshape(t, d)
    row = lambda a: a.reshape(1, -1)
    for l in range(depth):
        w_hi, w_lo = _split2(_layout_w_in(w_in[l]))
        proj2d = _in_proj(x2d, row(norm_gain[l]), w_hi, w_lo)
        proj = proj2d.reshape(bsz, s, N_PROJ_BLOCKS * D_BRANCH)
        o_lru = _lru_mixer(proj, lru_conv_w[l], row(lru_conv_b[l]), _block_diag(lru_wa[l]), row(lru_ba[l]),
                           _block_diag(lru_wx[l]), row(lru_bx[l]), row(lru_lambda[l]))
        o_nsa = _nsa_mixer(proj, nsa_cmp_pos[l], nsa_cmp_w1[l], nsa_cmp_w2[l], nsa_gate_b[l])
        o_rwkv = _rwkv_mixer(proj, rwkv_mu[l], rwkv_w0[l], rwkv_w_up[l], rwkv_a0[l], rwkv_a_up[l],
                             rwkv_k_k[l], rwkv_k_a[l], rwkv_r_k[l].reshape(-1), rwkv_ln_g[l], rwkv_ln_b[l])
        lambda_init = 0.8 - 0.6 * math.exp(-0.3 * l)
        o_diff = _diff_mixer(proj, diff_lambda[l], diff_subln_g[l], lambda_init)
        o_hi, o_lo = _split2(w_out[l])
        mixers = [o.reshape(t, D_BRANCH) for o in (o_lru, o_nsa, o_rwkv, o_diff)]
        x2d = _out_proj(mixers, proj2d, x2d, o_hi, o_lo, row(final_gain), final=(l == depth - 1))
    return x2d.reshape(bsz, s, d)
```

```python
import functools
import math

import jax
import jax.numpy as jnp
from jax import lax
from jax.experimental import pallas as pl
from jax.experimental.pallas import tpu as pltpu

F32 = jnp.float32
BF16 = jnp.bfloat16

N_MIX = 4
HEAD_DIM = 64
N_HEADS = 4
D_BRANCH = N_HEADS * HEAD_DIM
NORM_EPS = 1e-6
NEG_INF = -1e30
LRU_CONV = 4
LRU_C = 8.0
CMP_BLOCK = 32
CMP_STRIDE = 16
SEL_BLOCK = 64
N_SELECT = 16
WINDOW = 512
N_NSA_BRANCH = 3
FORCE_SCORE = 1e9
RWKV_RANK = 32
RWKV_GN_EPS = 64e-5
DIFF_QK_DIM = HEAD_DIM // 2
DIFF_EPS = 1e-5
N_PROJ_BLOCKS = 15
Q_BLOCK = 128
KV_TILE = 512
RWKV_CHUNK = 64
VMEM_LIMIT = 56 * 1024 * 1024
LOG2E = 1.4426950408889634
DIFF_Q_BLOCK = 256
DIFF_KV_TILE = 256
NSA_Q_BLOCK = 256
NSA_KV_TILE = 256
MASK_BIG = 1e30


def _dg(a, b, ca=1, cb=0):
    return lax.dot_general(a, b, (((ca,), (cb,)), ((), ())), preferred_element_type=F32)


def _split2(a):
    hi = a.astype(BF16)
    lo = (a - hi.astype(F32)).astype(BF16)
    return hi, lo


def _dot3(a, b, ca=1, cb=0):
    ah, al = _split2(a)
    bh, bl = _split2(b)
    return _dg(ah, bh, ca, cb) + (_dg(ah, bl, ca, cb) + _dg(al, bh, ca, cb))


def _dot3_pre(a, bh, bl, ca=1, cb=0):
    ah, al = _split2(a)
    return _dg(ah, bh, ca, cb) + (_dg(ah, bl, ca, cb) + _dg(al, bh, ca, cb))


def _dot_lx(a, b_exact, ca=1, cb=0):
    a1 = a.astype(BF16)
    r = a - a1.astype(F32)
    a2 = r.astype(BF16)
    a3 = (r - a2.astype(F32)).astype(BF16)
    return _dg(a1, b_exact, ca, cb) + (_dg(a2, b_exact, ca, cb) + _dg(a3, b_exact, ca, cb))


def _dot_xl(a_exact, b, ca=1, cb=0):
    b1 = b.astype(BF16)
    r = b - b1.astype(F32)
    b2 = r.astype(BF16)
    b3 = (r - b2.astype(F32)).astype(BF16)
    return _dg(a_exact, b1, ca, cb) + (_dg(a_exact, b2, ca, cb) + _dg(a_exact, b3, ca, cb))


def _head_masks(width=D_BRANCH, group=HEAD_DIM):
    lane = lax.broadcasted_iota(jnp.int32, (1, width), 1)
    return [((lane >= h * group) & (lane < (h + 1) * group)).astype(F32) for h in range(width // group)]


def _block_ones(scale=1.0):
    r = lax.broadcasted_iota(jnp.int32, (D_BRANCH, D_BRANCH), 0) // HEAD_DIM
    c = lax.broadcasted_iota(jnp.int32, (D_BRANCH, D_BRANCH), 1) // HEAD_DIM
    return jnp.where(r == c, scale, 0.0).astype(BF16)


def _sigmoid(x):
    return 1.0 / (1.0 + jnp.exp(-x))


def _softplus(x):
    return jnp.maximum(x, 0.0) + jnp.log1p(jnp.exp(-jnp.abs(x)))


def _params(sem):
    return pltpu.CompilerParams(dimension_semantics=sem, vmem_limit_bytes=VMEM_LIMIT)


def _proj_kernel(x_ref, g_ref, wh_ref, wl_ref, o_ref):
    x = x_ref[...]
    ms = jnp.mean(x * x, axis=-1, keepdims=True)
    h = x * lax.rsqrt(ms + NORM_EPS) * g_ref[...]
    o_ref[...] = _dot3_pre(h, wh_ref[...], wl_ref[...])


def _in_proj(x2d, gain, w_hi, w_lo, tm=512, tn=768):
    t, d = x2d.shape
    n = w_hi.shape[1]
    return pl.pallas_call(
        _proj_kernel,
        grid=(t // tm, n // tn),
        in_specs=[pl.BlockSpec((tm, d), lambda i, j: (i, 0)),
                  pl.BlockSpec((1, d), lambda i, j: (0, 0)),
                  pl.BlockSpec((d, tn), lambda i, j: (0, j)),
                  pl.BlockSpec((d, tn), lambda i, j: (0, j))],
        out_specs=pl.BlockSpec((tm, tn), lambda i, j: (i, j)),
        out_shape=jax.ShapeDtypeStruct((t, n), F32),
        compiler_params=_params(("parallel", "arbitrary")),
        name="in_proj",
    )(x2d, gain, w_hi, w_lo)


def _out_kernel(final, o1_ref, o2_ref, o3_ref, o4_ref, gate_ref, x_ref, wh_ref, wl_ref, fg_ref, y_ref):
    mix = jnp.concatenate([o1_ref[...], o2_ref[...], o3_ref[...], o4_ref[...]], axis=-1)
    g = gate_ref[...]
    z = mix * (g * _sigmoid(g))
    y = x_ref[...] + _dot3_pre(z, wh_ref[...], wl_ref[...])
    if final:
        ms = jnp.mean(y * y, axis=-1, keepdims=True)
        y = y * lax.rsqrt(ms + NORM_EPS) * fg_ref[...]
    y_ref[...] = y


def _out_proj(mixers, proj2d, x2d, w_hi, w_lo, final_gain, final, tm=512):
    t, d = x2d.shape
    dm = w_hi.shape[0]
    branch = pl.BlockSpec((tm, D_BRANCH), lambda i: (i, 0))
    return pl.pallas_call(
        functools.partial(_out_kernel, final),
        grid=(t // tm,),
        in_specs=[branch, branch, branch, branch,
                  pl.BlockSpec((tm, dm), lambda i: (i, 0)),
                  pl.BlockSpec((tm, d), lambda i: (i, 0)),
                  pl.BlockSpec((dm, d), lambda i: (0, 0)),
                  pl.BlockSpec((dm, d), lambda i: (0, 0)),
                  pl.BlockSpec((1, d), lambda i: (0, 0))],
        out_specs=pl.BlockSpec((tm, d), lambda i: (i, 0)),
        out_shape=jax.ShapeDtypeStruct((t, d), F32),
        compiler_params=_params(("parallel",)),
        name="out_proj",
    )(*mixers, proj2d, x2d, w_hi, w_lo, final_gain)


def _lru_kernel(u_ref, cw_ref, cb_ref, wah_ref, wal_ref, ba_ref, wxh_ref, wxl_ref, bx_ref, lam_ref,
                o_ref, ubuf, hcar):
    ts = u_ref.shape[1]

    @pl.when(pl.program_id(1) == 0)
    def _():
        ubuf[0:8, :] = jnp.zeros((8, D_BRANCH), F32)
        hcar[...] = jnp.zeros_like(hcar)

    u = u_ref[0]
    ubuf[8:8 + ts, :] = u
    xc = cb_ref[...] + cw_ref[0:1, :] * ubuf[5:5 + ts, :]
    for j in range(1, LRU_CONV):
        xc = xc + cw_ref[j:j + 1, :] * ubuf[5 + j:5 + j + ts, :]
    ubuf[0:8, :] = u[ts - 8:ts, :]

    r = _sigmoid(_dot3_pre(xc, wah_ref[...], wal_ref[...]) + ba_ref[...])
    gi = _sigmoid(_dot3_pre(xc, wxh_ref[...], wxl_ref[...]) + bx_ref[...])
    log_a = (-LRU_C) * r * _softplus(-lam_ref[...])
    a = jnp.exp(log_a)
    b = jnp.sqrt(jnp.tanh(-log_a) * (a * a + 1.0)) * (gi * xc)

    row = lax.broadcasted_iota(jnp.int32, (ts, 1), 0)
    d = 1
    while d < ts:
        keep = row >= d
        a_s = jnp.where(keep, pltpu.roll(a, d, axis=0), 1.0)
        b_s = jnp.where(keep, pltpu.roll(b, d, axis=0), 0.0)
        b = a * b_s + b
        a = a * a_s
        d *= 2
    h = b + a * hcar[...]
    o_ref[0] = h
    hcar[...] = h[ts - 1:ts, :]


def _lru_mixer(proj, conv_w, conv_b, wa_bd, ba, wx_bd, bx, lam, ts=512):
    bsz, s, _ = proj.shape
    wah, wal = _split2(wa_bd)
    wxh, wxl = _split2(wx_bd)
    row = pl.BlockSpec((1, D_BRANCH), lambda b, j: (0, 0))
    mat = pl.BlockSpec((D_BRANCH, D_BRANCH), lambda b, j: (0, 0))
    return pl.pallas_call(
        _lru_kernel,
        grid=(bsz, s // ts),
        in_specs=[pl.BlockSpec((1, ts, D_BRANCH), lambda b, j: (b, j, 4)),
                  pl.BlockSpec((LRU_CONV, D_BRANCH), lambda b, j: (0, 0)),
                  row, mat, mat, row, mat, mat, row, row],
        out_specs=pl.BlockSpec((1, ts, D_BRANCH), lambda b, j: (b, j, 0)),
        out_shape=jax.ShapeDtypeStruct((bsz, s, D_BRANCH), F32),
        scratch_shapes=[pltpu.VMEM((ts + 8, D_BRANCH), F32), pltpu.VMEM((1, D_BRANCH), F32)],
        compiler_params=_params(("parallel", "arbitrary")),
        name="rg_lru",
    )(proj, conv_w, conv_b, wah, wal, ba, wxh, wxl, bx, lam)


def _cmp_kernel(rk_ref, rv_ref, pos_ref, w1_ref, w2_ref, kch_ref, kcl_ref, vct_ref):
    half = (CMP_BLOCK // 2) * HEAD_DIM
    nrow = rk_ref.shape[1]
    outs = []
    for idx, r_ref in enumerate((rk_ref, rv_ref)):
        r = r_ref[0]
        w1 = w1_ref[idx]
        lo_half = _dot3(r, w1[0:half, :])
        hi_half = _dot3(r, w1[half:2 * half, :])
        bias = _dot3(jnp.broadcast_to(pos_ref[idx], (8, 2 * half)), w1)[0:1, :]
        hid = lo_half + pltpu.roll(hi_half, nrow - 1, axis=0) + bias
        outs.append(_dot3(jax.nn.gelu(hid), w2_ref[idx]))
    kc_hi, kc_lo = _split2(outs[0])
    kch_ref[0] = kc_hi
    kcl_ref[0] = kc_lo
    vct_ref[0] = outs[1].T.astype(BF16)


def _nsa_compress(rk, rv, pos_flat, w1, w2_pair):
    bsz, nrow, width = rk.shape
    blk = pl.BlockSpec((1, nrow, width), lambda b: (b, 0, 0))
    out = pl.BlockSpec((1, nrow, 128), lambda b: (b, 0, 0))
    return pl.pallas_call(
        _cmp_kernel,
        grid=(bsz,),
        in_specs=[blk, blk,
                  pl.BlockSpec(pos_flat.shape, lambda b: (0, 0, 0)),
                  pl.BlockSpec(w1.shape, lambda b: (0, 0, 0)),
                  pl.BlockSpec(w2_pair.shape, lambda b: (0, 0, 0))],
        out_specs=[out, out, pl.BlockSpec((1, 128, nrow), lambda b: (b, 0, 0))],
        out_shape=[jax.ShapeDtypeStruct((bsz, nrow, 128), BF16)] * 2
                  + [jax.ShapeDtypeStruct((bsz, 128, nrow), BF16)],
        compiler_params=_params(("parallel",)),
        name="nsa_compress",
    )(rk, rv, pos_flat, w1, w2_pair)


def _nsa_kernel(n_top, q_ref, g_ref, kch_ref, kcl_ref, vct_ref, ks_ref, vs_ref, kw_ref, vw_ref,
                mt_ref, gb_ref, o_ref, acc_ref, m_ref):
    qb = q_ref.shape[1]
    n_cmp = kch_ref.shape[1]
    n_sel = mt_ref.shape[0]
    tk = min(NSA_KV_TILE, vs_ref.shape[1])
    rep = tk // 128
    q0 = pl.program_id(1) * qb
    lane = lax.broadcasted_iota(jnp.int32, (1, 128), 1)
    q = q_ref[0] * (HEAD_DIM ** -0.5)
    parts = []
    for h in range(N_HEADS):
        pair, half = divmod(h, 2)
        keep = (lane >= half * HEAD_DIM) & (lane < (half + 1) * HEAD_DIM)
        parts.append(jnp.where(keep, q[:, pair * 128:(pair + 1) * 128], 0.0))
    q4 = jnp.concatenate(parts, axis=0)
    q4_log2 = (q4 * LOG2E).astype(BF16)
    t = q0 + lax.broadcasted_iota(jnp.int32, (qb, 1), 0)
    t4 = jnp.concatenate([t] * N_HEADS, axis=0)
    t_row = q0 + lax.broadcasted_iota(jnp.int32, (1, qb), 1)
    t4_row = jnp.concatenate([t_row] * N_HEADS, axis=1)

    q4h, q4l = _split2(q4)
    kch = kch_ref[0]
    sct = _dg(kch, q4h, 1, 1) + (_dg(kch, q4l, 1, 1) + _dg(kcl_ref[0], q4h, 1, 1))
    cend = lax.broadcasted_iota(jnp.int32, (n_cmp, 1), 0) * CMP_STRIDE + (CMP_BLOCK - 1)
    valid = cend <= t4_row
    sct = jnp.where(valid, sct, NEG_INF)
    e = jnp.exp(sct - jnp.max(sct, axis=0, keepdims=True))
    p_ct = e / jnp.sum(e, axis=0, keepdims=True)
    anyv = (t4_row >= CMP_BLOCK - 1).astype(F32)
    oc4 = (_dg(vct_ref[0], p_ct.astype(BF16)) * anyv).T
    p_cv = jnp.where(valid, p_ct, 0.0)
    imp_t = p_cv[:, 0:qb] + p_cv[:, qb:2 * qb] + p_cv[:, 2 * qb:3 * qb] + p_cv[:, 3 * qb:4 * qb]
    pslc_t = _dot_xl(mt_ref[...], imp_t)

    blk = lax.broadcasted_iota(jnp.int32, (n_sel, 1), 0)
    blk_f = blk.astype(F32)
    cur = t_row // SEL_BLOCK
    forced = (blk == 0) | (blk == cur) | (blk == cur - 1)
    score = jnp.where(blk > cur, -1.0, jnp.where(forced, FORCE_SCORE, pslc_t))
    sel_t = jnp.zeros((n_sel, qb), F32)
    for _ in range(n_top):
        best = jnp.max(score, axis=0, keepdims=True)
        idx = jnp.min(jnp.where(score == best, blk_f, float(n_sel)), axis=0, keepdims=True)
        pick = blk_f == idx
        sel_t = jnp.where(pick, 1.0, sel_t)
        score = jnp.where(pick, -3e38, score)
    notsel = (1.0 - sel_t).T.astype(BF16)
    lhs = jnp.concatenate([q4_log2, jnp.concatenate([notsel] * N_HEADS, axis=0)], axis=1)

    m_ref[...] = jnp.full(m_ref.shape, NEG_INF, F32)
    acc_ref[...] = jnp.zeros(acc_ref.shape, F32)

    def tile(j, masked):
        start = pl.multiple_of(j * tk, tk)
        s = _dg(lhs, ks_ref[0, :, pl.ds(start, tk)])
        if masked:
            s = jnp.where((start + lax.broadcasted_iota(jnp.int32, (1, tk), 1)) <= t4, s, NEG_INF)
        m_old = m_ref[...]
        m_new = jnp.maximum(m_old, jnp.max(s, axis=-1, keepdims=True))
        p = jnp.exp2(s - jnp.concatenate([m_new] * rep, axis=1))
        acc_ref[...] = jnp.exp2(m_old - m_new) * acc_ref[...] + _dg(p.astype(BF16), vs_ref[0, pl.ds(start, tk), :])
        m_ref[...] = m_new

    n_full = q0 // tk

    def body(j, carry):
        tile(j, False)
        return carry

    lax.fori_loop(0, n_full, body, 0)
    for u in range(max(qb // tk, 1)):
        tile(n_full + u, True)
    lsel = _lane_sum_selector()
    acc = acc_ref[...]
    os4 = acc / _dot_lx(acc, lsel)

    nw = qb + WINDOW
    start_w = pl.multiple_of(jnp.maximum(q0 - WINDOW, 0), 128)
    s_w = _dg(q4_log2, kw_ref[0, :, pl.ds(start_w, nw)])
    wpos = start_w + lax.broadcasted_iota(jnp.int32, (1, nw), 1)
    s_w = jnp.where((wpos <= t4) & (wpos > t4 - WINDOW), s_w, NEG_INF)
    p_w = jnp.exp2(s_w - jnp.max(s_w, axis=-1, keepdims=True))
    aw = _dg(p_w.astype(BF16), vw_ref[0, pl.ds(start_w, nw), :])
    ow4 = aw / _dot_lx(aw, lsel)

    gates = _sigmoid(g_ref[0] + gb_ref[...])
    out = jnp.zeros((qb, D_BRANCH), F32)
    for h in range(N_HEADS):
        rows = slice(h * qb, (h + 1) * qb)
        base = 128 + h * N_NSA_BRANCH
        mixed = (gates[:, base:base + 1] * oc4[rows]
                 + gates[:, base + 1:base + 2] * os4[rows]
                 + gates[:, base + 2:base + 3] * ow4[rows])
        out = out + _dot_lx(mixed, _placement(h))
    o_ref[0] = out


def _nsa_mixer(proj, cmp_pos, cmp_w1, cmp_w2, gate_b):
    bsz, s, _ = proj.shape
    n_row = s // CMP_STRIDE
    n_sel = s // SEL_BLOCK
    n_top = min(N_SELECT, n_sel)
    qb = min(NSA_Q_BLOCK, s)
    base = 6 * D_BRANCH
    col = lambda k: proj[:, :, base + k * HEAD_DIM: base + (k + 1) * HEAD_DIM]
    rk = col(0).reshape(bsz, n_row, CMP_STRIDE * HEAD_DIM)
    rv = col(1).reshape(bsz, n_row, CMP_STRIDE * HEAD_DIM)
    pos_flat = cmp_pos.reshape(2, 1, CMP_BLOCK * HEAD_DIM)
    w2_pair = jnp.stack([jnp.concatenate([cmp_w2[0], cmp_w2[0]], axis=1),
                         jnp.concatenate([cmp_w2[1], jnp.zeros_like(cmp_w2[1])], axis=1)])
    kc_hi, kc_lo, vc_t = _nsa_compress(rk, rv, pos_flat, cmp_w1, w2_pair)

    key_blk = jnp.arange(s, dtype=jnp.int32) // SEL_BLOCK
    penalty = jnp.where(jnp.arange(n_sel, dtype=jnp.int32)[:, None] == key_blk[None, :], -MASK_BIG, 0.0)
    ks_t = jnp.swapaxes(col(2), 1, 2)
    ks_aug = jnp.concatenate([ks_t, ks_t, jnp.broadcast_to(penalty, (bsz, n_sel, s))], axis=1).astype(BF16)
    kw_t = jnp.swapaxes(col(4), 1, 2)
    kw_pair = jnp.concatenate([kw_t, kw_t], axis=1).astype(BF16)
    vs1 = _with_ones_lane(col(3))
    vw1 = _with_ones_lane(col(5))
    ratio = SEL_BLOCK // CMP_STRIDE
    off = jnp.arange(n_row, dtype=jnp.int32)[None, :] - ratio * jnp.arange(n_sel, dtype=jnp.int32)[:, None]
    m_t = jnp.where((off == -1) | (off == 3), 1.0, jnp.where((off >= 0) & (off <= 2), 2.0, 0.0)).astype(BF16)
    gate_b_pad = jnp.zeros((1, D_BRANCH), F32).at[0, 128:128 + N_NSA_BRANCH * N_HEADS].set(gate_b)
    full = lambda arr: pl.BlockSpec((1,) + arr.shape[1:], lambda b, i: (b, 0, 0))
    return pl.pallas_call(
        functools.partial(_nsa_kernel, n_top),
        grid=(bsz, s // qb),
        in_specs=[pl.BlockSpec((1, qb, D_BRANCH), lambda b, i: (b, i, 5)),
                  pl.BlockSpec((1, qb, D_BRANCH), lambda b, i: (b, i, 7)),
                  full(kc_hi), full(kc_lo), full(vc_t), full(ks_aug), full(vs1), full(kw_pair), full(vw1),
                  pl.BlockSpec(m_t.shape, lambda b, i: (0, 0)),
                  pl.BlockSpec((1, D_BRANCH), lambda b, i: (0, 0))],
        out_specs=pl.BlockSpec((1, qb, D_BRANCH), lambda b, i: (b, i, 0)),
        out_shape=jax.ShapeDtypeStruct((bsz, s, D_BRANCH), F32),
        scratch_shapes=[pltpu.VMEM((N_HEADS * qb, 128), F32), pltpu.VMEM((N_HEADS * qb, 128), F32)],
        compiler_params=_params(("parallel", "arbitrary")),
        name="nsa_attention",
    )(proj, proj, kc_hi, kc_lo, vc_t, ks_aug, vs1, kw_pair, vw1, m_t, gate_b_pad)


def _rwkv_kernel(pr_ref, pk_ref, pv_ref, pw_ref, mu_ref, w0_ref, wup_ref, a0_ref, aup_ref,
                 kk_ref, ka_ref, rk_ref, lng_ref, lnb_ref, o_ref, prev_ref, st_ref):
    nb, c, _ = pr_ref.shape
    n = nb * c

    @pl.when(pl.program_id(0) == 0)
    def _():
        prev_ref[...] = jnp.zeros_like(prev_ref)
        st_ref[...] = jnp.zeros_like(st_ref)

    row = lax.broadcasted_iota(jnp.int32, (n, 1), 0)
    col = lax.broadcasted_iota(jnp.int32, (1, n), 1)
    rowb = row // c
    colb = col // c

    def shifted(ref, k):
        x = ref[...].reshape(n, D_BRANCH)
        prev = pltpu.roll(x, 1, axis=0)
        for b in range(nb):
            prev = jnp.where(row == b * c, prev_ref[k, b:b + 1, :], prev)
        for b in range(nb):
            prev_ref[k, b:b + 1, :] = x[(b + 1) * c - 1:(b + 1) * c, :]
        return x + (prev - x) * mu_ref[k:k + 1, :]

    r = shifted(pr_ref, 0)
    k = shifted(pk_ref, 1)
    v = shifted(pv_ref, 2)
    xw = shifted(pw_ref, 3)

    ones_bd = _block_ones()
    wlog = w0_ref[...] + _dot3(jnp.tanh(xw), wup_ref[...])
    w = -_softplus(-wlog) - 0.5
    ld = -jnp.exp(w)
    a_lr = _sigmoid(a0_ref[...] + _dot3(xw, aup_ref[...]))
    kk = k * kk_ref[...]
    kk = kk / jnp.maximum(jnp.sqrt(_dot_lx(kk * kk, ones_bd)), 1e-12)
    k = k * (1.0 + (a_lr - 1.0) * ka_ref[...])
    a = -kk
    b = kk * a_lr

    same = rowb == colb
    tri = jnp.where(same & (col <= row), 1.0, 0.0).astype(BF16)
    allm = jnp.where(same, 1.0, 0.0).astype(BF16)
    cs = _dot_xl(tri, ld)
    tot = _dot_xl(allm, ld)
    a_t = a * jnp.exp(cs - ld)
    r_t = r * jnp.exp(cs)
    inv = jnp.exp(-cs)
    b_t = b * inv
    k_t = k * inv
    fin = jnp.exp(tot - cs)
    b_f = b * fin
    k_f = k * fin
    g_tot = jnp.exp(tot)

    hm = _head_masks()
    lhs = jnp.concatenate([b_t * hm[h] for h in range(N_HEADS)] + [k_t * hm[h] for h in range(N_HEADS)], axis=0)
    rhs = jnp.concatenate([a_t, r_t], axis=0)
    small = _dot3(lhs, rhs, 1, 1)
    upper = same & (row < col)
    upper_eq = same & (row <= col)
    eye = jnp.where(row == col, 1.0, 0.0)

    at_t = a_t.T
    rt_t = r_t.T
    v_t = v.T
    lane_b = [(colb == bb).astype(F32) for bb in range(nb)]
    x_t = jnp.zeros((D_BRANCH, n), F32)
    y_t = jnp.zeros((D_BRANCH, n), F32)
    for bb in range(nb):
        s_b = st_ref[bb]
        x_t = x_t + _dot3(s_b, at_t * lane_b[bb])
        y_t = y_t + _dot3(s_b, rt_t * lane_b[bb])

    u_rows = []
    y_rows = []
    for h in range(N_HEADS):
        ba = jnp.where(upper, small[h * n:(h + 1) * n, 0:n], 0.0)
        br = jnp.where(upper_eq, small[h * n:(h + 1) * n, n:2 * n], 0.0)
        ka = jnp.where(upper, small[(N_HEADS + h) * n:(N_HEADS + h + 1) * n, 0:n], 0.0)
        kr = jnp.where(upper_eq, small[(N_HEADS + h) * n:(N_HEADS + h + 1) * n, n:2 * n], 0.0)
        tinv = eye + ba
        pw = ba
        span = 2
        while span < c:
            pw = _dot3(pw, pw)
            tinv = tinv + _dot3(tinv, pw)
            span *= 2
        vh = v_t[h * HEAD_DIM:(h + 1) * HEAD_DIM, :]
        xh = x_t[h * HEAD_DIM:(h + 1) * HEAD_DIM, :] + _dot3(vh, ka)
        uh = _dot3(xh, tinv)
        u_rows.append(uh)
        y_rows.append(y_t[h * HEAD_DIM:(h + 1) * HEAD_DIM, :] + _dot3(uh, br) + _dot3(vh, kr))
    u_t = jnp.concatenate(u_rows, axis=0)
    y = jnp.concatenate(y_rows, axis=0).T

    bd = ones_bd.astype(F32)
    uv = jnp.concatenate([u_t, v_t], axis=1)
    for bb in range(nb):
        rmask = (rowb == bb).astype(F32)
        bk = jnp.concatenate([b_f * rmask, k_f * rmask], axis=0)
        g_row = g_tot[bb * c:bb * c + 1, :]
        st_ref[bb] = st_ref[bb] * g_row + bd * _dot3(uv, bk)

    mean_m = _block_ones(1.0 / HEAD_DIM)
    mean = _dot_lx(y, mean_m)
    yc = y - mean
    var = _dot_lx(yc * yc, mean_m)
    out = yc * lax.rsqrt(var + RWKV_GN_EPS) * lng_ref[...] + lnb_ref[...]
    bonus = _dot_lx(r * k * rk_ref[...], ones_bd) * v
    o_ref[...] = (out + bonus).reshape(nb, c, D_BRANCH)


def _rwkv_mixer(proj, mu, w0, w_up, a0, a_up, k_k, k_a, r_k, ln_g, ln_b):
    bsz, s, _ = proj.shape
    c = min(RWKV_CHUNK, s)
    mu_p = jnp.zeros((4, D_BRANCH), F32)
    mu_p = mu_p.at[0:3, :].set(mu[:3 * D_BRANCH].reshape(3, D_BRANCH))
    mu_p = mu_p.at[3, :2 * RWKV_RANK].set(mu[3 * D_BRANCH:])
    wup_p = jnp.zeros((D_BRANCH, D_BRANCH), F32).at[:RWKV_RANK, :].set(w_up)
    aup_p = jnp.zeros((D_BRANCH, D_BRANCH), F32).at[RWKV_RANK:2 * RWKV_RANK, :].set(a_up)
    row = lambda a: a.reshape(1, D_BRANCH)
    blk = lambda cidx: pl.BlockSpec((bsz, c, D_BRANCH), lambda i, cidx=cidx: (0, i, cidx))
    rowspec = pl.BlockSpec((1, D_BRANCH), lambda i: (0, 0))
    matspec = pl.BlockSpec((D_BRANCH, D_BRANCH), lambda i: (0, 0))
    return pl.pallas_call(
        _rwkv_kernel,
        grid=(s // c,),
        in_specs=[blk(8), blk(9), blk(10), blk(11),
                  pl.BlockSpec((4, D_BRANCH), lambda i: (0, 0)),
                  rowspec, matspec, rowspec, matspec, rowspec, rowspec, rowspec, rowspec, rowspec],
        out_specs=pl.BlockSpec((bsz, c, D_BRANCH), lambda i: (0, i, 0)),
        out_shape=jax.ShapeDtypeStruct((bsz, s, D_BRANCH), F32),
        scratch_shapes=[pltpu.VMEM((4, 8, D_BRANCH), F32), pltpu.VMEM((bsz, D_BRANCH, D_BRANCH), F32)],
        compiler_params=_params(("arbitrary",)),
        name="rwkv7",
    )(proj, proj, proj, proj, mu_p, row(w0), wup_p, row(a0), aup_p, row(k_k), row(k_a),
      row(r_k), row(ln_g), row(ln_b))


def _lane_sum_selector():
    r = lax.broadcasted_iota(jnp.int32, (128, 128), 0)
    return jnp.where(r == HEAD_DIM, 1.0, 0.0).astype(BF16)


def _placement(h):
    r = lax.broadcasted_iota(jnp.int32, (128, D_BRANCH), 0)
    c = lax.broadcasted_iota(jnp.int32, (128, D_BRANCH), 1)
    return jnp.where((r < HEAD_DIM) & (c == r + h * HEAD_DIM), 1.0, 0.0).astype(BF16)


def _diff_kernel(lambda_init, q_ref, k_ref, v_ref, lam_ref, g_ref, o_ref, acc_ref, m_ref):
    qb = q_ref.shape[1]
    tk = min(DIFF_KV_TILE, k_ref.shape[3])
    rep = tk // 128
    q0 = pl.program_id(1) * qb
    lane = lax.broadcasted_iota(jnp.int32, (1, 128), 1)
    q = q_ref[0] * (DIFF_QK_DIM ** -0.5 * LOG2E)
    qs = []
    for h in range(N_HEADS):
        pair, half = divmod(h, 2)
        qp = q[:, pair * 128:(pair + 1) * 128]
        both = []
        for cc in range(2):
            lo = half * HEAD_DIM + cc * DIFF_QK_DIM
            both.append(jnp.where((lane >= lo) & (lane < lo + DIFF_QK_DIM), qp, 0.0).astype(BF16))
        qs.append(jnp.concatenate(both, axis=0))
    t = q0 + lax.broadcasted_iota(jnp.int32, (qb, 1), 0)
    t2 = jnp.concatenate([t, t], axis=0)
    m_ref[...] = jnp.full(m_ref.shape, NEG_INF, F32)
    acc_ref[...] = jnp.zeros(acc_ref.shape, F32)

    def tile(j, masked):
        start = pl.multiple_of(j * tk, tk)
        if masked:
            allow = (start + lax.broadcasted_iota(jnp.int32, (1, tk), 1)) <= t2
        for h in range(N_HEADS):
            kt = k_ref[0, h // 2, :, pl.ds(start, tk)]
            v = v_ref[0, h, pl.ds(start, tk), :]
            s = _dg(qs[h], kt)
            if masked:
                s = jnp.where(allow, s, NEG_INF)
            m_old = m_ref[h]
            m_new = jnp.maximum(m_old, jnp.max(s, axis=-1, keepdims=True))
            p = jnp.exp2(s - jnp.concatenate([m_new] * rep, axis=1))
            acc_ref[h] = jnp.exp2(m_old - m_new) * acc_ref[h] + _dg(p.astype(BF16), v)
            m_ref[h] = m_new

    n_full = q0 // tk

    def body(j, carry):
        tile(j, False)
        return carry

    lax.fori_loop(0, n_full, body, 0)
    for u in range(max(qb // tk, 1)):
        tile(n_full + u, True)

    lam = lam_ref[...]
    lam_full = (jnp.exp(jnp.sum(lam[0:1] * lam[1:2], axis=-1, keepdims=True))
                - jnp.exp(jnp.sum(lam[2:3] * lam[3:4], axis=-1, keepdims=True)) + lambda_init)
    lsel = _lane_sum_selector()
    out = jnp.zeros((qb, D_BRANCH), F32)
    for h in range(N_HEADS):
        a = acc_ref[h]
        a = a / _dot_lx(a, lsel)
        d = a[0:qb] - lam_full * a[qb:2 * qb]
        out = out + _dot_lx(d, _placement(h))
    ms = _dot_lx(out * out, _block_ones(1.0 / HEAD_DIM))
    o_ref[0] = out * lax.rsqrt(ms + DIFF_EPS) * g_ref[...] * (1.0 - lambda_init)


def _with_ones_lane(v):
    pad = jnp.zeros(v.shape[:-1] + (128 - HEAD_DIM,), v.dtype).at[..., 0].set(1.0)
    return jnp.concatenate([v, pad], axis=-1).astype(BF16)


def _diff_mixer(proj, lam, subln_g, lambda_init):
    bsz, s, _ = proj.shape
    qb = min(DIFF_Q_BLOCK, s)
    kt_pairs = proj[:, :, 13 * D_BRANCH:14 * D_BRANCH].reshape(bsz, s, 2, 128).transpose(0, 2, 3, 1).astype(BF16)
    v_heads = proj[:, :, 14 * D_BRANCH:15 * D_BRANCH].reshape(bsz, s, N_HEADS, HEAD_DIM).transpose(0, 2, 1, 3)
    v_heads = _with_ones_lane(v_heads)
    g_rep = jnp.tile(subln_g, N_HEADS).reshape(1, D_BRANCH)
    return pl.pallas_call(
        functools.partial(_diff_kernel, lambda_init),
        grid=(bsz, s // qb),
        in_specs=[pl.BlockSpec((1, qb, D_BRANCH), lambda b, i: (b, i, 12)),
                  pl.BlockSpec((1, 2, 128, s), lambda b, i: (b, 0, 0, 0)),
                  pl.BlockSpec((1, N_HEADS, s, 128), lambda b, i: (b, 0, 0, 0)),
                  pl.BlockSpec(lam.shape, lambda b, i: (0, 0)),
                  pl.BlockSpec((1, D_BRANCH), lambda b, i: (0, 0))],
        out_specs=pl.BlockSpec((1, qb, D_BRANCH), lambda b, i: (b, i, 0)),
        out_shape=jax.ShapeDtypeStruct((bsz, s, D_BRANCH), F32),
        scratch_shapes=[pltpu.VMEM((N_HEADS, 2 * qb, 128), F32),
                        pltpu.VMEM((N_HEADS, 2 * qb, 128), F32)],
        compiler_params=_params(("parallel", "arbitrary")),
        name="diff_attention",
    )(proj, kt_pairs, v_heads, lam, g_rep)


def _block_diag(w):
    n = w.shape[0] * w.shape[1]
    out = jnp.zeros((n, n), F32)
    for h in range(w.shape[0]):
        out = out.at[h * HEAD_DIM:(h + 1) * HEAD_DIM, h * HEAD_DIM:(h + 1) * HEAD_DIM].set(w[h])
    return out


def _layout_w_in(w):
    d = w.shape[0]
    nsa_end = 5 * D_BRANCH + D_BRANCH + 6 * HEAD_DIM + N_NSA_BRANCH * N_HEADS
    rwkv_end = nsa_end + 3 * D_BRANCH + 2 * RWKV_RANK
    pad1 = 8 * D_BRANCH - nsa_end
    pad2 = 12 * D_BRANCH - (rwkv_end + pad1)
    return jnp.concatenate([w[:, :nsa_end], jnp.zeros((d, pad1), F32), w[:, nsa_end:rwkv_end],
                            jnp.zeros((d, pad2), F32), w[:, rwkv_end:]], axis=1)


def kernel(x, norm_gain, w_in, w_out, final_gain, lru_conv_w, lru_conv_b, lru_wa, lru_ba, lru_wx, lru_bx, lru_lambda, nsa_cmp_pos, nsa_cmp_w1, nsa_cmp_w2, nsa_gate_b, rwkv_mu, rwkv_w0, rwkv_w_up, rwkv_a0, rwkv_a_up, rwkv_k_k, rwkv_k_a, rwkv_r_k, rwkv_ln_g, rwkv_ln_b, diff_lambda, diff_subln_g):
    bsz, s, d = x.shape
    depth = w_in.shape[0]
    t = bsz * s
    x2d = x.reshape(t, d)
    row = lambda a: a.reshape(1, -1)
    for l in range(depth):
        w_hi, w_lo = _split2(_layout_w_in(w_in[l]))
        proj2d = _in_proj(x2d, row(norm_gain[l]), w_hi, w_lo)
        proj = proj2d.reshape(bsz, s, N_PROJ_BLOCKS * D_BRANCH)
        o_lru = _lru_mixer(proj, lru_conv_w[l], row(lru_conv_b[l]), _block_diag(lru_wa[l]), row(lru_ba[l]),
                           _block_diag(lru_wx[l]), row(lru_bx[l]), row(lru_lambda[l]))
        o_nsa = _nsa_mixer(proj, nsa_cmp_pos[l], nsa_cmp_w1[l], nsa_cmp_w2[l], nsa_gate_b[l])
        o_rwkv = _rwkv_mixer(proj, rwkv_mu[l], rwkv_w0[l], rwkv_w_up[l], rwkv_a0[l], rwkv_a_up[l],
                             rwkv_k_k[l], rwkv_k_a[l], rwkv_r_k[l].reshape(-1), rwkv_ln_g[l], rwkv_ln_b[l])
        lambda_init = 0.8 - 0.6 * math.exp(-0.3 * l)
        o_diff = _diff_mixer(proj, diff_lambda[l], diff_subln_g[l], lambda_init)
        o_hi, o_lo = _split2(w_out[l])
        mixers = [o.reshape(t, D_BRANCH) for o in (o_lru, o_nsa, o_rwkv, o_diff)]
        x2d = _out_proj(mixers, proj2d, x2d, o_hi, o_lo, row(final_gain), final=(l == depth - 1))
    return x2d.reshape(bsz, s, d)
```

```python
import functools
import math

import jax
import jax.numpy as jnp
from jax import lax
from jax.experimental import pallas as pl
from jax.experimental.pallas import tpu as pltpu

F32 = jnp.float32
BF16 = jnp.bfloat16

N_MIX = 4
HEAD_DIM = 64
N_HEADS = 4
D_BRANCH = N_HEADS * HEAD_DIM
NORM_EPS = 1e-6
NEG_INF = -1e30
LRU_CONV = 4
LRU_C = 8.0
CMP_BLOCK = 32
CMP_STRIDE = 16
SEL_BLOCK = 64
N_SELECT = 16
WINDOW = 512
N_NSA_BRANCH = 3
FORCE_SCORE = 1e9
RWKV_RANK = 32
RWKV_GN_EPS = 64e-5
DIFF_QK_DIM = HEAD_DIM // 2
DIFF_EPS = 1e-5
N_PROJ_BLOCKS = 15
Q_BLOCK = 128
KV_TILE = 512
RWKV_CHUNK = 64
RWKV_BLOCK = 256
VMEM_LIMIT = 56 * 1024 * 1024
LOG2E = 1.4426950408889634
DIFF_Q_BLOCK = 256
DIFF_KV_TILE = 256
NSA_Q_BLOCK = 256
NSA_KV_TILE = 256
MASK_BIG = 1e30


def _dg(a, b, ca=1, cb=0):
    return lax.dot_general(a, b, (((ca,), (cb,)), ((), ())), preferred_element_type=F32)


def _split2(a):
    hi = a.astype(BF16)
    lo = (a - hi.astype(F32)).astype(BF16)
    return hi, lo


def _dot3(a, b, ca=1, cb=0):
    ah, al = _split2(a)
    bh, bl = _split2(b)
    return _dg(ah, bh, ca, cb) + (_dg(ah, bl, ca, cb) + _dg(al, bh, ca, cb))


def _dot3_pre(a, bh, bl, ca=1, cb=0):
    ah, al = _split2(a)
    return _dg(ah, bh, ca, cb) + (_dg(ah, bl, ca, cb) + _dg(al, bh, ca, cb))


def _dot_lx(a, b_exact, ca=1, cb=0):
    a1 = a.astype(BF16)
    r = a - a1.astype(F32)
    a2 = r.astype(BF16)
    a3 = (r - a2.astype(F32)).astype(BF16)
    return _dg(a1, b_exact, ca, cb) + (_dg(a2, b_exact, ca, cb) + _dg(a3, b_exact, ca, cb))


def _dot_xl(a_exact, b, ca=1, cb=0):
    b1 = b.astype(BF16)
    r = b - b1.astype(F32)
    b2 = r.astype(BF16)
    b3 = (r - b2.astype(F32)).astype(BF16)
    return _dg(a_exact, b1, ca, cb) + (_dg(a_exact, b2, ca, cb) + _dg(a_exact, b3, ca, cb))


def _head_masks(width=D_BRANCH, group=HEAD_DIM):
    lane = lax.broadcasted_iota(jnp.int32, (1, width), 1)
    return [((lane >= h * group) & (lane < (h + 1) * group)).astype(F32) for h in range(width // group)]


def _block_ones(scale=1.0):
    r = lax.broadcasted_iota(jnp.int32, (D_BRANCH, D_BRANCH), 0) // HEAD_DIM
    c = lax.broadcasted_iota(jnp.int32, (D_BRANCH, D_BRANCH), 1) // HEAD_DIM
    return jnp.where(r == c, scale, 0.0).astype(BF16)


def _sigmoid(x):
    return 1.0 / (1.0 + jnp.exp(-x))


def _softplus(x):
    return jnp.maximum(x, 0.0) + jnp.log1p(jnp.exp(-jnp.abs(x)))


def _params(sem):
    return pltpu.CompilerParams(dimension_semantics=sem, vmem_limit_bytes=VMEM_LIMIT)


def _proj_kernel(x_ref, g_ref, w_ref, o_ref):
    x = x_ref[...]
    ms = jnp.mean(x * x, axis=-1, keepdims=True)
    h = x * lax.rsqrt(ms + NORM_EPS) * g_ref[...]
    o_ref[...] = _dg(h.astype(BF16), w_ref[...])


def _in_proj(x2d, gain, w_bf, tm=1024, tn=768):
    t, d = x2d.shape
    n = w_bf.shape[1]
    tm = min(tm, t)
    return pl.pallas_call(
        _proj_kernel,
        grid=(t // tm, n // tn),
        in_specs=[pl.BlockSpec((tm, d), lambda i, j: (i, 0)),
                  pl.BlockSpec((1, d), lambda i, j: (0, 0)),
                  pl.BlockSpec((d, tn), lambda i, j: (0, j))],
        out_specs=pl.BlockSpec((tm, tn), lambda i, j: (i, j)),
        out_shape=jax.ShapeDtypeStruct((t, n), F32),
        compiler_params=_params(("parallel", "arbitrary")),
        name="in_proj",
    )(x2d, gain, w_bf)


def _out_kernel(final, o1_ref, o2_ref, o3_ref, o4_ref, gate_ref, x_ref, w_ref, fg_ref, y_ref):
    mix = jnp.concatenate([o1_ref[...], o2_ref[...], o3_ref[...], o4_ref[...]], axis=-1)
    g = gate_ref[...]
    z = mix * (g * _sigmoid(g))
    y = x_ref[...] + _dg(z.astype(BF16), w_ref[...])
    if final:
        ms = jnp.mean(y * y, axis=-1, keepdims=True)
        y = y * lax.rsqrt(ms + NORM_EPS) * fg_ref[...]
    y_ref[...] = y


def _out_proj(mixers, proj2d, x2d, w_bf, final_gain, final, tm=512):
    t, d = x2d.shape
    dm = w_bf.shape[0]
    tm = min(tm, t)
    branch = pl.BlockSpec((tm, D_BRANCH), lambda i: (i, 0))
    return pl.pallas_call(
        functools.partial(_out_kernel, final),
        grid=(t // tm,),
        in_specs=[branch, branch, branch, branch,
                  pl.BlockSpec((tm, dm), lambda i: (i, 0)),
                  pl.BlockSpec((tm, d), lambda i: (i, 0)),
                  pl.BlockSpec((dm, d), lambda i: (0, 0)),
                  pl.BlockSpec((1, d), lambda i: (0, 0))],
        out_specs=pl.BlockSpec((tm, d), lambda i: (i, 0)),
        out_shape=jax.ShapeDtypeStruct((t, d), F32),
        compiler_params=_params(("parallel",)),
        name="out_proj",
    )(*mixers, proj2d, x2d, w_bf, final_gain)


def _lru_kernel(u_ref, cw_ref, cb_ref, wah_ref, wal_ref, ba_ref, wxh_ref, wxl_ref, bx_ref, lam_ref,
                o_ref, ubuf, hcar):
    ts = u_ref.shape[1]

    @pl.when(pl.program_id(1) == 0)
    def _():
        ubuf[0:8, :] = jnp.zeros((8, D_BRANCH), F32)
        hcar[...] = jnp.zeros_like(hcar)

    u = u_ref[0]
    ubuf[8:8 + ts, :] = u
    xc = cb_ref[...] + cw_ref[0:1, :] * ubuf[5:5 + ts, :]
    for j in range(1, LRU_CONV):
        xc = xc + cw_ref[j:j + 1, :] * ubuf[5 + j:5 + j + ts, :]
    ubuf[0:8, :] = u[ts - 8:ts, :]

    r = _sigmoid(_dot3_pre(xc, wah_ref[...], wal_ref[...]) + ba_ref[...])
    gi = _sigmoid(_dot3_pre(xc, wxh_ref[...], wxl_ref[...]) + bx_ref[...])
    log_a = (-LRU_C) * r * _softplus(-lam_ref[...])
    a = jnp.exp(log_a)
    b = jnp.sqrt(jnp.tanh(-log_a) * (a * a + 1.0)) * (gi * xc)

    row = lax.broadcasted_iota(jnp.int32, (ts, 1), 0)
    d = 1
    while d < ts:
        keep = row >= d
        a_s = jnp.where(keep, pltpu.roll(a, d, axis=0), 1.0)
        b_s = jnp.where(keep, pltpu.roll(b, d, axis=0), 0.0)
        b = a * b_s + b
        a = a * a_s
        d *= 2
    h = b + a * hcar[...]
    o_ref[0] = h
    hcar[...] = h[ts - 1:ts, :]


def _lru_mixer(proj, conv_w, conv_b, wa_bd, ba, wx_bd, bx, lam, ts=512):
    bsz, s, _ = proj.shape
    wah, wal = _split2(wa_bd)
    wxh, wxl = _split2(wx_bd)
    row = pl.BlockSpec((1, D_BRANCH), lambda b, j: (0, 0))
    mat = pl.BlockSpec((D_BRANCH, D_BRANCH), lambda b, j: (0, 0))
    return pl.pallas_call(
        _lru_kernel,
        grid=(bsz, s // ts),
        in_specs=[pl.BlockSpec((1, ts, D_BRANCH), lambda b, j: (b, j, 4)),
                  pl.BlockSpec((LRU_CONV, D_BRANCH), lambda b, j: (0, 0)),
                  row, mat, mat, row, mat, mat, row, row],
        out_specs=pl.BlockSpec((1, ts, D_BRANCH), lambda b, j: (b, j, 0)),
        out_shape=jax.ShapeDtypeStruct((bsz, s, D_BRANCH), F32),
        scratch_shapes=[pltpu.VMEM((ts + 8, D_BRANCH), F32), pltpu.VMEM((1, D_BRANCH), F32)],
        compiler_params=_params(("parallel", "arbitrary")),
        name="rg_lru",
    )(proj, conv_w, conv_b, wah, wal, ba, wxh, wxl, bx, lam)


def _cmp_kernel(rk_ref, rv_ref, pos_ref, w1_ref, w2_ref, kch_ref, kcl_ref, vct_ref):
    half = (CMP_BLOCK // 2) * HEAD_DIM
    nrow = rk_ref.shape[1]
    outs = []
    for idx, r_ref in enumerate((rk_ref, rv_ref)):
        r = r_ref[0]
        w1 = w1_ref[idx]
        lo_half = _dot3(r, w1[0:half, :])
        hi_half = _dot3(r, w1[half:2 * half, :])
        bias = _dot3(jnp.broadcast_to(pos_ref[idx], (8, 2 * half)), w1)[0:1, :]
        hid = lo_half + pltpu.roll(hi_half, nrow - 1, axis=0) + bias
        outs.append(_dot3(jax.nn.gelu(hid), w2_ref[idx]))
    kc_hi, kc_lo = _split2(outs[0])
    kch_ref[0] = kc_hi
    kcl_ref[0] = kc_lo
    vct_ref[0] = outs[1].T.astype(BF16)


def _nsa_compress(rk, rv, pos_flat, w1, w2_pair):
    bsz, nrow, width = rk.shape
    blk = pl.BlockSpec((1, nrow, width), lambda b: (b, 0, 0))
    out = pl.BlockSpec((1, nrow, 128), lambda b: (b, 0, 0))
    return pl.pallas_call(
        _cmp_kernel,
        grid=(bsz,),
        in_specs=[blk, blk,
                  pl.BlockSpec(pos_flat.shape, lambda b: (0, 0, 0)),
                  pl.BlockSpec(w1.shape, lambda b: (0, 0, 0)),
                  pl.BlockSpec(w2_pair.shape, lambda b: (0, 0, 0))],
        out_specs=[out, out, pl.BlockSpec((1, 128, nrow), lambda b: (b, 0, 0))],
        out_shape=[jax.ShapeDtypeStruct((bsz, nrow, 128), BF16)] * 2
                  + [jax.ShapeDtypeStruct((bsz, 128, nrow), BF16)],
        compiler_params=_params(("parallel",)),
        name="nsa_compress",
    )(rk, rv, pos_flat, w1, w2_pair)


def _nsa_kernel(n_top, q_ref, g_ref, kch_ref, kcl_ref, vct_ref, ks_ref, vs_ref, kw_ref, vw_ref,
                mt_ref, gb_ref, o_ref, acc_ref, m_ref):
    qb = q_ref.shape[1]
    n_cmp = kch_ref.shape[1]
    n_sel = mt_ref.shape[0]
    tk = min(NSA_KV_TILE, vs_ref.shape[1])
    rep = tk // 128
    q0 = pl.program_id(1) * qb
    lane = lax.broadcasted_iota(jnp.int32, (1, 128), 1)
    q = q_ref[0] * (HEAD_DIM ** -0.5)
    parts = []
    for h in range(N_HEADS):
        pair, half = divmod(h, 2)
        keep = (lane >= half * HEAD_DIM) & (lane < (half + 1) * HEAD_DIM)
        parts.append(jnp.where(keep, q[:, pair * 128:(pair + 1) * 128], 0.0))
    q4 = jnp.concatenate(parts, axis=0)
    q4_log2 = (q4 * LOG2E).astype(BF16)
    t = q0 + lax.broadcasted_iota(jnp.int32, (qb, 1), 0)
    t4 = jnp.concatenate([t] * N_HEADS, axis=0)
    t_row = q0 + lax.broadcasted_iota(jnp.int32, (1, qb), 1)
    t4_row = jnp.concatenate([t_row] * N_HEADS, axis=1)

    q4h, q4l = _split2(q4)
    kch = kch_ref[0]
    sct = _dg(kch, q4h, 1, 1) + (_dg(kch, q4l, 1, 1) + _dg(kcl_ref[0], q4h, 1, 1))
    cend = lax.broadcasted_iota(jnp.int32, (n_cmp, 1), 0) * CMP_STRIDE + (CMP_BLOCK - 1)
    valid = cend <= t4_row
    sct = jnp.where(valid, sct, NEG_INF)
    e = jnp.exp(sct - jnp.max(sct, axis=0, keepdims=True))
    p_ct = e / jnp.sum(e, axis=0, keepdims=True)
    anyv = (t4_row >= CMP_BLOCK - 1).astype(F32)
    oc4 = (_dg(vct_ref[0], p_ct.astype(BF16)) * anyv).T
    p_cv = jnp.where(valid, p_ct, 0.0)
    imp_t = p_cv[:, 0:qb] + p_cv[:, qb:2 * qb] + p_cv[:, 2 * qb:3 * qb] + p_cv[:, 3 * qb:4 * qb]
    pslc_t = _dot_xl(mt_ref[...], imp_t)

    blk = lax.broadcasted_iota(jnp.int32, (n_sel, 1), 0)
    blk_f = blk.astype(F32)
    cur = t_row // SEL_BLOCK
    forced = (blk == 0) | (blk == cur) | (blk == cur - 1)
    score = jnp.where(blk > cur, -1.0, jnp.where(forced, FORCE_SCORE, pslc_t))
    sel_t = jnp.zeros((n_sel, qb), F32)
    for _ in range(n_top):
        best = jnp.max(score, axis=0, keepdims=True)
        idx = jnp.min(jnp.where(score == best, blk_f, float(n_sel)), axis=0, keepdims=True)
        pick = blk_f == idx
        sel_t = jnp.where(pick, 1.0, sel_t)
        score = jnp.where(pick, -3e38, score)
    notsel = (1.0 - sel_t).T.astype(BF16)
    lhs = jnp.concatenate([q4_log2, jnp.concatenate([notsel] * N_HEADS, axis=0)], axis=1)

    m_ref[...] = jnp.full(m_ref.shape, NEG_INF, F32)
    acc_ref[...] = jnp.zeros(acc_ref.shape, F32)

    def tile(j, masked):
        start = pl.multiple_of(j * tk, tk)
        s = _dg(lhs, ks_ref[0, :, pl.ds(start, tk)])
        if masked:
            s = jnp.where((start + lax.broadcasted_iota(jnp.int32, (1, tk), 1)) <= t4, s, NEG_INF)
        m_old = m_ref[...]
        m_new = jnp.maximum(m_old, jnp.max(s, axis=-1, keepdims=True))
        p = jnp.exp2(s - jnp.concatenate([m_new] * rep, axis=1))
        acc_ref[...] = jnp.exp2(m_old - m_new) * acc_ref[...] + _dg(p.astype(BF16), vs_ref[0, pl.ds(start, tk), :])
        m_ref[...] = m_new

    n_full = q0 // tk

    def body(j, carry):
        tile(j, False)
        return carry

    lax.fori_loop(0, n_full, body, 0)
    for u in range(max(qb // tk, 1)):
        tile(n_full + u, True)
    lsel = _lane_sum_selector()
    acc = acc_ref[...]
    os4 = acc / _dot_lx(acc, lsel)

    nw = qb + WINDOW
    start_w = pl.multiple_of(jnp.maximum(q0 - WINDOW, 0), 128)
    s_w = _dg(q4_log2, kw_ref[0, :, pl.ds(start_w, nw)])
    wpos = start_w + lax.broadcasted_iota(jnp.int32, (1, nw), 1)
    s_w = jnp.where((wpos <= t4) & (wpos > t4 - WINDOW), s_w, NEG_INF)
    p_w = jnp.exp2(s_w - jnp.max(s_w, axis=-1, keepdims=True))
    aw = _dg(p_w.astype(BF16), vw_ref[0, pl.ds(start_w, nw), :])
    ow4 = aw / _dot_lx(aw, lsel)

    gates = _sigmoid(g_ref[0] + gb_ref[...])
    out = jnp.zeros((qb, D_BRANCH), F32)
    for h in range(N_HEADS):
        rows = slice(h * qb, (h + 1) * qb)
        base = 128 + h * N_NSA_BRANCH
        mixed = (gates[:, base:base + 1] * oc4[rows]
                 + gates[:, base + 1:base + 2] * os4[rows]
                 + gates[:, base + 2:base + 3] * ow4[rows])
        out = out + _dot_lx(mixed, _placement(h))
    o_ref[0] = out


def _nsa_mixer(proj, cmp_pos, cmp_w1, cmp_w2, gate_b):
    bsz, s, _ = proj.shape
    n_row = s // CMP_STRIDE
    n_sel = s // SEL_BLOCK
    n_top = min(N_SELECT, n_sel)
    qb = min(NSA_Q_BLOCK, s)
    base = 6 * D_BRANCH
    col = lambda k: proj[:, :, base + k * HEAD_DIM: base + (k + 1) * HEAD_DIM]
    rk = col(0).reshape(bsz, n_row, CMP_STRIDE * HEAD_DIM)
    rv = col(1).reshape(bsz, n_row, CMP_STRIDE * HEAD_DIM)
    pos_flat = cmp_pos.reshape(2, 1, CMP_BLOCK * HEAD_DIM)
    w2_pair = jnp.stack([jnp.concatenate([cmp_w2[0], cmp_w2[0]], axis=1),
                         jnp.concatenate([cmp_w2[1], jnp.zeros_like(cmp_w2[1])], axis=1)])
    kc_hi, kc_lo, vc_t = _nsa_compress(rk, rv, pos_flat, cmp_w1, w2_pair)

    key_blk = jnp.arange(s, dtype=jnp.int32) // SEL_BLOCK
    penalty = jnp.where(jnp.arange(n_sel, dtype=jnp.int32)[:, None] == key_blk[None, :], -MASK_BIG, 0.0)
    ks_t = jnp.swapaxes(col(2), 1, 2)
    ks_aug = jnp.concatenate([ks_t, ks_t, jnp.broadcast_to(penalty, (bsz, n_sel, s))], axis=1).astype(BF16)
    kw_t = jnp.swapaxes(col(4), 1, 2)
    kw_pair = jnp.concatenate([kw_t, kw_t], axis=1).astype(BF16)
    vs1 = _with_ones_lane(col(3))
    vw1 = _with_ones_lane(col(5))
    ratio = SEL_BLOCK // CMP_STRIDE
    off = jnp.arange(n_row, dtype=jnp.int32)[None, :] - ratio * jnp.arange(n_sel, dtype=jnp.int32)[:, None]
    m_t = jnp.where((off == -1) | (off == 3), 1.0, jnp.where((off >= 0) & (off <= 2), 2.0, 0.0)).astype(BF16)
    gate_b_pad = jnp.zeros((1, D_BRANCH), F32).at[0, 128:128 + N_NSA_BRANCH * N_HEADS].set(gate_b)
    full = lambda arr: pl.BlockSpec((1,) + arr.shape[1:], lambda b, i: (b, 0, 0))
    return pl.pallas_call(
        functools.partial(_nsa_kernel, n_top),
        grid=(bsz, s // qb),
        in_specs=[pl.BlockSpec((1, qb, D_BRANCH), lambda b, i: (b, i, 5)),
                  pl.BlockSpec((1, qb, D_BRANCH), lambda b, i: (b, i, 7)),
                  full(kc_hi), full(kc_lo), full(vc_t), full(ks_aug), full(vs1), full(kw_pair), full(vw1),
                  pl.BlockSpec(m_t.shape, lambda b, i: (0, 0)),
                  pl.BlockSpec((1, D_BRANCH), lambda b, i: (0, 0))],
        out_specs=pl.BlockSpec((1, qb, D_BRANCH), lambda b, i: (b, i, 0)),
        out_shape=jax.ShapeDtypeStruct((bsz, s, D_BRANCH), F32),
        scratch_shapes=[pltpu.VMEM((N_HEADS * qb, 128), F32), pltpu.VMEM((N_HEADS * qb, 128), F32)],
        compiler_params=_params(("parallel", "arbitrary")),
        name="nsa_attention",
    )(proj, proj, kc_hi, kc_lo, vc_t, ks_aug, vs1, kw_pair, vw1, m_t, gate_b_pad)


def _rwkv_kernel(c, pr_ref, pk_ref, pv_ref, pw_ref, mu_ref, w0_ref, wuph_ref, wupl_ref, a0_ref, auph_ref, aupl_ref,
                 kk_ref, ka_ref, rk_ref, lng_ref, lnb_ref, o_ref, prev_ref, st_ref):
    nb, blk, _ = pr_ref.shape
    rows = nb * blk
    n = nb * c

    @pl.when(pl.program_id(0) == 0)
    def _():
        prev_ref[...] = jnp.zeros_like(prev_ref)
        st_ref[...] = jnp.zeros_like(st_ref)

    brow = lax.broadcasted_iota(jnp.int32, (rows, 1), 0)

    def shifted(ref, k):
        x = ref[...].reshape(rows, D_BRANCH)
        prev = pltpu.roll(x, 1, axis=0)
        for b in range(nb):
            prev = jnp.where(brow == b * blk, prev_ref[k, b:b + 1, :], prev)
        for b in range(nb):
            prev_ref[k, b:b + 1, :] = x[(b + 1) * blk - 1:(b + 1) * blk, :]
        return x + (prev - x) * mu_ref[k:k + 1, :]

    r_all = shifted(pr_ref, 0)
    k_all = shifted(pk_ref, 1)
    v_all = shifted(pv_ref, 2)
    xw = shifted(pw_ref, 3)

    ones_bd = _block_ones()
    wlog = w0_ref[...] + _dot3_pre(jnp.tanh(xw), wuph_ref[...], wupl_ref[...])
    ld_all = -jnp.exp(-_softplus(-wlog) - 0.5)
    a_lr = _sigmoid(a0_ref[...] + _dot3_pre(xw, auph_ref[...], aupl_ref[...]))
    kk = k_all * kk_ref[...]
    kk = kk / jnp.maximum(jnp.sqrt(_dot_lx(kk * kk, ones_bd)), 1e-12)
    k_all = k_all * (1.0 + (a_lr - 1.0) * ka_ref[...])
    a_all = -kk
    b_all = kk * a_lr

    row = lax.broadcasted_iota(jnp.int32, (n, 1), 0)
    col = lax.broadcasted_iota(jnp.int32, (1, n), 1)
    rowb = row // c
    colb = col // c
    same = rowb == colb
    tri = jnp.where(same & (col <= row), 1.0, 0.0).astype(BF16)
    allm = jnp.where(same, 1.0, 0.0).astype(BF16)
    upper = same & (row < col)
    upper_eq = same & (row <= col)
    eye = jnp.where(row == col, 1.0, 0.0)
    col2b = jnp.concatenate([colb, colb], axis=1)
    hm = _head_masks()
    bd = ones_bd.astype(F32)
    mean_m = _block_ones(1.0 / HEAD_DIM)

    def chunk_rows(x, j):
        return jnp.concatenate([x[b * blk + j * c:b * blk + (j + 1) * c] for b in range(nb)], axis=0)

    def prepare_all(nchunk):
        ps = []
        for j in range(nchunk):
            r, k, v, ld, a, b = (chunk_rows(x, j) for x in (r_all, k_all, v_all, ld_all, a_all, b_all))
            ps.append(dict(r=r, k=k, v=v, ld=ld, a=a, b=b))
        for p in ps:
            p["cs"] = _dot_xl(tri, p["ld"])
            p["tot"] = _dot_xl(allm, p["ld"])
        for p in ps:
            cs, tot = p["cs"], p["tot"]
            a_t = p["a"] * jnp.exp(cs - p["ld"])
            r_t = p["r"] * jnp.exp(cs)
            inv = jnp.exp(-cs)
            fin = jnp.exp(tot - cs)
            lhs = jnp.concatenate([(p["b"] * inv) * hm[h] for h in range(N_HEADS)]
                                  + [(p["k"] * inv) * hm[h] for h in range(N_HEADS)], axis=0)
            rhs = jnp.concatenate([a_t, r_t], axis=0)
            p["small"] = _dg(lhs.astype(BF16), rhs.astype(BF16), 1, 1)
            ar_t = jnp.concatenate([a_t.T, r_t.T], axis=1)
            p["v_t"] = p["v"].T.astype(BF16)
            p["ar_b"] = [jnp.where(col2b == bb, ar_t, 0.0).astype(BF16) for bb in range(nb)]
            p["bk_b"] = [jnp.concatenate([jnp.where(rowb == bb, p["b"] * fin, 0.0),
                                          jnp.where(rowb == bb, p["k"] * fin, 0.0)], axis=0).astype(BF16)
                         for bb in range(nb)]
            p["g_rows"] = [jnp.exp(tot[bb * c:bb * c + 1, :]) for bb in range(nb)]
            p["bonus"] = _dot_lx(p["r"] * p["k"] * rk_ref[...], ones_bd) * p["v"]
        for p in ps:
            small = p["small"]
            p["pw"], p["tinv"], p["brs"], p["zs"], p["y0s"] = [], [], [], [], []
            for h in range(N_HEADS):
                ba = jnp.where(upper, small[h * n:(h + 1) * n, 0:n], 0.0)
                br = jnp.where(upper_eq, small[h * n:(h + 1) * n, n:2 * n], 0.0).astype(BF16)
                ka = jnp.where(upper, small[(N_HEADS + h) * n:(N_HEADS + h + 1) * n, 0:n], 0.0).astype(BF16)
                kr = jnp.where(upper_eq, small[(N_HEADS + h) * n:(N_HEADS + h + 1) * n, n:2 * n], 0.0).astype(BF16)
                vh = p["v_t"][h * HEAD_DIM:(h + 1) * HEAD_DIM, :]
                p["pw"].append(ba)
                p["tinv"].append(eye + ba)
                p["brs"].append(br)
                p["zs"].append(_dg(vh, ka))
                p["y0s"].append(_dg(vh, kr))
        span = 2
        while span < c:
            for p in ps:
                for h in range(N_HEADS):
                    pw_bf = p["pw"][h].astype(BF16)
                    p["pw"][h] = _dg(pw_bf, pw_bf)
            for p in ps:
                for h in range(N_HEADS):
                    p["tinv"][h] = p["tinv"][h] + _dg(p["tinv"][h].astype(BF16), p["pw"][h].astype(BF16))
            span *= 2
        for p in ps:
            p["tinvs"] = [t.astype(BF16) for t in p["tinv"]]
        return ps

    def advance(p):
        xy_t = jnp.zeros((D_BRANCH, 2 * n), F32)
        for bb in range(nb):
            xy_t = xy_t + _dg(st_ref[bb].astype(BF16), p["ar_b"][bb])
        u_rows, y_rows = [], []
        for h in range(N_HEADS):
            hs = slice(h * HEAD_DIM, (h + 1) * HEAD_DIM)
            uh = _dg((xy_t[hs, 0:n] + p["zs"][h]).astype(BF16), p["tinvs"][h]).astype(BF16)
            u_rows.append(uh)
            y_rows.append(xy_t[hs, n:2 * n] + _dg(uh, p["brs"][h]) + p["y0s"][h])
        uv = jnp.concatenate([jnp.concatenate(u_rows, axis=0), p["v_t"]], axis=1)
        for bb in range(nb):
            st_ref[bb] = st_ref[bb] * p["g_rows"][bb] + bd * _dg(uv, p["bk_b"][bb])
        return jnp.concatenate(y_rows, axis=0).T

    prepared = prepare_all(blk // c)
    for j, p in enumerate(prepared):
        y = advance(p)
        mean = _dot_lx(y, mean_m)
        yc = y - mean
        var = _dot_lx(yc * yc, mean_m)
        out = yc * lax.rsqrt(var + RWKV_GN_EPS) * lng_ref[...] + lnb_ref[...]
        o_ref[:, j * c:(j + 1) * c, :] = (out + p["bonus"]).reshape(nb, c, D_BRANCH)


def _rwkv_mixer(proj, mu, w0, w_up, a0, a_up, k_k, k_a, r_k, ln_g, ln_b):
    bsz, s, _ = proj.shape
    c = min(RWKV_CHUNK, s)
    tb = min(RWKV_BLOCK, s)
    mu_p = jnp.zeros((4, D_BRANCH), F32)
    mu_p = mu_p.at[0:3, :].set(mu[:3 * D_BRANCH].reshape(3, D_BRANCH))
    mu_p = mu_p.at[3, :2 * RWKV_RANK].set(mu[3 * D_BRANCH:])
    wup_h, wup_l = _split2(jnp.zeros((D_BRANCH, D_BRANCH), F32).at[:RWKV_RANK, :].set(w_up))
    aup_h, aup_l = _split2(jnp.zeros((D_BRANCH, D_BRANCH), F32).at[RWKV_RANK:2 * RWKV_RANK, :].set(a_up))
    row = lambda a: a.reshape(1, D_BRANCH)
    blk = lambda cidx: pl.BlockSpec((bsz, tb, D_BRANCH), lambda i, cidx=cidx: (0, i, cidx))
    rowspec = pl.BlockSpec((1, D_BRANCH), lambda i: (0, 0))
    matspec = pl.BlockSpec((D_BRANCH, D_BRANCH), lambda i: (0, 0))
    return pl.pallas_call(
        functools.partial(_rwkv_kernel, c),
        grid=(s // tb,),
        in_specs=[blk(8), blk(9), blk(10), blk(11),
                  pl.BlockSpec((4, D_BRANCH), lambda i: (0, 0)),
                  rowspec, matspec, matspec, rowspec, matspec, matspec,
                  rowspec, rowspec, rowspec, rowspec, rowspec],
        out_specs=pl.BlockSpec((bsz, tb, D_BRANCH), lambda i: (0, i, 0)),
        out_shape=jax.ShapeDtypeStruct((bsz, s, D_BRANCH), F32),
        scratch_shapes=[pltpu.VMEM((4, 8, D_BRANCH), F32), pltpu.VMEM((bsz, D_BRANCH, D_BRANCH), F32)],
        compiler_params=_params(("arbitrary",)),
        name="rwkv7",
    )(proj, proj, proj, proj, mu_p, row(w0), wup_h, wup_l, row(a0), aup_h, aup_l, row(k_k), row(k_a),
      row(r_k), row(ln_g), row(ln_b))


def _lane_sum_selector():
    r = lax.broadcasted_iota(jnp.int32, (128, 128), 0)
    return jnp.where(r == HEAD_DIM, 1.0, 0.0).astype(BF16)


def _placement(h):
    r = lax.broadcasted_iota(jnp.int32, (128, D_BRANCH), 0)
    c = lax.broadcasted_iota(jnp.int32, (128, D_BRANCH), 1)
    return jnp.where((r < HEAD_DIM) & (c == r + h * HEAD_DIM), 1.0, 0.0).astype(BF16)


def _diff_kernel(lambda_init, q_ref, k_ref, v_ref, lam_ref, g_ref, o_ref, acc_ref, m_ref):
    qb = q_ref.shape[1]
    tk = min(DIFF_KV_TILE, k_ref.shape[3])
    rep = tk // 128
    q0 = pl.program_id(1) * qb
    lane = lax.broadcasted_iota(jnp.int32, (1, 128), 1)
    q = q_ref[0] * (DIFF_QK_DIM ** -0.5 * LOG2E)
    qs = []
    for h in range(N_HEADS):
        pair, half = divmod(h, 2)
        qp = q[:, pair * 128:(pair + 1) * 128]
        both = []
        for cc in range(2):
            lo = half * HEAD_DIM + cc * DIFF_QK_DIM
            both.append(jnp.where((lane >= lo) & (lane < lo + DIFF_QK_DIM), qp, 0.0).astype(BF16))
        qs.append(jnp.concatenate(both, axis=0))
    t = q0 + lax.broadcasted_iota(jnp.int32, (qb, 1), 0)
    t2 = jnp.concatenate([t, t], axis=0)
    m_ref[...] = jnp.full(m_ref.shape, NEG_INF, F32)
    acc_ref[...] = jnp.zeros(acc_ref.shape, F32)

    def tile(j, masked):
        start = pl.multiple_of(j * tk, tk)
        if masked:
            allow = (start + lax.broadcasted_iota(jnp.int32, (1, tk), 1)) <= t2
        for h in range(N_HEADS):
            kt = k_ref[0, h // 2, :, pl.ds(start, tk)]
            v = v_ref[0, h, pl.ds(start, tk), :]
            s = _dg(qs[h], kt)
            if masked:
                s = jnp.where(allow, s, NEG_INF)
            m_old = m_ref[h]
            m_new = jnp.maximum(m_old, jnp.max(s, axis=-1, keepdims=True))
            p = jnp.exp2(s - jnp.concatenate([m_new] * rep, axis=1))
            acc_ref[h] = jnp.exp2(m_old - m_new) * acc_ref[h] + _dg(p.astype(BF16), v)
            m_ref[h] = m_new

    n_full = q0 // tk

    def body(j, carry):
        tile(j, False)
        return carry

    lax.fori_loop(0, n_full, body, 0)
    for u in range(max(qb // tk, 1)):
        tile(n_full + u, True)

    lam = lam_ref[...]
    lam_full = (jnp.exp(jnp.sum(lam[0:1] * lam[1:2], axis=-1, keepdims=True))
                - jnp.exp(jnp.sum(lam[2:3] * lam[3:4], axis=-1, keepdims=True)) + lambda_init)
    lsel = _lane_sum_selector()
    out = jnp.zeros((qb, D_BRANCH), F32)
    for h in range(N_HEADS):
        a = acc_ref[h]
        a = a / _dot_lx(a, lsel)
        d = a[0:qb] - lam_full * a[qb:2 * qb]
        out = out + _dot_lx(d, _placement(h))
    ms = _dot_lx(out * out, _block_ones(1.0 / HEAD_DIM))
    o_ref[0] = out * lax.rsqrt(ms + DIFF_EPS) * g_ref[...] * (1.0 - lambda_init)


def _with_ones_lane(v):
    pad = jnp.zeros(v.shape[:-1] + (128 - HEAD_DIM,), v.dtype).at[..., 0].set(1.0)
    return jnp.concatenate([v, pad], axis=-1).astype(BF16)


def _diff_mixer(proj, lam, subln_g, lambda_init):
    bsz, s, _ = proj.shape
    qb = min(DIFF_Q_BLOCK, s)
    kt_pairs = proj[:, :, 13 * D_BRANCH:14 * D_BRANCH].reshape(bsz, s, 2, 128).transpose(0, 2, 3, 1).astype(BF16)
    v_heads = proj[:, :, 14 * D_BRANCH:15 * D_BRANCH].reshape(bsz, s, N_HEADS, HEAD_DIM).transpose(0, 2, 1, 3)
    v_heads = _with_ones_lane(v_heads)
    g_rep = jnp.tile(subln_g, N_HEADS).reshape(1, D_BRANCH)
    return pl.pallas_call(
        functools.partial(_diff_kernel, lambda_init),
        grid=(bsz, s // qb),
        in_specs=[pl.BlockSpec((1, qb, D_BRANCH), lambda b, i: (b, i, 12)),
                  pl.BlockSpec((1, 2, 128, s), lambda b, i: (b, 0, 0, 0)),
                  pl.BlockSpec((1, N_HEADS, s, 128), lambda b, i: (b, 0, 0, 0)),
                  pl.BlockSpec(lam.shape, lambda b, i: (0, 0)),
                  pl.BlockSpec((1, D_BRANCH), lambda b, i: (0, 0))],
        out_specs=pl.BlockSpec((1, qb, D_BRANCH), lambda b, i: (b, i, 0)),
        out_shape=jax.ShapeDtypeStruct((bsz, s, D_BRANCH), F32),
        scratch_shapes=[pltpu.VMEM((N_HEADS, 2 * qb, 128), F32),
                        pltpu.VMEM((N_HEADS, 2 * qb, 128), F32)],
        compiler_params=_params(("parallel", "arbitrary")),
        name="diff_attention",
    )(proj, kt_pairs, v_heads, lam, g_rep)


def _block_diag(w):
    n = w.shape[0] * w.shape[1]
    out = jnp.zeros((n, n), F32)
    for h in range(w.shape[0]):
        out = out.at[h * HEAD_DIM:(h + 1) * HEAD_DIM, h * HEAD_DIM:(h + 1) * HEAD_DIM].set(w[h])
    return out


def _layout_w_in(w):
    d = w.shape[0]
    nsa_end = 5 * D_BRANCH + D_BRANCH + 6 * HEAD_DIM + N_NSA_BRANCH * N_HEADS
    rwkv_end = nsa_end + 3 * D_BRANCH + 2 * RWKV_RANK
    pad1 = 8 * D_BRANCH - nsa_end
    pad2 = 12 * D_BRANCH - (rwkv_end + pad1)
    return jnp.concatenate([w[:, :nsa_end], jnp.zeros((d, pad1), F32), w[:, nsa_end:rwkv_end],
                            jnp.zeros((d, pad2), F32), w[:, rwkv_end:]], axis=1)


def kernel(x, norm_gain, w_in, w_out, final_gain, lru_conv_w, lru_conv_b, lru_wa, lru_ba, lru_wx, lru_bx, lru_lambda, nsa_cmp_pos, nsa_cmp_w1, nsa_cmp_w2, nsa_gate_b, rwkv_mu, rwkv_w0, rwkv_w_up, rwkv_a0, rwkv_a_up, rwkv_k_k, rwkv_k_a, rwkv_r_k, rwkv_ln_g, rwkv_ln_b, diff_lambda, diff_subln_g):
    bsz, s, d = x.shape
    depth = w_in.shape[0]
    t = bsz * s
    x2d = x.reshape(t, d)
    row = lambda a: a.reshape(1, -1)
    for l in range(depth):
        proj2d = _in_proj(x2d, row(norm_gain[l]), _layout_w_in(w_in[l]).astype(BF16))
        proj = proj2d.reshape(bsz, s, N_PROJ_BLOCKS * D_BRANCH)
        o_lru = _lru_mixer(proj, lru_conv_w[l], row(lru_conv_b[l]), _block_diag(lru_wa[l]), row(lru_ba[l]),
                           _block_diag(lru_wx[l]), row(lru_bx[l]), row(lru_lambda[l]))
        o_nsa = _nsa_mixer(proj, nsa_cmp_pos[l], nsa_cmp_w1[l], nsa_cmp_w2[l], nsa_gate_b[l])
        o_rwkv = _rwkv_mixer(proj, rwkv_mu[l], rwkv_w0[l], rwkv_w_up[l], rwkv_a0[l], rwkv_a_up[l],
                             rwkv_k_k[l], rwkv_k_a[l], rwkv_r_k[l].reshape(-1), rwkv_ln_g[l], rwkv_ln_b[l])
        lambda_init = 0.8 - 0.6 * math.exp(-0.3 * l)
        o_diff = _diff_mixer(proj, diff_lambda[l], diff_subln_g[l], lambda_init)
        mixers = [o.reshape(t, D_BRANCH) for o in (o_lru, o_nsa, o_rwkv, o_diff)]
        x2d = _out_proj(mixers, proj2d, x2d, w_out[l].astype(BF16), row(final_gain), final=(l == depth - 1))
    return x2d.reshape(bsz, s, d)
```

```python
import functools
import math

import jax
import jax.numpy as jnp
from jax import lax
from jax.experimental import pallas as pl
from jax.experimental.pallas import tpu as pltpu

F32 = jnp.float32
BF16 = jnp.bfloat16

N_MIX = 4
HEAD_DIM = 64
N_HEADS = 4
D_BRANCH = N_HEADS * HEAD_DIM
NORM_EPS = 1e-6
NEG_INF = -1e30
LRU_CONV = 4
LRU_C = 8.0
CMP_BLOCK = 32
CMP_STRIDE = 16
SEL_BLOCK = 64
N_SELECT = 16
WINDOW = 512
N_NSA_BRANCH = 3
FORCE_SCORE = 1e9
RWKV_RANK = 32
RWKV_GN_EPS = 64e-5
DIFF_QK_DIM = HEAD_DIM // 2
DIFF_EPS = 1e-5
N_PROJ_BLOCKS = 15
Q_BLOCK = 128
KV_TILE = 512
RWKV_CHUNK = 64
RWKV_BLOCK = 256
VMEM_LIMIT = 56 * 1024 * 1024
LOG2E = 1.4426950408889634
DIFF_Q_BLOCK = 256
DIFF_KV_TILE = 512
NSA_Q_BLOCK = 256
NSA_KV_TILE = 512
MASK_BIG = 1e30


def _dg(a, b, ca=1, cb=0):
    return lax.dot_general(a, b, (((ca,), (cb,)), ((), ())), preferred_element_type=F32)


def _split2(a):
    hi = a.astype(BF16)
    lo = (a - hi.astype(F32)).astype(BF16)
    return hi, lo


def _dot3(a, b, ca=1, cb=0):
    ah, al = _split2(a)
    bh, bl = _split2(b)
    return _dg(ah, bh, ca, cb) + (_dg(ah, bl, ca, cb) + _dg(al, bh, ca, cb))


def _dot3_pre(a, bh, bl, ca=1, cb=0):
    ah, al = _split2(a)
    return _dg(ah, bh, ca, cb) + (_dg(ah, bl, ca, cb) + _dg(al, bh, ca, cb))


def _dot_lx(a, b_exact, ca=1, cb=0):
    a1 = a.astype(BF16)
    r = a - a1.astype(F32)
    a2 = r.astype(BF16)
    a3 = (r - a2.astype(F32)).astype(BF16)
    return _dg(a1, b_exact, ca, cb) + (_dg(a2, b_exact, ca, cb) + _dg(a3, b_exact, ca, cb))


def _dot_xl(a_exact, b, ca=1, cb=0):
    b1 = b.astype(BF16)
    r = b - b1.astype(F32)
    b2 = r.astype(BF16)
    b3 = (r - b2.astype(F32)).astype(BF16)
    return _dg(a_exact, b1, ca, cb) + (_dg(a_exact, b2, ca, cb) + _dg(a_exact, b3, ca, cb))


def _head_masks(width=D_BRANCH, group=HEAD_DIM):
    lane = lax.broadcasted_iota(jnp.int32, (1, width), 1)
    return [((lane >= h * group) & (lane < (h + 1) * group)).astype(F32) for h in range(width // group)]


def _block_ones(scale=1.0):
    r = lax.broadcasted_iota(jnp.int32, (D_BRANCH, D_BRANCH), 0) // HEAD_DIM
    c = lax.broadcasted_iota(jnp.int32, (D_BRANCH, D_BRANCH), 1) // HEAD_DIM
    return jnp.where(r == c, scale, 0.0).astype(BF16)


def _sigmoid(x):
    return 1.0 / (1.0 + jnp.exp(-x))


def _softplus(x):
    return jnp.maximum(x, 0.0) + jnp.log1p(jnp.exp(-jnp.abs(x)))


def _params(sem):
    return pltpu.CompilerParams(dimension_semantics=sem, vmem_limit_bytes=VMEM_LIMIT)


def _proj_kernel(x_ref, g_ref, w_ref, wt_ref, o_ref, ot_ref):
    x = x_ref[...]
    ms = jnp.mean(x * x, axis=-1, keepdims=True)
    h = (x * lax.rsqrt(ms + NORM_EPS) * g_ref[...]).astype(BF16)
    o_ref[...] = _dg(h, w_ref[...])

    @pl.when(pl.program_id(1) == 0)
    def _():
        ot_ref[...] = _dg(wt_ref[...], h, 1, 1).astype(BF16)


def _in_proj(x2d, gain, w_bf, wt_bf, tm=1024, tn=768):
    t, d = x2d.shape
    n = w_bf.shape[1]
    nt = wt_bf.shape[0]
    tm = min(tm, t)
    return pl.pallas_call(
        _proj_kernel,
        grid=(t // tm, n // tn),
        in_specs=[pl.BlockSpec((tm, d), lambda i, j: (i, 0)),
                  pl.BlockSpec((1, d), lambda i, j: (0, 0)),
                  pl.BlockSpec((d, tn), lambda i, j: (0, j)),
                  pl.BlockSpec((nt, d), lambda i, j: (0, 0))],
        out_specs=[pl.BlockSpec((tm, tn), lambda i, j: (i, j)),
                   pl.BlockSpec((nt, tm), lambda i, j: (0, i))],
        out_shape=[jax.ShapeDtypeStruct((t, n), F32), jax.ShapeDtypeStruct((nt, t), BF16)],
        compiler_params=_params(("parallel", "arbitrary")),
        name="in_proj",
    )(x2d, gain, w_bf, wt_bf)


def _out_kernel(final, o1_ref, o2_ref, o3_ref, o4_ref, gate_ref, x_ref, w_ref, fg_ref, y_ref):
    mix = jnp.concatenate([o1_ref[...], o2_ref[...], o3_ref[...], o4_ref[...]], axis=-1)
    g = gate_ref[...]
    z = mix * (g * _sigmoid(g))
    y = x_ref[...] + _dg(z.astype(BF16), w_ref[...])
    if final:
        ms = jnp.mean(y * y, axis=-1, keepdims=True)
        y = y * lax.rsqrt(ms + NORM_EPS) * fg_ref[...]
    y_ref[...] = y


def _out_proj(mixers, proj2d, x2d, w_bf, final_gain, final, tm=512):
    t, d = x2d.shape
    dm = w_bf.shape[0]
    tm = min(tm, t)
    branch = pl.BlockSpec((tm, D_BRANCH), lambda i: (i, 0))
    return pl.pallas_call(
        functools.partial(_out_kernel, final),
        grid=(t // tm,),
        in_specs=[branch, branch, branch, branch,
                  pl.BlockSpec((tm, dm), lambda i: (i, 0)),
                  pl.BlockSpec((tm, d), lambda i: (i, 0)),
                  pl.BlockSpec((dm, d), lambda i: (0, 0)),
                  pl.BlockSpec((1, d), lambda i: (0, 0))],
        out_specs=pl.BlockSpec((tm, d), lambda i: (i, 0)),
        out_shape=jax.ShapeDtypeStruct((t, d), F32),
        compiler_params=_params(("parallel",)),
        name="out_proj",
    )(*mixers, proj2d, x2d, w_bf, final_gain)


def _lru_kernel(u_ref, cw_ref, cb_ref, wah_ref, wal_ref, ba_ref, wxh_ref, wxl_ref, bx_ref, lam_ref,
                o_ref, ubuf, hcar):
    ts = u_ref.shape[1]

    @pl.when(pl.program_id(1) == 0)
    def _():
        ubuf[0:8, :] = jnp.zeros((8, D_BRANCH), F32)
        hcar[...] = jnp.zeros_like(hcar)

    u = u_ref[0]
    ubuf[8:8 + ts, :] = u
    xc = cb_ref[...] + cw_ref[0:1, :] * ubuf[5:5 + ts, :]
    for j in range(1, LRU_CONV):
        xc = xc + cw_ref[j:j + 1, :] * ubuf[5 + j:5 + j + ts, :]
    ubuf[0:8, :] = u[ts - 8:ts, :]

    r = _sigmoid(_dot3_pre(xc, wah_ref[...], wal_ref[...]) + ba_ref[...])
    gi = _sigmoid(_dot3_pre(xc, wxh_ref[...], wxl_ref[...]) + bx_ref[...])
    log_a = (-LRU_C) * r * _softplus(-lam_ref[...])
    a = jnp.exp(log_a)
    b = jnp.sqrt(jnp.tanh(-log_a) * (a * a + 1.0)) * (gi * xc)

    row = lax.broadcasted_iota(jnp.int32, (ts, 1), 0)
    d = 1
    while d < ts:
        keep = row >= d
        a_s = jnp.where(keep, pltpu.roll(a, d, axis=0), 1.0)
        b_s = jnp.where(keep, pltpu.roll(b, d, axis=0), 0.0)
        b = a * b_s + b
        a = a * a_s
        d *= 2
    h = b + a * hcar[...]
    o_ref[0] = h
    hcar[...] = h[ts - 1:ts, :]


def _lru_mixer(proj, conv_w, conv_b, wa_bd, ba, wx_bd, bx, lam, ts=512):
    bsz, s, _ = proj.shape
    wah, wal = _split2(wa_bd)
    wxh, wxl = _split2(wx_bd)
    row = pl.BlockSpec((1, D_BRANCH), lambda b, j: (0, 0))
    mat = pl.BlockSpec((D_BRANCH, D_BRANCH), lambda b, j: (0, 0))
    return pl.pallas_call(
        _lru_kernel,
        grid=(bsz, s // ts),
        in_specs=[pl.BlockSpec((1, ts, D_BRANCH), lambda b, j: (b, j, 4)),
                  pl.BlockSpec((LRU_CONV, D_BRANCH), lambda b, j: (0, 0)),
                  row, mat, mat, row, mat, mat, row, row],
        out_specs=pl.BlockSpec((1, ts, D_BRANCH), lambda b, j: (b, j, 0)),
        out_shape=jax.ShapeDtypeStruct((bsz, s, D_BRANCH), F32),
        scratch_shapes=[pltpu.VMEM((ts + 8, D_BRANCH), F32), pltpu.VMEM((1, D_BRANCH), F32)],
        compiler_params=_params(("parallel", "arbitrary")),
        name="rg_lru",
    )(proj, conv_w, conv_b, wah, wal, ba, wxh, wxl, bx, lam)


def _cmp_kernel(rk_ref, rv_ref, pos_ref, w1_ref, w2_ref, kch_ref, kcl_ref, vct_ref):
    half = (CMP_BLOCK // 2) * HEAD_DIM
    nrow = rk_ref.shape[1]
    outs = []
    for idx, r_ref in enumerate((rk_ref, rv_ref)):
        r = r_ref[0]
        w1 = w1_ref[idx]
        lo_half = _dot3(r, w1[0:half, :])
        hi_half = _dot3(r, w1[half:2 * half, :])
        bias = _dot3(jnp.broadcast_to(pos_ref[idx], (8, 2 * half)), w1)[0:1, :]
        hid = lo_half + pltpu.roll(hi_half, nrow - 1, axis=0) + bias
        outs.append(_dot3(jax.nn.gelu(hid), w2_ref[idx]))
    kc_hi, kc_lo = _split2(outs[0])
    kch_ref[0] = kc_hi
    kcl_ref[0] = kc_lo
    vct_ref[0] = outs[1].T.astype(BF16)


def _nsa_compress(rk, rv, pos_flat, w1, w2_pair):
    bsz, nrow, width = rk.shape
    blk = pl.BlockSpec((1, nrow, width), lambda b: (b, 0, 0))
    out = pl.BlockSpec((1, nrow, 128), lambda b: (b, 0, 0))
    return pl.pallas_call(
        _cmp_kernel,
        grid=(bsz,),
        in_specs=[blk, blk,
                  pl.BlockSpec(pos_flat.shape, lambda b: (0, 0, 0)),
                  pl.BlockSpec(w1.shape, lambda b: (0, 0, 0)),
                  pl.BlockSpec(w2_pair.shape, lambda b: (0, 0, 0))],
        out_specs=[out, out, pl.BlockSpec((1, 128, nrow), lambda b: (b, 0, 0))],
        out_shape=[jax.ShapeDtypeStruct((bsz, nrow, 128), BF16)] * 2
                  + [jax.ShapeDtypeStruct((bsz, 128, nrow), BF16)],
        compiler_params=_params(("parallel",)),
        name="nsa_compress",
    )(rk, rv, pos_flat, w1, w2_pair)


def _nsa_kernel(n_top, q_ref, g_ref, kch_ref, kcl_ref, vct_ref, ks_ref, pen_ref, vs_ref, kw_ref, vw_ref,
                mt_ref, gb_ref, o_ref, acc_ref, m_ref, s_ref):
    qb = q_ref.shape[1]
    n_cmp = kch_ref.shape[1]
    n_sel = mt_ref.shape[0]
    tk = min(NSA_KV_TILE, vs_ref.shape[1])
    rep = tk // 128
    q0 = pl.program_id(1) * qb
    lane = lax.broadcasted_iota(jnp.int32, (1, 128), 1)
    q = q_ref[0] * (HEAD_DIM ** -0.5)
    parts = []
    for h in range(N_HEADS):
        pair, half = divmod(h, 2)
        keep = (lane >= half * HEAD_DIM) & (lane < (half + 1) * HEAD_DIM)
        parts.append(jnp.where(keep, q[:, pair * 128:(pair + 1) * 128], 0.0))
    q4 = jnp.concatenate(parts, axis=0)
    q4_log2 = (q4 * LOG2E).astype(BF16)
    t = q0 + lax.broadcasted_iota(jnp.int32, (qb, 1), 0)
    t4 = jnp.concatenate([t] * N_HEADS, axis=0)
    t_row = q0 + lax.broadcasted_iota(jnp.int32, (1, qb), 1)
    t4_row = jnp.concatenate([t_row] * N_HEADS, axis=1)

    q4h, q4l = _split2(q4)
    kch = kch_ref[0]
    sct = _dg(kch, q4h, 1, 1) + (_dg(kch, q4l, 1, 1) + _dg(kcl_ref[0], q4h, 1, 1))
    cend = lax.broadcasted_iota(jnp.int32, (n_cmp, 1), 0) * CMP_STRIDE + (CMP_BLOCK - 1)
    valid = cend <= t4_row
    sct = jnp.where(valid, sct, NEG_INF)
    e = jnp.exp(sct - jnp.max(sct, axis=0, keepdims=True))
    p_ct = e / jnp.sum(e, axis=0, keepdims=True)
    anyv = (t4_row >= CMP_BLOCK - 1).astype(F32)
    oc4 = (_dg(vct_ref[0], p_ct.astype(BF16)) * anyv).T
    p_cv = jnp.where(valid, p_ct, 0.0)
    imp_t = p_cv[:, 0:qb] + p_cv[:, qb:2 * qb] + p_cv[:, 2 * qb:3 * qb] + p_cv[:, 3 * qb:4 * qb]
    pslc_t = _dot_xl(mt_ref[...], imp_t)

    blk = lax.broadcasted_iota(jnp.int32, (n_sel, 1), 0)
    blk_f = blk.astype(F32)
    cur = t_row // SEL_BLOCK
    forced = (blk == 0) | (blk == cur) | (blk == cur - 1)
    score = jnp.where(blk > cur, -1.0, jnp.where(forced, FORCE_SCORE, pslc_t))
    nw = qb + WINDOW
    start_w = pl.multiple_of(jnp.maximum(q0 - WINDOW, 0), 128)
    wpos = start_w + lax.broadcasted_iota(jnp.int32, (1, nw), 1)
    allow_w = (wpos <= t) & (wpos > t - WINDOW)
    lsel = _lane_sum_selector()

    def window_head(h):
        s_w = _dg(q4_log2[h * qb:(h + 1) * qb], kw_ref[:, pl.ds(start_w, nw)])
        s_w = jnp.where(allow_w, s_w, NEG_INF)
        p_w = jnp.exp2(s_w - jnp.max(s_w, axis=-1, keepdims=True))
        aw = _dg(p_w.astype(BF16), vw_ref[0, pl.ds(start_w, nw), :])
        return aw / _dot_lx(aw, lsel)

    ow_heads = []
    every = max(n_top // N_HEADS, 1)
    sel_t = jnp.zeros((n_sel, qb), F32)
    for rnd in range(n_top):
        best = jnp.max(score, axis=0, keepdims=True)
        idx = jnp.min(jnp.where(score == best, blk_f, float(n_sel)), axis=0, keepdims=True)
        pick = blk_f == idx
        sel_t = jnp.where(pick, 1.0, sel_t)
        score = jnp.where(pick, -3e38, score)
        if rnd % every == every - 1 and len(ow_heads) < N_HEADS:
            ow_heads.append(window_head(len(ow_heads)))
    while len(ow_heads) < N_HEADS:
        ow_heads.append(window_head(len(ow_heads)))
    notsel = (1.0 - sel_t).T.astype(BF16)
    lhs = jnp.concatenate([q4_log2, jnp.concatenate([notsel] * N_HEADS, axis=0)], axis=1)

    m_ref[...] = jnp.full(m_ref.shape, NEG_INF, F32)
    acc_ref[...] = jnp.zeros(acc_ref.shape, F32)

    def scores(j):
        start = pl.multiple_of(j * tk, tk)
        rhs = jnp.concatenate([ks_ref[:, pl.ds(start, tk)], pen_ref[:, pl.ds(start, tk)]], axis=0)
        return _dg(lhs, rhs)

    def absorb(j, s):
        start = pl.multiple_of(j * tk, tk)
        m_old = m_ref[...]
        m_new = jnp.maximum(m_old, jnp.max(s, axis=-1, keepdims=True))
        p = jnp.exp2(s - jnp.concatenate([m_new] * rep, axis=1))
        acc_ref[...] = jnp.exp2(m_old - m_new) * acc_ref[...] + _dg(p.astype(BF16), vs_ref[0, pl.ds(start, tk), :])
        m_ref[...] = m_new

    n_full = q0 // tk
    n_diag = max(qb // tk, 1)
    s_ref[...] = scores(0)

    def body(j, carry):
        s = s_ref[...]
        s_ref[...] = scores(j + 1)
        absorb(j, s)
        return carry

    lax.fori_loop(0, n_full, body, 0)
    for u in range(n_diag):
        j = n_full + u
        s = s_ref[...]
        if u + 1 < n_diag:
            s_ref[...] = scores(j + 1)
        allow = (pl.multiple_of(j * tk, tk) + lax.broadcasted_iota(jnp.int32, (1, tk), 1)) <= t4
        absorb(j, jnp.where(allow, s, NEG_INF))
    acc = acc_ref[...]
    os4 = acc / _dot_lx(acc, lsel)

    gates = _sigmoid(g_ref[0] + gb_ref[...])
    out = jnp.zeros((qb, D_BRANCH), F32)
    for h in range(N_HEADS):
        rows = slice(h * qb, (h + 1) * qb)
        base = 128 + h * N_NSA_BRANCH
        mixed = (gates[:, base:base + 1] * oc4[rows]
                 + gates[:, base + 1:base + 2] * os4[rows]
                 + gates[:, base + 2:base + 3] * ow_heads[h])
        out = out + _dot_lx(mixed, _placement(h))
    o_ref[0] = out


def _nsa_mixer(proj, kt_all, cmp_pos, cmp_w1, cmp_w2, gate_b):
    bsz, s, _ = proj.shape
    n_row = s // CMP_STRIDE
    n_sel = s // SEL_BLOCK
    n_top = min(N_SELECT, n_sel)
    qb = min(NSA_Q_BLOCK, s)
    base = 6 * D_BRANCH
    col = lambda k: proj[:, :, base + k * HEAD_DIM: base + (k + 1) * HEAD_DIM]
    rk = col(0).reshape(bsz, n_row, CMP_STRIDE * HEAD_DIM)
    rv = col(1).reshape(bsz, n_row, CMP_STRIDE * HEAD_DIM)
    pos_flat = cmp_pos.reshape(2, 1, CMP_BLOCK * HEAD_DIM)
    w2_pair = jnp.stack([jnp.concatenate([cmp_w2[0], cmp_w2[0]], axis=1),
                         jnp.concatenate([cmp_w2[1], jnp.zeros_like(cmp_w2[1])], axis=1)])
    kc_hi, kc_lo, vc_t = _nsa_compress(rk, rv, pos_flat, cmp_w1, w2_pair)

    key_blk = jnp.arange(s, dtype=jnp.int32) // SEL_BLOCK
    penalty = jnp.where(jnp.arange(n_sel, dtype=jnp.int32)[:, None] == key_blk[None, :],
                        -MASK_BIG, 0.0).astype(BF16)
    vs1 = _with_ones_lane(col(3))
    vw1 = _with_ones_lane(col(5))
    ratio = SEL_BLOCK // CMP_STRIDE
    off = jnp.arange(n_row, dtype=jnp.int32)[None, :] - ratio * jnp.arange(n_sel, dtype=jnp.int32)[:, None]
    m_t = jnp.where((off == -1) | (off == 3), 1.0, jnp.where((off >= 0) & (off <= 2), 2.0, 0.0)).astype(BF16)
    gate_b_pad = jnp.zeros((1, D_BRANCH), F32).at[0, 128:128 + N_NSA_BRANCH * N_HEADS].set(gate_b)
    full = lambda arr: pl.BlockSpec((1,) + arr.shape[1:], lambda b, i: (b, 0, 0))
    return pl.pallas_call(
        functools.partial(_nsa_kernel, n_top),
        grid=(bsz, s // qb),
        in_specs=[pl.BlockSpec((1, qb, D_BRANCH), lambda b, i: (b, i, 5)),
                  pl.BlockSpec((1, qb, D_BRANCH), lambda b, i: (b, i, 7)),
                  full(kc_hi), full(kc_lo), full(vc_t),
                  pl.BlockSpec((128, s), lambda b, i: (2, b)),
                  pl.BlockSpec(penalty.shape, lambda b, i: (0, 0)),
                  full(vs1),
                  pl.BlockSpec((128, s), lambda b, i: (3, b)),
                  full(vw1),
                  pl.BlockSpec(m_t.shape, lambda b, i: (0, 0)),
                  pl.BlockSpec((1, D_BRANCH), lambda b, i: (0, 0))],
        out_specs=pl.BlockSpec((1, qb, D_BRANCH), lambda b, i: (b, i, 0)),
        out_shape=jax.ShapeDtypeStruct((bsz, s, D_BRANCH), F32),
        scratch_shapes=[pltpu.VMEM((N_HEADS * qb, 128), F32), pltpu.VMEM((N_HEADS * qb, 128), F32),
                        pltpu.VMEM((N_HEADS * qb, min(NSA_KV_TILE, s)), F32)],
        compiler_params=_params(("parallel", "arbitrary")),
        name="nsa_attention",
    )(proj, proj, kc_hi, kc_lo, vc_t, kt_all, penalty, vs1, kt_all, vw1, m_t, gate_b_pad)


def _rwkv_kernel(c, pr_ref, pk_ref, pv_ref, pw_ref, mu_ref, w0_ref, wuph_ref, wupl_ref, a0_ref, auph_ref, aupl_ref,
                 kk_ref, ka_ref, rk_ref, lng_ref, lnb_ref, o_ref, prev_ref, st_ref):
    nb, blk, _ = pr_ref.shape
    rows = nb * blk
    n = nb * c

    @pl.when(pl.program_id(0) == 0)
    def _():
        prev_ref[...] = jnp.zeros_like(prev_ref)
        st_ref[...] = jnp.zeros_like(st_ref)

    brow = lax.broadcasted_iota(jnp.int32, (rows, 1), 0)

    def shifted(ref, k):
        x = ref[...].reshape(rows, D_BRANCH)
        prev = pltpu.roll(x, 1, axis=0)
        for b in range(nb):
            prev = jnp.where(brow == b * blk, prev_ref[k, b:b + 1, :], prev)
        for b in range(nb):
            prev_ref[k, b:b + 1, :] = x[(b + 1) * blk - 1:(b + 1) * blk, :]
        return x + (prev - x) * mu_ref[k:k + 1, :]

    r_all = shifted(pr_ref, 0)
    k_all = shifted(pk_ref, 1)
    v_all = shifted(pv_ref, 2)
    xw = shifted(pw_ref, 3)

    ones_bd = _block_ones()
    wlog = w0_ref[...] + _dot3_pre(jnp.tanh(xw), wuph_ref[...], wupl_ref[...])
    ld_all = -jnp.exp(-_softplus(-wlog) - 0.5)
    a_lr = _sigmoid(a0_ref[...] + _dot3_pre(xw, auph_ref[...], aupl_ref[...]))
    kk = k_all * kk_ref[...]
    kk = kk / jnp.maximum(jnp.sqrt(_dot_lx(kk * kk, ones_bd)), 1e-12)
    k_all = k_all * (1.0 + (a_lr - 1.0) * ka_ref[...])
    a_all = -kk
    b_all = kk * a_lr

    row = lax.broadcasted_iota(jnp.int32, (n, 1), 0)
    col = lax.broadcasted_iota(jnp.int32, (1, n), 1)
    rowb = row // c
    colb = col // c
    same = rowb == colb
    tri = jnp.where(same & (col <= row), 1.0, 0.0).astype(BF16)
    allm = jnp.where(same, 1.0, 0.0).astype(BF16)
    upper = same & (row < col)
    upper_eq = same & (row <= col)
    eye = jnp.where(row == col, 1.0, 0.0)
    col2b = jnp.concatenate([colb, colb], axis=1)
    hm = _head_masks()
    bd = ones_bd.astype(F32)
    mean_m = _block_ones(1.0 / HEAD_DIM)

    def chunk_rows(x, j):
        return jnp.concatenate([x[b * blk + j * c:b * blk + (j + 1) * c] for b in range(nb)], axis=0)

    def prepare_all(nchunk):
        ps = []
        for j in range(nchunk):
            r, k, v, ld, a, b = (chunk_rows(x, j) for x in (r_all, k_all, v_all, ld_all, a_all, b_all))
            ps.append(dict(r=r, k=k, v=v, ld=ld, a=a, b=b))
        for p in ps:
            p["cs"] = _dot_xl(tri, p["ld"])
            p["tot"] = _dot_xl(allm, p["ld"])
        for p in ps:
            cs, tot = p["cs"], p["tot"]
            a_t = p["a"] * jnp.exp(cs - p["ld"])
            r_t = p["r"] * jnp.exp(cs)
            inv = jnp.exp(-cs)
            fin = jnp.exp(tot - cs)
            lhs = jnp.concatenate([(p["b"] * inv) * hm[h] for h in range(N_HEADS)]
                                  + [(p["k"] * inv) * hm[h] for h in range(N_HEADS)], axis=0)
            rhs = jnp.concatenate([a_t, r_t], axis=0)
            p["small"] = _dg(lhs.astype(BF16), rhs.astype(BF16), 1, 1)
            ar_t = jnp.concatenate([a_t.T, r_t.T], axis=1)
            p["v_t"] = p["v"].T.astype(BF16)
            p["ar_b"] = [jnp.where(col2b == bb, ar_t, 0.0).astype(BF16) for bb in range(nb)]
            p["bk_b"] = [jnp.concatenate([jnp.where(rowb == bb, p["b"] * fin, 0.0),
                                          jnp.where(rowb == bb, p["k"] * fin, 0.0)], axis=0).astype(BF16)
                         for bb in range(nb)]
            p["g_rows"] = [jnp.exp(tot[bb * c:bb * c + 1, :]) for bb in range(nb)]
            p["bonus"] = _dot_lx(p["r"] * p["k"] * rk_ref[...], ones_bd) * p["v"]
        for p in ps:
            small = p["small"]
            p["pw"], p["tinv"], p["brs"], p["zs"], p["y0s"] = [], [], [], [], []
            for h in range(N_HEADS):
                ba = jnp.where(upper, small[h * n:(h + 1) * n, 0:n], 0.0)
                br = jnp.where(upper_eq, small[h * n:(h + 1) * n, n:2 * n], 0.0).astype(BF16)
                ka = jnp.where(upper, small[(N_HEADS + h) * n:(N_HEADS + h + 1) * n, 0:n], 0.0).astype(BF16)
                kr = jnp.where(upper_eq, small[(N_HEADS + h) * n:(N_HEADS + h + 1) * n, n:2 * n], 0.0).astype(BF16)
                vh = p["v_t"][h * HEAD_DIM:(h + 1) * HEAD_DIM, :]
                p["pw"].append(ba)
                p["tinv"].append(eye + ba)
                p["brs"].append(br)
                p["zs"].append(_dg(vh, ka))
                p["y0s"].append(_dg(vh, kr))
        span = 2
        while span < c:
            for p in ps:
                for h in range(N_HEADS):
                    pw_bf = p["pw"][h].astype(BF16)
                    p["pw"][h] = _dg(pw_bf, pw_bf)
            for p in ps:
                for h in range(N_HEADS):
                    p["tinv"][h] = p["tinv"][h] + _dg(p["tinv"][h].astype(BF16), p["pw"][h].astype(BF16))
            span *= 2
        for p in ps:
            p["tinvs"] = [t.astype(BF16) for t in p["tinv"]]
        return ps

    def advance(p):
        xy_t = jnp.zeros((D_BRANCH, 2 * n), F32)
        for bb in range(nb):
            xy_t = xy_t + _dg(st_ref[bb].astype(BF16), p["ar_b"][bb])
        u_rows, y_rows = [], []
        for h in range(N_HEADS):
            hs = slice(h * HEAD_DIM, (h + 1) * HEAD_DIM)
            uh = _dg((xy_t[hs, 0:n] + p["zs"][h]).astype(BF16), p["tinvs"][h]).astype(BF16)
            u_rows.append(uh)
            y_rows.append(xy_t[hs, n:2 * n] + _dg(uh, p["brs"][h]) + p["y0s"][h])
        uv = jnp.concatenate([jnp.concatenate(u_rows, axis=0), p["v_t"]], axis=1)
        for bb in range(nb):
            st_ref[bb] = st_ref[bb] * p["g_rows"][bb] + bd * _dg(uv, p["bk_b"][bb])
        return jnp.concatenate(y_rows, axis=0).T

    prepared = prepare_all(blk // c)
    for j, p in enumerate(prepared):
        y = advance(p)
        mean = _dot_lx(y, mean_m)
        yc = y - mean
        var = _dot_lx(yc * yc, mean_m)
        out = yc * lax.rsqrt(var + RWKV_GN_EPS) * lng_ref[...] + lnb_ref[...]
        o_ref[:, j * c:(j + 1) * c, :] = (out + p["bonus"]).reshape(nb, c, D_BRANCH)


def _rwkv_mixer(proj, mu, w0, w_up, a0, a_up, k_k, k_a, r_k, ln_g, ln_b):
    bsz, s, _ = proj.shape
    c = min(RWKV_CHUNK, s)
    tb = min(RWKV_BLOCK, s)
    mu_p = jnp.zeros((4, D_BRANCH), F32)
    mu_p = mu_p.at[0:3, :].set(mu[:3 * D_BRANCH].reshape(3, D_BRANCH))
    mu_p = mu_p.at[3, :2 * RWKV_RANK].set(mu[3 * D_BRANCH:])
    wup_h, wup_l = _split2(jnp.zeros((D_BRANCH, D_BRANCH), F32).at[:RWKV_RANK, :].set(w_up))
    aup_h, aup_l = _split2(jnp.zeros((D_BRANCH, D_BRANCH), F32).at[RWKV_RANK:2 * RWKV_RANK, :].set(a_up))
    row = lambda a: a.reshape(1, D_BRANCH)
    blk = lambda cidx: pl.BlockSpec((bsz, tb, D_BRANCH), lambda i, cidx=cidx: (0, i, cidx))
    rowspec = pl.BlockSpec((1, D_BRANCH), lambda i: (0, 0))
    matspec = pl.BlockSpec((D_BRANCH, D_BRANCH), lambda i: (0, 0))
    return pl.pallas_call(
        functools.partial(_rwkv_kernel, c),
        grid=(s // tb,),
        in_specs=[blk(8), blk(9), blk(10), blk(11),
                  pl.BlockSpec((4, D_BRANCH), lambda i: (0, 0)),
                  rowspec, matspec, matspec, rowspec, matspec, matspec,
                  rowspec, rowspec, rowspec, rowspec, rowspec],
        out_specs=pl.BlockSpec((bsz, tb, D_BRANCH), lambda i: (0, i, 0)),
        out_shape=jax.ShapeDtypeStruct((bsz, s, D_BRANCH), F32),
        scratch_shapes=[pltpu.VMEM((4, 8, D_BRANCH), F32), pltpu.VMEM((bsz, D_BRANCH, D_BRANCH), F32)],
        compiler_params=_params(("arbitrary",)),
        name="rwkv7",
    )(proj, proj, proj, proj, mu_p, row(w0), wup_h, wup_l, row(a0), aup_h, aup_l, row(k_k), row(k_a),
      row(r_k), row(ln_g), row(ln_b))


def _lane_sum_selector():
    r = lax.broadcasted_iota(jnp.int32, (128, 128), 0)
    return jnp.where(r == HEAD_DIM, 1.0, 0.0).astype(BF16)


def _placement(h):
    r = lax.broadcasted_iota(jnp.int32, (128, D_BRANCH), 0)
    c = lax.broadcasted_iota(jnp.int32, (128, D_BRANCH), 1)
    return jnp.where((r < HEAD_DIM) & (c == r + h * HEAD_DIM), 1.0, 0.0).astype(BF16)


def _diff_kernel(lambda_init, q_ref, k_ref, v_ref, lam_ref, g_ref, o_ref, acc_ref, m_ref, s_ref):
    qb = q_ref.shape[1]
    tk = min(DIFF_KV_TILE, k_ref.shape[1])
    rep = tk // 128
    q0 = pl.program_id(1) * qb
    lane = lax.broadcasted_iota(jnp.int32, (1, 128), 1)
    q = q_ref[0] * (DIFF_QK_DIM ** -0.5 * LOG2E)
    qs = []
    for h in range(N_HEADS):
        pair, half = divmod(h, 2)
        qp = q[:, pair * 128:(pair + 1) * 128]
        both = []
        for cc in range(2):
            lo = half * HEAD_DIM + cc * DIFF_QK_DIM
            both.append(jnp.where((lane >= lo) & (lane < lo + DIFF_QK_DIM), qp, 0.0).astype(BF16))
        qs.append(jnp.concatenate(both, axis=0))
    t = q0 + lax.broadcasted_iota(jnp.int32, (qb, 1), 0)
    t2 = jnp.concatenate([t, t], axis=0)
    m_ref[...] = jnp.full(m_ref.shape, NEG_INF, F32)
    acc_ref[...] = jnp.zeros(acc_ref.shape, F32)

    def scores(j, h):
        start = pl.multiple_of(j * tk, tk)
        return _dg(qs[h], k_ref[(h // 2) * 128:(h // 2 + 1) * 128, pl.ds(start, tk)])

    def absorb(j, h, s):
        start = pl.multiple_of(j * tk, tk)
        m_old = m_ref[h]
        m_new = jnp.maximum(m_old, jnp.max(s, axis=-1, keepdims=True))
        p = jnp.exp2(s - jnp.concatenate([m_new] * rep, axis=1))
        acc_ref[h] = jnp.exp2(m_old - m_new) * acc_ref[h] + _dg(p.astype(BF16), v_ref[0, h, pl.ds(start, tk), :])
        m_ref[h] = m_new

    n_full = q0 // tk
    n_diag = max(qb // tk, 1)
    for h in range(N_HEADS):
        s_ref[h] = scores(0, h)

    def body(j, carry):
        for h in range(N_HEADS):
            s = s_ref[h]
            s_ref[h] = scores(j + 1, h)
            absorb(j, h, s)
        return carry

    lax.fori_loop(0, n_full, body, 0)
    for u in range(n_diag):
        j = n_full + u
        allow = (pl.multiple_of(j * tk, tk) + lax.broadcasted_iota(jnp.int32, (1, tk), 1)) <= t2
        for h in range(N_HEADS):
            s = s_ref[h]
            if u + 1 < n_diag:
                s_ref[h] = scores(j + 1, h)
            absorb(j, h, jnp.where(allow, s, NEG_INF))

    lam = lam_ref[...]
    lam_full = (jnp.exp(jnp.sum(lam[0:1] * lam[1:2], axis=-1, keepdims=True))
                - jnp.exp(jnp.sum(lam[2:3] * lam[3:4], axis=-1, keepdims=True)) + lambda_init)
    lsel = _lane_sum_selector()
    out = jnp.zeros((qb, D_BRANCH), F32)
    for h in range(N_HEADS):
        a = acc_ref[h]
        a = a / _dot_lx(a, lsel)
        d = a[0:qb] - lam_full * a[qb:2 * qb]
        out = out + _dot_lx(d, _placement(h))
    ms = _dot_lx(out * out, _block_ones(1.0 / HEAD_DIM))
    o_ref[0] = out * lax.rsqrt(ms + DIFF_EPS) * g_ref[...] * (1.0 - lambda_init)


def _with_ones_lane(v):
    pad = jnp.zeros(v.shape[:-1] + (128 - HEAD_DIM,), v.dtype).at[..., 0].set(1.0)
    return jnp.concatenate([v, pad], axis=-1).astype(BF16)


def _diff_mixer(proj, kt_all, lam, subln_g, lambda_init):
    bsz, s, _ = proj.shape
    qb = min(DIFF_Q_BLOCK, s)
    v_heads = proj[:, :, 14 * D_BRANCH:15 * D_BRANCH].reshape(bsz, s, N_HEADS, HEAD_DIM).transpose(0, 2, 1, 3)
    v_heads = _with_ones_lane(v_heads)
    g_rep = jnp.tile(subln_g, N_HEADS).reshape(1, D_BRANCH)
    return pl.pallas_call(
        functools.partial(_diff_kernel, lambda_init),
        grid=(bsz, s // qb),
        in_specs=[pl.BlockSpec((1, qb, D_BRANCH), lambda b, i: (b, i, 12)),
                  pl.BlockSpec((D_BRANCH, s), lambda b, i: (0, b)),
                  pl.BlockSpec((1, N_HEADS, s, 128), lambda b, i: (b, 0, 0, 0)),
                  pl.BlockSpec(lam.shape, lambda b, i: (0, 0)),
                  pl.BlockSpec((1, D_BRANCH), lambda b, i: (0, 0))],
        out_specs=pl.BlockSpec((1, qb, D_BRANCH), lambda b, i: (b, i, 0)),
        out_shape=jax.ShapeDtypeStruct((bsz, s, D_BRANCH), F32),
        scratch_shapes=[pltpu.VMEM((N_HEADS, 2 * qb, 128), F32),
                        pltpu.VMEM((N_HEADS, 2 * qb, 128), F32),
                        pltpu.VMEM((N_HEADS, 2 * qb, min(DIFF_KV_TILE, s)), F32)],
        compiler_params=_params(("parallel", "arbitrary")),
        name="diff_attention",
    )(proj, kt_all, v_heads, lam, g_rep)


def _block_diag(w):
    n = w.shape[0] * w.shape[1]
    out = jnp.zeros((n, n), F32)
    for h in range(w.shape[0]):
        out = out.at[h * HEAD_DIM:(h + 1) * HEAD_DIM, h * HEAD_DIM:(h + 1) * HEAD_DIM].set(w[h])
    return out


def _layout_w_in(w):
    d = w.shape[0]
    nsa_end = 5 * D_BRANCH + D_BRANCH + 6 * HEAD_DIM + N_NSA_BRANCH * N_HEADS
    rwkv_end = nsa_end + 3 * D_BRANCH + 2 * RWKV_RANK
    pad1 = 8 * D_BRANCH - nsa_end
    pad2 = 12 * D_BRANCH - (rwkv_end + pad1)
    return jnp.concatenate([w[:, :nsa_end], jnp.zeros((d, pad1), w.dtype), w[:, nsa_end:rwkv_end],
                            jnp.zeros((d, pad2), w.dtype), w[:, rwkv_end:]], axis=1)


def _layout_wt_in(w):
    nsa0 = 5 * D_BRANCH
    ks0 = nsa0 + D_BRANCH + 2 * HEAD_DIM
    kw0 = ks0 + 2 * HEAD_DIM
    dk0 = w.shape[1] - 2 * D_BRANCH
    ks, kw = w[:, ks0:ks0 + HEAD_DIM], w[:, kw0:kw0 + HEAD_DIM]
    return jnp.concatenate([w[:, dk0:dk0 + D_BRANCH], ks, ks, kw, kw], axis=1).T


def kernel(x, norm_gain, w_in, w_out, final_gain, lru_conv_w, lru_conv_b, lru_wa, lru_ba, lru_wx, lru_bx, lru_lambda, nsa_cmp_pos, nsa_cmp_w1, nsa_cmp_w2, nsa_gate_b, rwkv_mu, rwkv_w0, rwkv_w_up, rwkv_a0, rwkv_a_up, rwkv_k_k, rwkv_k_a, rwkv_r_k, rwkv_ln_g, rwkv_ln_b, diff_lambda, diff_subln_g):
    bsz, s, d = x.shape
    depth = w_in.shape[0]
    t = bsz * s
    x2d = x.reshape(t, d)
    row = lambda a: a.reshape(1, -1)
    for l in range(depth):
        w_bf = w_in[l].astype(BF16)
        proj2d, kt_all = _in_proj(x2d, row(norm_gain[l]), _layout_w_in(w_bf), _layout_wt_in(w_bf))
        proj = proj2d.reshape(bsz, s, N_PROJ_BLOCKS * D_BRANCH)
        o_lru = _lru_mixer(proj, lru_conv_w[l], row(lru_conv_b[l]), _block_diag(lru_wa[l]), row(lru_ba[l]),
                           _block_diag(lru_wx[l]), row(lru_bx[l]), row(lru_lambda[l]))
        o_nsa = _nsa_mixer(proj, kt_all, nsa_cmp_pos[l], nsa_cmp_w1[l], nsa_cmp_w2[l], nsa_gate_b[l])
        o_rwkv = _rwkv_mixer(proj, rwkv_mu[l], rwkv_w0[l], rwkv_w_up[l], rwkv_a0[l], rwkv_a_up[l],
                             rwkv_k_k[l], rwkv_k_a[l], rwkv_r_k[l].reshape(-1), rwkv_ln_g[l], rwkv_ln_b[l])
        lambda_init = 0.8 - 0.6 * math.exp(-0.3 * l)
        o_diff = _diff_mixer(proj, kt_all, diff_lambda[l], diff_subln_g[l], lambda_init)
        mixers = [o.reshape(t, D_BRANCH) for o in (o_lru, o_nsa, o_rwkv, o_diff)]
        x2d = _out_proj(mixers, proj2d, x2d, w_out[l].astype(BF16), row(final_gain), final=(l == depth - 1))
    return x2d.reshape(bsz, s, d)
```

```python
import functools
import math

import jax
import jax.numpy as jnp
from jax import lax
from jax.experimental import pallas as pl
from jax.experimental.pallas import tpu as pltpu

F32 = jnp.float32
BF16 = jnp.bfloat16

N_MIX = 4
HEAD_DIM = 64
N_HEADS = 4
D_BRANCH = N_HEADS * HEAD_DIM
NORM_EPS = 1e-6
NEG_INF = -1e30
LRU_CONV = 4
LRU_C = 8.0
CMP_BLOCK = 32
CMP_STRIDE = 16
SEL_BLOCK = 64
N_SELECT = 16
WINDOW = 512
N_NSA_BRANCH = 3
FORCE_SCORE = 1e9
RWKV_RANK = 32
RWKV_GN_EPS = 64e-5
DIFF_QK_DIM = HEAD_DIM // 2
DIFF_EPS = 1e-5
N_PROJ_BLOCKS = 15
Q_BLOCK = 128
KV_TILE = 512
RWKV_CHUNK = 64
RWKV_BLOCK = 256
VMEM_LIMIT = 56 * 1024 * 1024
LOG2E = 1.4426950408889634
DIFF_Q_BLOCK = 256
DIFF_KV_TILE = 512
NSA_Q_BLOCK = 512
NSA_KV_TILE = 512
MASK_BIG = 1e30


def _dg(a, b, ca=1, cb=0):
    return lax.dot_general(a, b, (((ca,), (cb,)), ((), ())), preferred_element_type=F32)


def _split2(a):
    hi = a.astype(BF16)
    lo = (a - hi.astype(F32)).astype(BF16)
    return hi, lo


def _dot3(a, b, ca=1, cb=0):
    ah, al = _split2(a)
    bh, bl = _split2(b)
    return _dg(ah, bh, ca, cb) + (_dg(ah, bl, ca, cb) + _dg(al, bh, ca, cb))


def _dot3_pre(a, bh, bl, ca=1, cb=0):
    ah, al = _split2(a)
    return _dg(ah, bh, ca, cb) + (_dg(ah, bl, ca, cb) + _dg(al, bh, ca, cb))


def _dot_lx(a, b_exact, ca=1, cb=0):
    a1 = a.astype(BF16)
    r = a - a1.astype(F32)
    a2 = r.astype(BF16)
    a3 = (r - a2.astype(F32)).astype(BF16)
    return _dg(a1, b_exact, ca, cb) + (_dg(a2, b_exact, ca, cb) + _dg(a3, b_exact, ca, cb))


def _dot_lx2(a, b_exact, ca=1, cb=0):
    a1, a2 = _split2(a)
    return _dg(a1, b_exact, ca, cb) + _dg(a2, b_exact, ca, cb)


def _dot_xl(a_exact, b, ca=1, cb=0):
    b1 = b.astype(BF16)
    r = b - b1.astype(F32)
    b2 = r.astype(BF16)
    b3 = (r - b2.astype(F32)).astype(BF16)
    return _dg(a_exact, b1, ca, cb) + (_dg(a_exact, b2, ca, cb) + _dg(a_exact, b3, ca, cb))


def _head_masks(width=D_BRANCH, group=HEAD_DIM):
    lane = lax.broadcasted_iota(jnp.int32, (1, width), 1)
    return [((lane >= h * group) & (lane < (h + 1) * group)).astype(F32) for h in range(width // group)]


def _block_ones(scale=1.0):
    r = lax.broadcasted_iota(jnp.int32, (D_BRANCH, D_BRANCH), 0) // HEAD_DIM
    c = lax.broadcasted_iota(jnp.int32, (D_BRANCH, D_BRANCH), 1) // HEAD_DIM
    return jnp.where(r == c, scale, 0.0).astype(BF16)


def _sigmoid(x):
    return 1.0 / (1.0 + jnp.exp(-x))


def _softplus(x):
    return jnp.maximum(x, 0.0) + jnp.log1p(jnp.exp(-jnp.abs(x)))


def _params(sem):
    return pltpu.CompilerParams(dimension_semantics=sem, vmem_limit_bytes=VMEM_LIMIT)


def _proj_kernel(x_ref, g_ref, w_ref, wt_ref, wv_ref, o_ref, ot_ref, ov_ref):
    x = x_ref[...]
    ms = jnp.mean(x * x, axis=-1, keepdims=True)
    h = (x * lax.rsqrt(ms + NORM_EPS) * g_ref[...]).astype(BF16)
    o_ref[...] = _dg(h, w_ref[...])

    @pl.when(pl.program_id(1) == 0)
    def _():
        ot_ref[...] = _dg(wt_ref[...], h, 1, 1).astype(BF16)
        lane = lax.broadcasted_iota(jnp.int32, (1, wv_ref.shape[1]), 1)
        ov_ref[...] = jnp.where(lane % 128 == HEAD_DIM, 1.0, _dg(h, wv_ref[...])).astype(BF16)


def _in_proj(x2d, gain, w_bf, wt_bf, wv_bf, tm=1024, tn=768):
    t, d = x2d.shape
    n = w_bf.shape[1]
    nt = wt_bf.shape[0]
    nv = wv_bf.shape[1]
    tm = min(tm, t)
    return pl.pallas_call(
        _proj_kernel,
        grid=(t // tm, n // tn),
        in_specs=[pl.BlockSpec((tm, d), lambda i, j: (i, 0)),
                  pl.BlockSpec((1, d), lambda i, j: (0, 0)),
                  pl.BlockSpec((d, tn), lambda i, j: (0, j)),
                  pl.BlockSpec((nt, d), lambda i, j: (0, 0)),
                  pl.BlockSpec((d, nv), lambda i, j: (0, 0))],
        out_specs=[pl.BlockSpec((tm, tn), lambda i, j: (i, j)),
                   pl.BlockSpec((nt, tm), lambda i, j: (0, i)),
                   pl.BlockSpec((tm, nv), lambda i, j: (i, 0))],
        out_shape=[jax.ShapeDtypeStruct((t, n), F32), jax.ShapeDtypeStruct((nt, t), BF16),
                   jax.ShapeDtypeStruct((t, nv), BF16)],
        compiler_params=_params(("parallel", "arbitrary")),
        name="in_proj",
    )(x2d, gain, w_bf, wt_bf, wv_bf)


def _out_kernel(final, o1_ref, o2_ref, o3_ref, o4_ref, gate_ref, x_ref, w_ref, fg_ref, y_ref):
    mix = jnp.concatenate([o1_ref[...], o2_ref[...], o3_ref[...], o4_ref[...]], axis=-1)
    g = gate_ref[...]
    z = mix * (g * _sigmoid(g))
    y = x_ref[...] + _dg(z.astype(BF16), w_ref[...])
    if final:
        ms = jnp.mean(y * y, axis=-1, keepdims=True)
        y = y * lax.rsqrt(ms + NORM_EPS) * fg_ref[...]
    y_ref[...] = y


def _out_proj(mixers, proj2d, x2d, w_bf, final_gain, final, tm=512):
    t, d = x2d.shape
    dm = w_bf.shape[0]
    tm = min(tm, t)
    branch = pl.BlockSpec((tm, D_BRANCH), lambda i: (i, 0))
    return pl.pallas_call(
        functools.partial(_out_kernel, final),
        grid=(t // tm,),
        in_specs=[branch, branch, branch, branch,
                  pl.BlockSpec((tm, dm), lambda i: (i, 0)),
                  pl.BlockSpec((tm, d), lambda i: (i, 0)),
                  pl.BlockSpec((dm, d), lambda i: (0, 0)),
                  pl.BlockSpec((1, d), lambda i: (0, 0))],
        out_specs=pl.BlockSpec((tm, d), lambda i: (i, 0)),
        out_shape=jax.ShapeDtypeStruct((t, d), F32),
        compiler_params=_params(("parallel",)),
        name="out_proj",
    )(*mixers, proj2d, x2d, w_bf, final_gain)


def _lru_kernel(u_ref, cw_ref, cb_ref, wah_ref, wal_ref, ba_ref, wxh_ref, wxl_ref, bx_ref, lam_ref,
                o_ref, ubuf, hcar):
    ts = u_ref.shape[1]

    @pl.when(pl.program_id(1) == 0)
    def _():
        ubuf[0:8, :] = jnp.zeros((8, D_BRANCH), F32)
        hcar[...] = jnp.zeros_like(hcar)

    u = u_ref[0]
    ubuf[8:8 + ts, :] = u
    xc = cb_ref[...] + cw_ref[0:1, :] * ubuf[5:5 + ts, :]
    for j in range(1, LRU_CONV):
        xc = xc + cw_ref[j:j + 1, :] * ubuf[5 + j:5 + j + ts, :]
    ubuf[0:8, :] = u[ts - 8:ts, :]

    r = _sigmoid(_dot3_pre(xc, wah_ref[...], wal_ref[...]) + ba_ref[...])
    gi = _sigmoid(_dot3_pre(xc, wxh_ref[...], wxl_ref[...]) + bx_ref[...])
    log_a = (-LRU_C) * r * _softplus(-lam_ref[...])
    a = jnp.exp(log_a)
    b = jnp.sqrt(jnp.tanh(-log_a) * (a * a + 1.0)) * (gi * xc)

    row = lax.broadcasted_iota(jnp.int32, (ts, 1), 0)
    d = 1
    while d < ts:
        keep = row >= d
        a_s = jnp.where(keep, pltpu.roll(a, d, axis=0), 1.0)
        b_s = jnp.where(keep, pltpu.roll(b, d, axis=0), 0.0)
        b = a * b_s + b
        a = a * a_s
        d *= 2
    h = b + a * hcar[...]
    o_ref[0] = h
    hcar[...] = h[ts - 1:ts, :]


def _lru_mixer(proj, conv_w, conv_b, wa_bd, ba, wx_bd, bx, lam, ts=512):
    bsz, s, _ = proj.shape
    wah, wal = _split2(wa_bd)
    wxh, wxl = _split2(wx_bd)
    row = pl.BlockSpec((1, D_BRANCH), lambda b, j: (0, 0))
    mat = pl.BlockSpec((D_BRANCH, D_BRANCH), lambda b, j: (0, 0))
    return pl.pallas_call(
        _lru_kernel,
        grid=(bsz, s // ts),
        in_specs=[pl.BlockSpec((1, ts, D_BRANCH), lambda b, j: (b, j, 4)),
                  pl.BlockSpec((LRU_CONV, D_BRANCH), lambda b, j: (0, 0)),
                  row, mat, mat, row, mat, mat, row, row],
        out_specs=pl.BlockSpec((1, ts, D_BRANCH), lambda b, j: (b, j, 0)),
        out_shape=jax.ShapeDtypeStruct((bsz, s, D_BRANCH), F32),
        scratch_shapes=[pltpu.VMEM((ts + 8, D_BRANCH), F32), pltpu.VMEM((1, D_BRANCH), F32)],
        compiler_params=_params(("parallel", "arbitrary")),
        name="rg_lru",
    )(proj, conv_w, conv_b, wah, wal, ba, wxh, wxl, bx, lam)


def _cmp_kernel(rk_ref, rv_ref, pos_ref, w1_ref, w2_ref, kc_ref, vct_ref):
    half = (CMP_BLOCK // 2) * HEAD_DIM
    nrow = rk_ref.shape[1]
    outs = []
    for idx, r_ref in enumerate((rk_ref, rv_ref)):
        r = r_ref[0]
        w1 = w1_ref[idx]
        lo_half = _dot3(r, w1[0:half, :])
        hi_half = _dot3(r, w1[half:2 * half, :])
        bias = _dot3(jnp.broadcast_to(pos_ref[idx], (8, 2 * half)), w1)[0:1, :]
        hid = lo_half + pltpu.roll(hi_half, nrow - 1, axis=0) + bias
        outs.append(_dot3(jax.nn.gelu(hid), w2_ref[idx]))
    kc_ref[0] = outs[0].astype(BF16)
    vct_ref[0] = outs[1].T.astype(BF16)


def _nsa_compress(rk, rv, pos_flat, w1, w2_pair):
    bsz, nrow, width = rk.shape
    blk = pl.BlockSpec((1, nrow, width), lambda b: (b, 0, 0))
    out = pl.BlockSpec((1, nrow, 128), lambda b: (b, 0, 0))
    return pl.pallas_call(
        _cmp_kernel,
        grid=(bsz,),
        in_specs=[blk, blk,
                  pl.BlockSpec(pos_flat.shape, lambda b: (0, 0, 0)),
                  pl.BlockSpec(w1.shape, lambda b: (0, 0, 0)),
                  pl.BlockSpec(w2_pair.shape, lambda b: (0, 0, 0))],
        out_specs=[out, pl.BlockSpec((1, 128, nrow), lambda b: (b, 0, 0))],
        out_shape=[jax.ShapeDtypeStruct((bsz, nrow, 128), BF16), jax.ShapeDtypeStruct((bsz, 128, nrow), BF16)],
        compiler_params=_params(("parallel",)),
        name="nsa_compress",
    )(rk, rv, pos_flat, w1, w2_pair)


def _nsa_kernel(n_top, q_ref, g_ref, kc_ref, vct_ref, ks_ref, pen_ref, vs_ref, kw_ref, vw_ref,
                mt_ref, gb_ref, o_ref, acc_ref, m_ref, s_ref):
    qb = q_ref.shape[1]
    n_cmp = kc_ref.shape[1]
    n_sel = mt_ref.shape[0]
    tk = min(NSA_KV_TILE, vs_ref.shape[0])
    rep = tk // 128
    q0 = pl.program_id(1) * qb
    lane = lax.broadcasted_iota(jnp.int32, (1, 128), 1)
    q = q_ref[0] * (HEAD_DIM ** -0.5)
    parts = []
    for h in range(N_HEADS):
        pair, half = divmod(h, 2)
        keep = (lane >= half * HEAD_DIM) & (lane < (half + 1) * HEAD_DIM)
        parts.append(jnp.where(keep, q[:, pair * 128:(pair + 1) * 128], 0.0))
    q4 = jnp.concatenate(parts, axis=0)
    q4_log2 = (q4 * LOG2E).astype(BF16)
    t = q0 + lax.broadcasted_iota(jnp.int32, (qb, 1), 0)
    t4 = jnp.concatenate([t] * N_HEADS, axis=0)
    t_row = q0 + lax.broadcasted_iota(jnp.int32, (1, qb), 1)
    t4_row = jnp.concatenate([t_row] * N_HEADS, axis=1)

    sct = _dg(kc_ref[0], q4.astype(BF16), 1, 1)
    cend = lax.broadcasted_iota(jnp.int32, (n_cmp, 1), 0) * CMP_STRIDE + (CMP_BLOCK - 1)
    valid = cend <= t4_row
    sct = jnp.where(valid, sct, NEG_INF)
    e = jnp.exp(sct - jnp.max(sct, axis=0, keepdims=True))
    p_ct = e / jnp.sum(e, axis=0, keepdims=True)
    anyv = (t4_row >= CMP_BLOCK - 1).astype(F32)
    oc4 = (_dg(vct_ref[0], p_ct.astype(BF16)) * anyv).T
    p_cv = jnp.where(valid, p_ct, 0.0)
    imp_t = p_cv[:, 0:qb] + p_cv[:, qb:2 * qb] + p_cv[:, 2 * qb:3 * qb] + p_cv[:, 3 * qb:4 * qb]
    pslc_t = _dot_xl(mt_ref[...], imp_t)

    blk = lax.broadcasted_iota(jnp.int32, (n_sel, 1), 0)
    blk_f = blk.astype(F32)
    cur = t_row // SEL_BLOCK
    forced = (blk == 0) | (blk == cur) | (blk == cur - 1)
    score = jnp.where(blk > cur, -1.0, jnp.where(forced, FORCE_SCORE, pslc_t))
    nw = qb + WINDOW
    start_w = pl.multiple_of(jnp.maximum(q0 - WINDOW, 0), 128)
    wpos = start_w + lax.broadcasted_iota(jnp.int32, (1, nw), 1)
    allow_w = (wpos <= t) & (wpos > t - WINDOW)
    lsel = _lane_sum_selector()

    def window_head(h):
        s_w = _dg(q4_log2[h * qb:(h + 1) * qb], kw_ref[:, pl.ds(start_w, nw)])
        s_w = jnp.where(allow_w, s_w, NEG_INF)
        p_w = jnp.exp2(s_w - jnp.max(s_w, axis=-1, keepdims=True))
        aw = _dg(p_w.astype(BF16), vw_ref[pl.ds(start_w, nw), :])
        return aw / _dot_lx2(aw, lsel)

    ow_heads = []
    every = max(n_top // N_HEADS, 1)
    sel_t = jnp.zeros((n_sel, qb), F32)
    for rnd in range(n_top):
        best = jnp.max(score, axis=0, keepdims=True)
        idx = jnp.min(jnp.where(score == best, blk_f, float(n_sel)), axis=0, keepdims=True)
        pick = blk_f == idx
        sel_t = jnp.where(pick, 1.0, sel_t)
        score = jnp.where(pick, -3e38, score)
        if rnd % every == every - 1 and len(ow_heads) < N_HEADS:
            ow_heads.append(window_head(len(ow_heads)))
    while len(ow_heads) < N_HEADS:
        ow_heads.append(window_head(len(ow_heads)))
    notsel = (1.0 - sel_t).T.astype(BF16)
    lhs = jnp.concatenate([q4_log2, jnp.concatenate([notsel] * N_HEADS, axis=0)], axis=1)

    m_ref[...] = jnp.full(m_ref.shape, NEG_INF, F32)
    acc_ref[...] = jnp.zeros(acc_ref.shape, F32)

    def scores(j):
        start = pl.multiple_of(j * tk, tk)
        rhs = jnp.concatenate([ks_ref[:, pl.ds(start, tk)], pen_ref[:, pl.ds(start, tk)]], axis=0)
        return _dg(lhs, rhs)

    def absorb(j, s):
        start = pl.multiple_of(j * tk, tk)
        m_old = m_ref[...]
        m_new = jnp.maximum(m_old, jnp.max(s, axis=-1, keepdims=True))
        p = jnp.exp2(s - jnp.concatenate([m_new] * rep, axis=1))
        acc_ref[...] = jnp.exp2(m_old - m_new) * acc_ref[...] + _dg(p.astype(BF16), vs_ref[pl.ds(start, tk), :])
        m_ref[...] = m_new

    n_full = q0 // tk
    n_diag = max(qb // tk, 1)
    s_ref[...] = scores(0)

    def body(j, carry):
        s = s_ref[...]
        s_ref[...] = scores(j + 1)
        absorb(j, s)
        return carry

    lax.fori_loop(0, n_full, body, 0)
    for u in range(n_diag):
        j = n_full + u
        s = s_ref[...]
        if u + 1 < n_diag:
            s_ref[...] = scores(j + 1)
        allow = (pl.multiple_of(j * tk, tk) + lax.broadcasted_iota(jnp.int32, (1, tk), 1)) <= t4
        absorb(j, jnp.where(allow, s, NEG_INF))
    acc = acc_ref[...]
    os4 = acc / _dot_lx2(acc, lsel)

    gates = _sigmoid(g_ref[0] + gb_ref[...])
    out = jnp.zeros((qb, D_BRANCH), F32)
    for h in range(N_HEADS):
        rows = slice(h * qb, (h + 1) * qb)
        base = 128 + h * N_NSA_BRANCH
        mixed = (gates[:, base:base + 1] * oc4[rows]
                 + gates[:, base + 1:base + 2] * os4[rows]
                 + gates[:, base + 2:base + 3] * ow_heads[h])
        out = out + _dot_lx2(mixed, _placement(h))
    o_ref[0] = out


def _nsa_mixer(proj, kt_all, v_aug, cmp_pos, cmp_w1, cmp_w2, gate_b):
    bsz, s, _ = proj.shape
    n_row = s // CMP_STRIDE
    n_sel = s // SEL_BLOCK
    n_top = min(N_SELECT, n_sel)
    qb = min(NSA_Q_BLOCK, s)
    base = 6 * D_BRANCH
    col = lambda k: proj[:, :, base + k * HEAD_DIM: base + (k + 1) * HEAD_DIM]
    rk = col(0).reshape(bsz, n_row, CMP_STRIDE * HEAD_DIM)
    rv = col(1).reshape(bsz, n_row, CMP_STRIDE * HEAD_DIM)
    pos_flat = cmp_pos.reshape(2, 1, CMP_BLOCK * HEAD_DIM)
    w2_pair = jnp.stack([jnp.concatenate([cmp_w2[0], cmp_w2[0]], axis=1),
                         jnp.concatenate([cmp_w2[1], jnp.zeros_like(cmp_w2[1])], axis=1)])
    kc_bf, vc_t = _nsa_compress(rk, rv, pos_flat, cmp_w1, w2_pair)

    key_blk = jnp.arange(s, dtype=jnp.int32) // SEL_BLOCK
    penalty = jnp.where(jnp.arange(n_sel, dtype=jnp.int32)[:, None] == key_blk[None, :],
                        -MASK_BIG, 0.0).astype(BF16)
    ratio = SEL_BLOCK // CMP_STRIDE
    off = jnp.arange(n_row, dtype=jnp.int32)[None, :] - ratio * jnp.arange(n_sel, dtype=jnp.int32)[:, None]
    m_t = jnp.where((off == -1) | (off == 3), 1.0, jnp.where((off >= 0) & (off <= 2), 2.0, 0.0)).astype(BF16)
    gate_b_pad = jnp.zeros((1, D_BRANCH), F32).at[0, 128:128 + N_NSA_BRANCH * N_HEADS].set(gate_b)
    full = lambda arr: pl.BlockSpec((1,) + arr.shape[1:], lambda b, i: (b, 0, 0))
    return pl.pallas_call(
        functools.partial(_nsa_kernel, n_top),
        grid=(bsz, s // qb),
        in_specs=[pl.BlockSpec((1, qb, D_BRANCH), lambda b, i: (b, i, 5)),
                  pl.BlockSpec((1, qb, D_BRANCH), lambda b, i: (b, i, 7)),
                  full(kc_bf), full(vc_t),
                  pl.BlockSpec((128, s), lambda b, i: (2, b)),
                  pl.BlockSpec(penalty.shape, lambda b, i: (0, 0)),
                  pl.BlockSpec((s, 128), lambda b, i: (b, N_HEADS)),
                  pl.BlockSpec((128, s), lambda b, i: (3, b)),
                  pl.BlockSpec((s, 128), lambda b, i: (b, N_HEADS + 1)),
                  pl.BlockSpec(m_t.shape, lambda b, i: (0, 0)),
                  pl.BlockSpec((1, D_BRANCH), lambda b, i: (0, 0))],
        out_specs=pl.BlockSpec((1, qb, D_BRANCH), lambda b, i: (b, i, 0)),
        out_shape=jax.ShapeDtypeStruct((bsz, s, D_BRANCH), F32),
        scratch_shapes=[pltpu.VMEM((N_HEADS * qb, 128), F32), pltpu.VMEM((N_HEADS * qb, 128), F32),
                        pltpu.VMEM((N_HEADS * qb, min(NSA_KV_TILE, s)), F32)],
        compiler_params=_params(("parallel", "arbitrary")),
        name="nsa_attention",
    )(proj, proj, kc_bf, vc_t, kt_all, penalty, v_aug, kt_all, v_aug, m_t, gate_b_pad)


def _rwkv_kernel(c, pr_ref, pk_ref, pv_ref, pw_ref, mu_ref, w0_ref, wuph_ref, wupl_ref, a0_ref, auph_ref, aupl_ref,
                 kk_ref, ka_ref, rk_ref, lng_ref, lnb_ref, o_ref, prev_ref, st_ref):
    nb, blk, _ = pr_ref.shape
    rows = nb * blk
    n = nb * c

    @pl.when(pl.program_id(0) == 0)
    def _():
        prev_ref[...] = jnp.zeros_like(prev_ref)
        st_ref[...] = jnp.zeros_like(st_ref)

    brow = lax.broadcasted_iota(jnp.int32, (rows, 1), 0)

    def shifted(ref, k):
        x = ref[...].reshape(rows, D_BRANCH)
        prev = pltpu.roll(x, 1, axis=0)
        for b in range(nb):
            prev = jnp.where(brow == b * blk, prev_ref[k, b:b + 1, :], prev)
        for b in range(nb):
            prev_ref[k, b:b + 1, :] = x[(b + 1) * blk - 1:(b + 1) * blk, :]
        return x + (prev - x) * mu_ref[k:k + 1, :]

    r_all = shifted(pr_ref, 0)
    k_all = shifted(pk_ref, 1)
    v_all = shifted(pv_ref, 2)
    xw = shifted(pw_ref, 3)

    ones_bd = _block_ones()
    wlog = w0_ref[...] + _dot3_pre(jnp.tanh(xw), wuph_ref[...], wupl_ref[...])
    ld_all = -jnp.exp(-_softplus(-wlog) - 0.5)
    a_lr = _sigmoid(a0_ref[...] + _dot3_pre(xw, auph_ref[...], aupl_ref[...]))
    kk = k_all * kk_ref[...]
    kk = kk / jnp.maximum(jnp.sqrt(_dot_lx(kk * kk, ones_bd)), 1e-12)
    k_all = k_all * (1.0 + (a_lr - 1.0) * ka_ref[...])
    a_all = -kk
    b_all = kk * a_lr

    row = lax.broadcasted_iota(jnp.int32, (n, 1), 0)
    col = lax.broadcasted_iota(jnp.int32, (1, n), 1)
    rowb = row // c
    colb = col // c
    same = rowb == colb
    tri = jnp.where(same & (col <= row), 1.0, 0.0).astype(BF16)
    allm = jnp.where(same, 1.0, 0.0).astype(BF16)
    upper = same & (row < col)
    upper_eq = same & (row <= col)
    eye = jnp.where(row == col, 1.0, 0.0)
    col2b = jnp.concatenate([colb, colb], axis=1)
    hm = _head_masks()
    bd = ones_bd.astype(F32)
    mean_m = _block_ones(1.0 / HEAD_DIM)

    def chunk_rows(x, j):
        return jnp.concatenate([x[b * blk + j * c:b * blk + (j + 1) * c] for b in range(nb)], axis=0)

    def prepare_all(nchunk):
        ps = []
        for j in range(nchunk):
            r, k, v, ld, a, b = (chunk_rows(x, j) for x in (r_all, k_all, v_all, ld_all, a_all, b_all))
            ps.append(dict(r=r, k=k, v=v, ld=ld, a=a, b=b))
        for p in ps:
            p["cs"] = _dot_xl(tri, p["ld"])
            p["tot"] = _dot_xl(allm, p["ld"])
        for p in ps:
            cs, tot = p["cs"], p["tot"]
            a_t = p["a"] * jnp.exp(cs - p["ld"])
            r_t = p["r"] * jnp.exp(cs)
            inv = jnp.exp(-cs)
            fin = jnp.exp(tot - cs)
            lhs = jnp.concatenate([(p["b"] * inv) * hm[h] for h in range(N_HEADS)]
                                  + [(p["k"] * inv) * hm[h] for h in range(N_HEADS)], axis=0)
            rhs = jnp.concatenate([a_t, r_t], axis=0)
            p["small"] = _dg(lhs.astype(BF16), rhs.astype(BF16), 1, 1)
            ar_t = jnp.concatenate([a_t.T, r_t.T], axis=1)
            p["v_t"] = p["v"].T.astype(BF16)
            p["ar_b"] = [jnp.where(col2b == bb, ar_t, 0.0).astype(BF16) for bb in range(nb)]
            p["bk_b"] = [jnp.concatenate([jnp.where(rowb == bb, p["b"] * fin, 0.0),
                                          jnp.where(rowb == bb, p["k"] * fin, 0.0)], axis=0).astype(BF16)
                         for bb in range(nb)]
            p["g_rows"] = [jnp.exp(tot[bb * c:bb * c + 1, :]) for bb in range(nb)]
            p["bonus"] = _dot_lx(p["r"] * p["k"] * rk_ref[...], ones_bd) * p["v"]
        for p in ps:
            small = p["small"]
            p["pw"], p["tinv"], p["brs"], p["zs"], p["y0s"] = [], [], [], [], []
            for h in range(N_HEADS):
                ba = jnp.where(upper, small[h * n:(h + 1) * n, 0:n], 0.0)
                br = jnp.where(upper_eq, small[h * n:(h + 1) * n, n:2 * n], 0.0).astype(BF16)
                ka = jnp.where(upper, small[(N_HEADS + h) * n:(N_HEADS + h + 1) * n, 0:n], 0.0).astype(BF16)
                kr = jnp.where(upper_eq, small[(N_HEADS + h) * n:(N_HEADS + h + 1) * n, n:2 * n], 0.0).astype(BF16)
                vh = p["v_t"][h * HEAD_DIM:(h + 1) * HEAD_DIM, :]
                p["pw"].append(ba)
                p["tinv"].append(eye + ba)
                p["brs"].append(br)
                p["zs"].append(_dg(vh, ka))
                p["y0s"].append(_dg(vh, kr))
        span = 2
        while span < c:
            for p in ps:
                for h in range(N_HEADS):
                    pw_bf = p["pw"][h].astype(BF16)
                    p["pw"][h] = _dg(pw_bf, pw_bf)
            for p in ps:
                for h in range(N_HEADS):
                    p["tinv"][h] = p["tinv"][h] + _dg(p["tinv"][h].astype(BF16), p["pw"][h].astype(BF16))
            span *= 2
        for p in ps:
            p["tinvs"] = [t.astype(BF16) for t in p["tinv"]]
        return ps

    def advance(p):
        xy_t = jnp.zeros((D_BRANCH, 2 * n), F32)
        for bb in range(nb):
            xy_t = xy_t + _dg(st_ref[bb].astype(BF16), p["ar_b"][bb])
        u_rows, y_rows = [], []
        for h in range(N_HEADS):
            hs = slice(h * HEAD_DIM, (h + 1) * HEAD_DIM)
            uh = _dg((xy_t[hs, 0:n] + p["zs"][h]).astype(BF16), p["tinvs"][h]).astype(BF16)
            u_rows.append(uh)
            y_rows.append(xy_t[hs, n:2 * n] + _dg(uh, p["brs"][h]) + p["y0s"][h])
        uv = jnp.concatenate([jnp.concatenate(u_rows, axis=0), p["v_t"]], axis=1)
        for bb in range(nb):
            st_ref[bb] = st_ref[bb] * p["g_rows"][bb] + bd * _dg(uv, p["bk_b"][bb])
        return jnp.concatenate(y_rows, axis=0).T

    prepared = prepare_all(blk // c)
    for j, p in enumerate(prepared):
        y = advance(p)
        mean = _dot_lx(y, mean_m)
        yc = y - mean
        var = _dot_lx(yc * yc, mean_m)
        out = yc * lax.rsqrt(var + RWKV_GN_EPS) * lng_ref[...] + lnb_ref[...]
        o_ref[:, j * c:(j + 1) * c, :] = (out + p["bonus"]).reshape(nb, c, D_BRANCH)


def _rwkv_mixer(proj, mu, w0, w_up, a0, a_up, k_k, k_a, r_k, ln_g, ln_b):
    bsz, s, _ = proj.shape
    c = min(RWKV_CHUNK, s)
    tb = min(RWKV_BLOCK, s)
    mu_p = jnp.zeros((4, D_BRANCH), F32)
    mu_p = mu_p.at[0:3, :].set(mu[:3 * D_BRANCH].reshape(3, D_BRANCH))
    mu_p = mu_p.at[3, :2 * RWKV_RANK].set(mu[3 * D_BRANCH:])
    wup_h, wup_l = _split2(jnp.zeros((D_BRANCH, D_BRANCH), F32).at[:RWKV_RANK, :].set(w_up))
    aup_h, aup_l = _split2(jnp.zeros((D_BRANCH, D_BRANCH), F32).at[RWKV_RANK:2 * RWKV_RANK, :].set(a_up))
    row = lambda a: a.reshape(1, D_BRANCH)
    blk = lambda cidx: pl.BlockSpec((bsz, tb, D_BRANCH), lambda i, cidx=cidx: (0, i, cidx))
    rowspec = pl.BlockSpec((1, D_BRANCH), lambda i: (0, 0))
    matspec = pl.BlockSpec((D_BRANCH, D_BRANCH), lambda i: (0, 0))
    return pl.pallas_call(
        functools.partial(_rwkv_kernel, c),
        grid=(s // tb,),
        in_specs=[blk(8), blk(9), blk(10), blk(11),
                  pl.BlockSpec((4, D_BRANCH), lambda i: (0, 0)),
                  rowspec, matspec, matspec, rowspec, matspec, matspec,
                  rowspec, rowspec, rowspec, rowspec, rowspec],
        out_specs=pl.BlockSpec((bsz, tb, D_BRANCH), lambda i: (0, i, 0)),
        out_shape=jax.ShapeDtypeStruct((bsz, s, D_BRANCH), F32),
        scratch_shapes=[pltpu.VMEM((4, 8, D_BRANCH), F32), pltpu.VMEM((bsz, D_BRANCH, D_BRANCH), F32)],
        compiler_params=_params(("arbitrary",)),
        name="rwkv7",
    )(proj, proj, proj, proj, mu_p, row(w0), wup_h, wup_l, row(a0), aup_h, aup_l, row(k_k), row(k_a),
      row(r_k), row(ln_g), row(ln_b))


def _lane_sum_selector():
    r = lax.broadcasted_iota(jnp.int32, (128, 128), 0)
    return jnp.where(r == HEAD_DIM, 1.0, 0.0).astype(BF16)


def _placement(h):
    r = lax.broadcasted_iota(jnp.int32, (128, D_BRANCH), 0)
    c = lax.broadcasted_iota(jnp.int32, (128, D_BRANCH), 1)
    return jnp.where((r < HEAD_DIM) & (c == r + h * HEAD_DIM), 1.0, 0.0).astype(BF16)


def _diff_kernel(lambda_init, q_ref, k_ref, v_ref, lam_ref, g_ref, o_ref, acc_ref, m_ref, s_ref):
    qb = q_ref.shape[1]
    tk = min(DIFF_KV_TILE, k_ref.shape[1])
    rep = tk // 128
    q0 = pl.program_id(1) * qb
    lane = lax.broadcasted_iota(jnp.int32, (1, 128), 1)
    q = q_ref[0] * (DIFF_QK_DIM ** -0.5 * LOG2E)
    qs = []
    for h in range(N_HEADS):
        pair, half = divmod(h, 2)
        qp = q[:, pair * 128:(pair + 1) * 128]
        both = []
        for cc in range(2):
            lo = half * HEAD_DIM + cc * DIFF_QK_DIM
            both.append(jnp.where((lane >= lo) & (lane < lo + DIFF_QK_DIM), qp, 0.0).astype(BF16))
        qs.append(jnp.concatenate(both, axis=0))
    t = q0 + lax.broadcasted_iota(jnp.int32, (qb, 1), 0)
    t2 = jnp.concatenate([t, t], axis=0)
    m_ref[...] = jnp.full(m_ref.shape, NEG_INF, F32)
    acc_ref[...] = jnp.zeros(acc_ref.shape, F32)

    def scores(j, h):
        start = pl.multiple_of(j * tk, tk)
        return _dg(qs[h], k_ref[(h // 2) * 128:(h // 2 + 1) * 128, pl.ds(start, tk)])

    def absorb(j, h, s):
        start = pl.multiple_of(j * tk, tk)
        m_old = m_ref[h]
        m_new = jnp.maximum(m_old, jnp.max(s, axis=-1, keepdims=True))
        p = jnp.exp2(s - jnp.concatenate([m_new] * rep, axis=1))
        acc_ref[h] = jnp.exp2(m_old - m_new) * acc_ref[h] + _dg(p.astype(BF16), v_ref[pl.ds(start, tk), h * 128:(h + 1) * 128])
        m_ref[h] = m_new

    n_full = q0 // tk
    n_diag = max(qb // tk, 1)
    for h in range(N_HEADS):
        s_ref[h] = scores(0, h)

    def body(j, carry):
        for h in range(N_HEADS):
            s = s_ref[h]
            s_ref[h] = scores(j + 1, h)
            absorb(j, h, s)
        return carry

    lax.fori_loop(0, n_full, body, 0)
    for u in range(n_diag):
        j = n_full + u
        allow = (pl.multiple_of(j * tk, tk) + lax.broadcasted_iota(jnp.int32, (1, tk), 1)) <= t2
        for h in range(N_HEADS):
            s = s_ref[h]
            if u + 1 < n_diag:
                s_ref[h] = scores(j + 1, h)
            absorb(j, h, jnp.where(allow, s, NEG_INF))

    lam = lam_ref[...]
    lam_full = (jnp.exp(jnp.sum(lam[0:1] * lam[1:2], axis=-1, keepdims=True))
                - jnp.exp(jnp.sum(lam[2:3] * lam[3:4], axis=-1, keepdims=True)) + lambda_init)
    lsel = _lane_sum_selector()
    out = jnp.zeros((qb, D_BRANCH), F32)
    for h in range(N_HEADS):
        a = acc_ref[h]
        a = a / _dot_lx2(a, lsel)
        d = a[0:qb] - lam_full * a[qb:2 * qb]
        out = out + _dot_lx2(d, _placement(h))
    ms = _dot_lx(out * out, _block_ones(1.0 / HEAD_DIM))
    o_ref[0] = out * lax.rsqrt(ms + DIFF_EPS) * g_ref[...] * (1.0 - lambda_init)


def _diff_mixer(proj, kt_all, v_aug, lam, subln_g, lambda_init):
    bsz, s, _ = proj.shape
    qb = min(DIFF_Q_BLOCK, s)
    g_rep = jnp.tile(subln_g, N_HEADS).reshape(1, D_BRANCH)
    return pl.pallas_call(
        functools.partial(_diff_kernel, lambda_init),
        grid=(bsz, s // qb),
        in_specs=[pl.BlockSpec((1, qb, D_BRANCH), lambda b, i: (b, i, 12)),
                  pl.BlockSpec((D_BRANCH, s), lambda b, i: (0, b)),
                  pl.BlockSpec((s, N_HEADS * 128), lambda b, i: (b, 0)),
                  pl.BlockSpec(lam.shape, lambda b, i: (0, 0)),
                  pl.BlockSpec((1, D_BRANCH), lambda b, i: (0, 0))],
        out_specs=pl.BlockSpec((1, qb, D_BRANCH), lambda b, i: (b, i, 0)),
        out_shape=jax.ShapeDtypeStruct((bsz, s, D_BRANCH), F32),
        scratch_shapes=[pltpu.VMEM((N_HEADS, 2 * qb, 128), F32),
                        pltpu.VMEM((N_HEADS, 2 * qb, 128), F32),
                        pltpu.VMEM((N_HEADS, 2 * qb, min(DIFF_KV_TILE, s)), F32)],
        compiler_params=_params(("parallel", "arbitrary")),
        name="diff_attention",
    )(proj, kt_all, v_aug, lam, g_rep)


def _block_diag(w):
    n = w.shape[0] * w.shape[1]
    out = jnp.zeros((n, n), F32)
    for h in range(w.shape[0]):
        out = out.at[h * HEAD_DIM:(h + 1) * HEAD_DIM, h * HEAD_DIM:(h + 1) * HEAD_DIM].set(w[h])
    return out


def _layout_w_in(w):
    d = w.shape[0]
    nsa_end = 5 * D_BRANCH + D_BRANCH + 6 * HEAD_DIM + N_NSA_BRANCH * N_HEADS
    rwkv_end = nsa_end + 3 * D_BRANCH + 2 * RWKV_RANK
    pad1 = 8 * D_BRANCH - nsa_end
    pad2 = 12 * D_BRANCH - (rwkv_end + pad1)
    return jnp.concatenate([w[:, :nsa_end], jnp.zeros((d, pad1), w.dtype), w[:, nsa_end:rwkv_end],
                            jnp.zeros((d, pad2), w.dtype), w[:, rwkv_end:]], axis=1)


def _layout_wt_in(w):
    nsa0 = 5 * D_BRANCH
    ks0 = nsa0 + D_BRANCH + 2 * HEAD_DIM
    kw0 = ks0 + 2 * HEAD_DIM
    dk0 = w.shape[1] - 2 * D_BRANCH
    ks, kw = w[:, ks0:ks0 + HEAD_DIM], w[:, kw0:kw0 + HEAD_DIM]
    return jnp.concatenate([w[:, dk0:dk0 + D_BRANCH], ks, ks, kw, kw], axis=1).T


def _layout_wv_in(w):
    nsa0 = 5 * D_BRANCH
    vs0 = nsa0 + D_BRANCH + 3 * HEAD_DIM
    vw0 = vs0 + 2 * HEAD_DIM
    dv0 = w.shape[1] - D_BRANCH
    pad = jnp.zeros((w.shape[0], 128 - HEAD_DIM), w.dtype)
    cols = [w[:, dv0 + h * HEAD_DIM:dv0 + (h + 1) * HEAD_DIM] for h in range(N_HEADS)]
    cols += [w[:, vs0:vs0 + HEAD_DIM], w[:, vw0:vw0 + HEAD_DIM]]
    return jnp.concatenate([piece for c in cols for piece in (c, pad)], axis=1)


def kernel(x, norm_gain, w_in, w_out, final_gain, lru_conv_w, lru_conv_b, lru_wa, lru_ba, lru_wx, lru_bx, lru_lambda, nsa_cmp_pos, nsa_cmp_w1, nsa_cmp_w2, nsa_gate_b, rwkv_mu, rwkv_w0, rwkv_w_up, rwkv_a0, rwkv_a_up, rwkv_k_k, rwkv_k_a, rwkv_r_k, rwkv_ln_g, rwkv_ln_b, diff_lambda, diff_subln_g):
    bsz, s, d = x.shape
    depth = w_in.shape[0]
    t = bsz * s
    x2d = x.reshape(t, d)
    row = lambda a: a.reshape(1, -1)
    w_in_bf = w_in.astype(BF16)
    for l in range(depth):
        w_bf = w_in_bf[l]
        proj2d, kt_all, v_aug = _in_proj(x2d, row(norm_gain[l]), _layout_w_in(w_bf), _layout_wt_in(w_bf),
                                         _layout_wv_in(w_bf))
        proj = proj2d.reshape(bsz, s, N_PROJ_BLOCKS * D_BRANCH)
        o_lru = _lru_mixer(proj, lru_conv_w[l], row(lru_conv_b[l]), _block_diag(lru_wa[l]), row(lru_ba[l]),
                           _block_diag(lru_wx[l]), row(lru_bx[l]), row(lru_lambda[l]))
        o_nsa = _nsa_mixer(proj, kt_all, v_aug, nsa_cmp_pos[l], nsa_cmp_w1[l], nsa_cmp_w2[l], nsa_gate_b[l])
        o_rwkv = _rwkv_mixer(proj, rwkv_mu[l], rwkv_w0[l], rwkv_w_up[l], rwkv_a0[l], rwkv_a_up[l],
                             rwkv_k_k[l], rwkv_k_a[l], rwkv_r_k[l].reshape(-1), rwkv_ln_g[l], rwkv_ln_b[l])
        lambda_init = 0.8 - 0.6 * math.exp(-0.3 * l)
        o_diff = _diff_mixer(proj, kt_all, v_aug, diff_lambda[l], diff_subln_g[l], lambda_init)
        mixers = [o.reshape(t, D_BRANCH) for o in (o_lru, o_nsa, o_rwkv, o_diff)]
        x2d = _out_proj(mixers, proj2d, x2d, w_out[l].astype(BF16), row(final_gain), final=(l == depth - 1))
    return x2d.reshape(bsz, s, d)
```

```python
import functools
import math

import jax
import jax.numpy as jnp
from jax import lax
from jax.experimental import pallas as pl
from jax.experimental.pallas import tpu as pltpu

F32 = jnp.float32
BF16 = jnp.bfloat16

N_MIX = 4
HEAD_DIM = 64
N_HEADS = 4
D_BRANCH = N_HEADS * HEAD_DIM
NORM_EPS = 1e-6
NEG_INF = -1e30
LRU_CONV = 4
LRU_C = 8.0
CMP_BLOCK = 32
CMP_STRIDE = 16
SEL_BLOCK = 64
N_SELECT = 16
WINDOW = 512
N_NSA_BRANCH = 3
RWKV_RANK = 32
RWKV_GN_EPS = 64e-5
DIFF_QK_DIM = HEAD_DIM // 2
DIFF_EPS = 1e-5
N_PROJ_BLOCKS = 13
Q_BLOCK = 128
KV_TILE = 512
RWKV_CHUNK = 64
RWKV_BLOCK = 512
RWKV_GROUP = 4
VMEM_LIMIT = 56 * 1024 * 1024
LOG2E = 1.4426950408889634
DIFF_Q_BLOCK = 256
DIFF_KV_TILE = 512
NSA_Q_BLOCK = 512
NSA_KV_TILE = 512
MASK_BIG = 1e30


def _dg(a, b, ca=1, cb=0):
    return lax.dot_general(a, b, (((ca,), (cb,)), ((), ())), preferred_element_type=F32)


def _split2(a):
    hi = a.astype(BF16)
    lo = (a - hi.astype(F32)).astype(BF16)
    return hi, lo


def _dot3(a, b, ca=1, cb=0):
    ah, al = _split2(a)
    bh, bl = _split2(b)
    return _dg(ah, bh, ca, cb) + (_dg(ah, bl, ca, cb) + _dg(al, bh, ca, cb))


def _dot3_pre(a, bh, bl, ca=1, cb=0):
    ah, al = _split2(a)
    return _dg(ah, bh, ca, cb) + (_dg(ah, bl, ca, cb) + _dg(al, bh, ca, cb))


def _dot_lx(a, b_exact, ca=1, cb=0):
    a1 = a.astype(BF16)
    r = a - a1.astype(F32)
    a2 = r.astype(BF16)
    a3 = (r - a2.astype(F32)).astype(BF16)
    return _dg(a1, b_exact, ca, cb) + (_dg(a2, b_exact, ca, cb) + _dg(a3, b_exact, ca, cb))


def _dot_lx2(a, b_exact, ca=1, cb=0):
    a1, a2 = _split2(a)
    return _dg(a1, b_exact, ca, cb) + _dg(a2, b_exact, ca, cb)


def _dot_xl(a_exact, b, ca=1, cb=0):
    b1 = b.astype(BF16)
    r = b - b1.astype(F32)
    b2 = r.astype(BF16)
    b3 = (r - b2.astype(F32)).astype(BF16)
    return _dg(a_exact, b1, ca, cb) + (_dg(a_exact, b2, ca, cb) + _dg(a_exact, b3, ca, cb))


def _head_masks(width=D_BRANCH, group=HEAD_DIM):
    lane = lax.broadcasted_iota(jnp.int32, (1, width), 1)
    return [((lane >= h * group) & (lane < (h + 1) * group)).astype(F32) for h in range(width // group)]


def _block_ones(scale=1.0):
    r = lax.broadcasted_iota(jnp.int32, (D_BRANCH, D_BRANCH), 0) // HEAD_DIM
    c = lax.broadcasted_iota(jnp.int32, (D_BRANCH, D_BRANCH), 1) // HEAD_DIM
    return jnp.where(r == c, scale, 0.0).astype(BF16)


def _sigmoid(x):
    return 1.0 / (1.0 + jnp.exp(-x))


def _softplus(x):
    return jnp.maximum(x, 0.0) + jnp.log1p(jnp.exp(-jnp.abs(x)))


def _params(sem):
    return pltpu.CompilerParams(dimension_semantics=sem, vmem_limit_bytes=VMEM_LIMIT)


def _proj_kernel(x_ref, g_ref, w_ref, wt_ref, wv_ref, o_ref, ot_ref, ov_ref):
    x = x_ref[...]
    ms = jnp.mean(x * x, axis=-1, keepdims=True)
    h = (x * lax.rsqrt(ms + NORM_EPS) * g_ref[...]).astype(BF16)
    o_ref[...] = _dg(h, w_ref[...])

    @pl.when(pl.program_id(1) == 0)
    def _():
        ot_ref[...] = _dg(wt_ref[...], h, 1, 1).astype(BF16)
        lane = lax.broadcasted_iota(jnp.int32, (1, wv_ref.shape[1]), 1)
        ov_ref[...] = jnp.where(lane % 128 == HEAD_DIM, 1.0, _dg(h, wv_ref[...])).astype(BF16)


def _in_proj(x2d, gain, w_bf, wt_bf, wv_bf, tm=1024, tn=N_PROJ_BLOCKS * 128):
    t, d = x2d.shape
    n = w_bf.shape[1]
    nt = wt_bf.shape[0]
    nv = wv_bf.shape[1]
    tm = min(tm, t)
    return pl.pallas_call(
        _proj_kernel,
        grid=(t // tm, n // tn),
        in_specs=[pl.BlockSpec((tm, d), lambda i, j: (i, 0)),
                  pl.BlockSpec((1, d), lambda i, j: (0, 0)),
                  pl.BlockSpec((d, tn), lambda i, j: (0, j)),
                  pl.BlockSpec((nt, d), lambda i, j: (0, 0)),
                  pl.BlockSpec((d, nv), lambda i, j: (0, 0))],
        out_specs=[pl.BlockSpec((tm, tn), lambda i, j: (i, j)),
                   pl.BlockSpec((nt, tm), lambda i, j: (0, i)),
                   pl.BlockSpec((tm, nv), lambda i, j: (i, 0))],
        out_shape=[jax.ShapeDtypeStruct((t, n), F32), jax.ShapeDtypeStruct((nt, t), BF16),
                   jax.ShapeDtypeStruct((t, nv), BF16)],
        compiler_params=_params(("parallel", "arbitrary")),
        name="in_proj",
    )(x2d, gain, w_bf, wt_bf, wv_bf)


def _out_kernel(final, o1_ref, o2_ref, o3_ref, o4_ref, gate_ref, x_ref, w_ref, fg_ref, y_ref):
    mix = jnp.concatenate([o1_ref[...], o2_ref[...], o3_ref[...], o4_ref[...]], axis=-1)
    g = gate_ref[...]
    z = mix * (g * _sigmoid(g))
    y = x_ref[...] + _dg(z.astype(BF16), w_ref[...])
    if final:
        ms = jnp.mean(y * y, axis=-1, keepdims=True)
        y = y * lax.rsqrt(ms + NORM_EPS) * fg_ref[...]
    y_ref[...] = y


def _out_proj(mixers, proj2d, x2d, w_bf, final_gain, final, tm=512):
    t, d = x2d.shape
    dm = w_bf.shape[0]
    tm = min(tm, t)
    branch = pl.BlockSpec((tm, D_BRANCH), lambda i: (i, 0))
    return pl.pallas_call(
        functools.partial(_out_kernel, final),
        grid=(t // tm,),
        in_specs=[branch, branch, branch, branch,
                  pl.BlockSpec((tm, dm), lambda i: (i, 0)),
                  pl.BlockSpec((tm, d), lambda i: (i, 0)),
                  pl.BlockSpec((dm, d), lambda i: (0, 0)),
                  pl.BlockSpec((1, d), lambda i: (0, 0))],
        out_specs=pl.BlockSpec((tm, d), lambda i: (i, 0)),
        out_shape=jax.ShapeDtypeStruct((t, d), F32),
        compiler_params=_params(("parallel",)),
        name="out_proj",
    )(*mixers, proj2d, x2d, w_bf, final_gain)


def _lru_kernel(u_ref, cw_ref, cb_ref, wah_ref, wal_ref, ba_ref, wxh_ref, wxl_ref, bx_ref, lam_ref,
                o_ref, ubuf, hcar):
    ts = u_ref.shape[1]

    @pl.when(pl.program_id(1) == 0)
    def _():
        ubuf[0:8, :] = jnp.zeros((8, D_BRANCH), F32)
        hcar[...] = jnp.zeros_like(hcar)

    u = u_ref[0]
    ubuf[8:8 + ts, :] = u
    xc = cb_ref[...] + cw_ref[0:1, :] * ubuf[5:5 + ts, :]
    for j in range(1, LRU_CONV):
        xc = xc + cw_ref[j:j + 1, :] * ubuf[5 + j:5 + j + ts, :]
    ubuf[0:8, :] = u[ts - 8:ts, :]

    r = _sigmoid(_dot3_pre(xc, wah_ref[...], wal_ref[...]) + ba_ref[...])
    gi = _sigmoid(_dot3_pre(xc, wxh_ref[...], wxl_ref[...]) + bx_ref[...])
    log_a = (-LRU_C) * r * _softplus(-lam_ref[...])
    a = jnp.exp(log_a)
    b = jnp.sqrt(jnp.tanh(-log_a) * (a * a + 1.0)) * (gi * xc)

    row = lax.broadcasted_iota(jnp.int32, (ts, 1), 0)
    d = 1
    while d < ts:
        keep = row >= d
        a_s = jnp.where(keep, pltpu.roll(a, d, axis=0), 1.0)
        b_s = jnp.where(keep, pltpu.roll(b, d, axis=0), 0.0)
        b = a * b_s + b
        a = a * a_s
        d *= 2
    h = b + a * hcar[...]
    o_ref[0] = h
    hcar[...] = h[ts - 1:ts, :]


def _lru_mixer(proj, conv_w, conv_b, wa_bd, ba, wx_bd, bx, lam, ts=512):
    bsz, s, _ = proj.shape
    wah, wal = _split2(wa_bd)
    wxh, wxl = _split2(wx_bd)
    row = pl.BlockSpec((1, D_BRANCH), lambda b, j: (0, 0))
    mat = pl.BlockSpec((D_BRANCH, D_BRANCH), lambda b, j: (0, 0))
    return pl.pallas_call(
        _lru_kernel,
        grid=(bsz, s // ts),
        in_specs=[pl.BlockSpec((1, ts, D_BRANCH), lambda b, j: (b, j, 4)),
                  pl.BlockSpec((LRU_CONV, D_BRANCH), lambda b, j: (0, 0)),
                  row, mat, mat, row, mat, mat, row, row],
        out_specs=pl.BlockSpec((1, ts, D_BRANCH), lambda b, j: (b, j, 0)),
        out_shape=jax.ShapeDtypeStruct((bsz, s, D_BRANCH), F32),
        scratch_shapes=[pltpu.VMEM((ts + 8, D_BRANCH), F32), pltpu.VMEM((1, D_BRANCH), F32)],
        compiler_params=_params(("parallel", "arbitrary")),
        name="rg_lru",
    )(proj, conv_w, conv_b, wah, wal, ba, wxh, wxl, bx, lam)


def _cmp_kernel(rk_ref, rv_ref, pos_ref, w1_ref, w2_ref, kc_ref, vct_ref):
    half = (CMP_BLOCK // 2) * HEAD_DIM
    nrow = rk_ref.shape[1]
    outs = []
    for idx, r_ref in enumerate((rk_ref, rv_ref)):
        r = r_ref[0]
        w1 = w1_ref[idx]
        lo_half = _dot3(r, w1[0:half, :])
        hi_half = _dot3(r, w1[half:2 * half, :])
        bias = _dot3(jnp.broadcast_to(pos_ref[idx], (8, 2 * half)), w1)[0:1, :]
        hid = lo_half + pltpu.roll(hi_half, nrow - 1, axis=0) + bias
        outs.append(_dot3(jax.nn.gelu(hid), w2_ref[idx]))
    kc_ref[0] = outs[0].astype(BF16)
    vct_ref[0] = outs[1].T.astype(BF16)


def _nsa_compress(rk, rv, pos_flat, w1, w2_pair):
    bsz, nrow, width = rk.shape
    blk = pl.BlockSpec((1, nrow, width), lambda b: (b, 0, 0))
    out = pl.BlockSpec((1, nrow, 128), lambda b: (b, 0, 0))
    return pl.pallas_call(
        _cmp_kernel,
        grid=(bsz,),
        in_specs=[blk, blk,
                  pl.BlockSpec(pos_flat.shape, lambda b: (0, 0, 0)),
                  pl.BlockSpec(w1.shape, lambda b: (0, 0, 0)),
                  pl.BlockSpec(w2_pair.shape, lambda b: (0, 0, 0))],
        out_specs=[out, pl.BlockSpec((1, 128, nrow), lambda b: (b, 0, 0))],
        out_shape=[jax.ShapeDtypeStruct((bsz, nrow, 128), BF16), jax.ShapeDtypeStruct((bsz, 128, nrow), BF16)],
        compiler_params=_params(("parallel",)),
        name="nsa_compress",
    )(rk, rv, pos_flat, w1, w2_pair)


def _nsa_kernel(n_top, q_ref, g_ref, kc_ref, vct_ref, ks_ref, pen_ref, vs_ref, kw_ref, vw_ref,
                mt_ref, gb_ref, o_ref, acc_ref, m_ref, s_ref):
    qb = q_ref.shape[1]
    n_cmp = kc_ref.shape[1]
    n_sel = mt_ref.shape[0]
    tk = min(NSA_KV_TILE, vs_ref.shape[0])
    rep = tk // 128
    q0 = pl.program_id(1) * qb
    lane = lax.broadcasted_iota(jnp.int32, (1, 128), 1)
    q = q_ref[0] * (HEAD_DIM ** -0.5)
    parts = []
    for h in range(N_HEADS):
        pair, half = divmod(h, 2)
        keep = (lane >= half * HEAD_DIM) & (lane < (half + 1) * HEAD_DIM)
        parts.append(jnp.where(keep, q[:, pair * 128:(pair + 1) * 128], 0.0))
    q4 = jnp.concatenate(parts, axis=0)
    q4_log2 = (q4 * LOG2E).astype(BF16)
    t = q0 + lax.broadcasted_iota(jnp.int32, (qb, 1), 0)
    t4 = jnp.concatenate([t] * N_HEADS, axis=0)
    t_row = q0 + lax.broadcasted_iota(jnp.int32, (1, qb), 1)
    t4_row = jnp.concatenate([t_row] * N_HEADS, axis=1)

    sct = _dg(kc_ref[0], q4.astype(BF16), 1, 1)
    cend = lax.broadcasted_iota(jnp.int32, (n_cmp, 1), 0) * CMP_STRIDE + (CMP_BLOCK - 1)
    valid = cend <= t4_row
    sct = jnp.where(valid, sct, NEG_INF)
    e = jnp.exp(sct - jnp.max(sct, axis=0, keepdims=True))
    p_ct = e / jnp.sum(e, axis=0, keepdims=True)
    anyv = (t4_row >= CMP_BLOCK - 1).astype(F32)
    oc4 = (_dg(vct_ref[0], p_ct.astype(BF16)) * anyv).T
    p_cv = jnp.where(valid, p_ct, 0.0)
    imp_t = p_cv[:, 0:qb] + p_cv[:, qb:2 * qb] + p_cv[:, 2 * qb:3 * qb] + p_cv[:, 3 * qb:4 * qb]
    pslc_t = _dot_xl(mt_ref[...], imp_t)

    blk = lax.broadcasted_iota(jnp.int32, (n_sel, 1), 0)
    blk_f = blk.astype(F32)
    cur = t_row // SEL_BLOCK
    forced = (blk == 0) | (blk == cur) | (blk == cur - 1)
    n_forced = 3
    score = jnp.where(blk > cur, -1.0, jnp.where(forced, -3e38, pslc_t))
    nw = qb + WINDOW
    start_w = pl.multiple_of(jnp.maximum(q0 - WINDOW, 0), 128)
    wpos = start_w + lax.broadcasted_iota(jnp.int32, (1, nw), 1)
    allow_w = (wpos <= t) & (wpos > t - WINDOW)
    lsel = _lane_sum_selector()

    def window_head(h):
        s_w = _dg(q4_log2[h * qb:(h + 1) * qb], kw_ref[:, pl.ds(start_w, nw)])
        s_w = jnp.where(allow_w, s_w, NEG_INF)
        p_w = jnp.exp2(s_w - jnp.max(s_w, axis=-1, keepdims=True))
        aw = _dg(p_w.astype(BF16), vw_ref[pl.ds(start_w, nw), :])
        return aw / _dot_lx2(aw, lsel)

    ow_heads = []
    n_rounds = max(n_top - n_forced, 0)
    every = max(n_rounds // N_HEADS, 1)
    sel_t = jnp.where(forced, 1.0, 0.0)
    for rnd in range(n_rounds):
        best = jnp.max(score, axis=0, keepdims=True)
        idx = jnp.min(jnp.where(score == best, blk_f, float(n_sel)), axis=0, keepdims=True)
        pick = blk_f == idx
        sel_t = jnp.where(pick, 1.0, sel_t)
        score = jnp.where(pick, -3e38, score)
        if rnd % every == every - 1 and len(ow_heads) < N_HEADS:
            ow_heads.append(window_head(len(ow_heads)))
    while len(ow_heads) < N_HEADS:
        ow_heads.append(window_head(len(ow_heads)))
    notsel = (1.0 - sel_t).T.astype(BF16)
    lhs = jnp.concatenate([q4_log2, jnp.concatenate([notsel] * N_HEADS, axis=0)], axis=1)

    m_ref[...] = jnp.full(m_ref.shape, NEG_INF, F32)
    acc_ref[...] = jnp.zeros(acc_ref.shape, F32)

    def scores(j):
        start = pl.multiple_of(j * tk, tk)
        rhs = jnp.concatenate([ks_ref[:, pl.ds(start, tk)], pen_ref[:, pl.ds(start, tk)]], axis=0)
        return _dg(lhs, rhs)

    def absorb(j, s):
        start = pl.multiple_of(j * tk, tk)
        m_old = m_ref[...]
        m_new = jnp.maximum(m_old, jnp.max(s, axis=-1, keepdims=True))
        p = jnp.exp2(s - jnp.concatenate([m_new] * rep, axis=1))
        acc_ref[...] = jnp.exp2(m_old - m_new) * acc_ref[...] + _dg(p.astype(BF16), vs_ref[pl.ds(start, tk), :])
        m_ref[...] = m_new

    n_full = q0 // tk
    n_diag = max(qb // tk, 1)
    s_ref[...] = scores(0)

    def body(j, carry):
        s = s_ref[...]
        s_ref[...] = scores(j + 1)
        absorb(j, s)
        return carry

    lax.fori_loop(0, n_full, body, 0)
    for u in range(n_diag):
        j = n_full + u
        s = s_ref[...]
        if u + 1 < n_diag:
            s_ref[...] = scores(j + 1)
        allow = (pl.multiple_of(j * tk, tk) + lax.broadcasted_iota(jnp.int32, (1, tk), 1)) <= t4
        absorb(j, jnp.where(allow, s, NEG_INF))
    acc = acc_ref[...]
    os4 = acc / _dot_lx2(acc, lsel)

    gates = _sigmoid(g_ref[0] + gb_ref[...])
    out = jnp.zeros((qb, D_BRANCH), F32)
    for h in range(N_HEADS):
        rows = slice(h * qb, (h + 1) * qb)
        base = 128 + h * N_NSA_BRANCH
        mixed = (gates[:, base:base + 1] * oc4[rows]
                 + gates[:, base + 1:base + 2] * os4[rows]
                 + gates[:, base + 2:base + 3] * ow_heads[h])
        out = out + _dot_lx2(mixed, _placement(h))
    o_ref[0] = out


def _nsa_mixer(proj, kt_all, v_aug, cmp_pos, cmp_w1, cmp_w2, gate_b):
    bsz, s, _ = proj.shape
    n_row = s // CMP_STRIDE
    n_sel = s // SEL_BLOCK
    n_top = min(N_SELECT, n_sel)
    qb = min(NSA_Q_BLOCK, s)
    base = 6 * D_BRANCH
    col = lambda k: proj[:, :, base + k * HEAD_DIM: base + (k + 1) * HEAD_DIM]
    rk = col(0).reshape(bsz, n_row, CMP_STRIDE * HEAD_DIM)
    rv = col(1).reshape(bsz, n_row, CMP_STRIDE * HEAD_DIM)
    pos_flat = cmp_pos.reshape(2, 1, CMP_BLOCK * HEAD_DIM)
    w2_pair = jnp.stack([jnp.concatenate([cmp_w2[0], cmp_w2[0]], axis=1),
                         jnp.concatenate([cmp_w2[1], jnp.zeros_like(cmp_w2[1])], axis=1)])
    kc_bf, vc_t = _nsa_compress(rk, rv, pos_flat, cmp_w1, w2_pair)

    key_blk = jnp.arange(s, dtype=jnp.int32) // SEL_BLOCK
    penalty = jnp.where(jnp.arange(n_sel, dtype=jnp.int32)[:, None] == key_blk[None, :],
                        -MASK_BIG, 0.0).astype(BF16)
    ratio = SEL_BLOCK // CMP_STRIDE
    off = jnp.arange(n_row, dtype=jnp.int32)[None, :] - ratio * jnp.arange(n_sel, dtype=jnp.int32)[:, None]
    m_t = jnp.where((off == -1) | (off == 3), 1.0, jnp.where((off >= 0) & (off <= 2), 2.0, 0.0)).astype(BF16)
    gate_b_pad = jnp.zeros((1, D_BRANCH), F32).at[0, 128:128 + N_NSA_BRANCH * N_HEADS].set(gate_b)
    full = lambda arr: pl.BlockSpec((1,) + arr.shape[1:], lambda b, i: (b, 0, 0))
    return pl.pallas_call(
        functools.partial(_nsa_kernel, n_top),
        grid=(bsz, s // qb),
        in_specs=[pl.BlockSpec((1, qb, D_BRANCH), lambda b, i: (b, i, 5)),
                  pl.BlockSpec((1, qb, D_BRANCH), lambda b, i: (b, i, 7)),
                  full(kc_bf), full(vc_t),
                  pl.BlockSpec((128, s), lambda b, i: (2, b)),
                  pl.BlockSpec(penalty.shape, lambda b, i: (0, 0)),
                  pl.BlockSpec((s, 128), lambda b, i: (b, N_HEADS)),
                  pl.BlockSpec((128, s), lambda b, i: (3, b)),
                  pl.BlockSpec((s, 128), lambda b, i: (b, N_HEADS + 1)),
                  pl.BlockSpec(m_t.shape, lambda b, i: (0, 0)),
                  pl.BlockSpec((1, D_BRANCH), lambda b, i: (0, 0))],
        out_specs=pl.BlockSpec((1, qb, D_BRANCH), lambda b, i: (b, i, 0)),
        out_shape=jax.ShapeDtypeStruct((bsz, s, D_BRANCH), F32),
        scratch_shapes=[pltpu.VMEM((N_HEADS * qb, 128), F32), pltpu.VMEM((N_HEADS * qb, 128), F32),
                        pltpu.VMEM((N_HEADS * qb, min(NSA_KV_TILE, s)), F32)],
        compiler_params=_params(("parallel", "arbitrary")),
        name="nsa_attention",
    )(proj, proj, kc_bf, vc_t, kt_all, penalty, v_aug, kt_all, v_aug, m_t, gate_b_pad)


def _rwkv_kernel(c, pr_ref, pk_ref, pv_ref, pw_ref, mu_ref, w0_ref, wuph_ref, wupl_ref, a0_ref, auph_ref, aupl_ref,
                 kk_ref, ka_ref, rk_ref, lng_ref, lnb_ref, o_ref, prev_ref, st_ref):
    nb, blk, _ = pr_ref.shape
    rows = nb * blk
    n = nb * c

    @pl.when(pl.program_id(0) == 0)
    def _():
        prev_ref[...] = jnp.zeros_like(prev_ref)
        st_ref[...] = jnp.zeros_like(st_ref)

    brow = lax.broadcasted_iota(jnp.int32, (rows, 1), 0)

    def shifted(ref, k):
        x = ref[...].reshape(rows, D_BRANCH)
        prev = pltpu.roll(x, 1, axis=0)
        for b in range(nb):
            prev = jnp.where(brow == b * blk, prev_ref[k, b:b + 1, :], prev)
        for b in range(nb):
            prev_ref[k, b:b + 1, :] = x[(b + 1) * blk - 1:(b + 1) * blk, :]
        return x + (prev - x) * mu_ref[k:k + 1, :]

    r_all = shifted(pr_ref, 0)
    k_all = shifted(pk_ref, 1)
    v_all = shifted(pv_ref, 2)
    xw = shifted(pw_ref, 3)

    ones_bd = _block_ones()
    wlog = w0_ref[...] + _dot3_pre(jnp.tanh(xw), wuph_ref[...], wupl_ref[...])
    ld_all = -jnp.exp(-_softplus(-wlog) - 0.5)
    a_lr = _sigmoid(a0_ref[...] + _dot3_pre(xw, auph_ref[...], aupl_ref[...]))
    kk = k_all * kk_ref[...]
    kk = kk / jnp.maximum(jnp.sqrt(_dot_lx(kk * kk, ones_bd)), 1e-12)
    k_all = k_all * (1.0 + (a_lr - 1.0) * ka_ref[...])
    a_all = -kk
    b_all = kk * a_lr

    row = lax.broadcasted_iota(jnp.int32, (n, 1), 0)
    col = lax.broadcasted_iota(jnp.int32, (1, n), 1)
    rowb = row // c
    colb = col // c
    same = rowb == colb
    tri = jnp.where(same & (col <= row), 1.0, 0.0).astype(BF16)
    allm = jnp.where(same, 1.0, 0.0).astype(BF16)
    upper = same & (row < col)
    upper_eq = same & (row <= col)
    eye = jnp.where(row == col, 1.0, 0.0)
    col2b = jnp.concatenate([colb, colb], axis=1)
    hm = _head_masks()
    bd = ones_bd.astype(F32)
    mean_m = _block_ones(1.0 / HEAD_DIM)

    def chunk_rows(x, j):
        return jnp.concatenate([x[b * blk + j * c:b * blk + (j + 1) * c] for b in range(nb)], axis=0)

    def prepare_stages(js):
        ps = [dict(j=j) for j in js]

        def slices():
            for p in ps:
                p["r"], p["k"], p["v"], p["ld"], p["a"], p["b"] = (
                    chunk_rows(x, p["j"]) for x in (r_all, k_all, v_all, ld_all, a_all, b_all))
                p["cs"] = _dot_xl(tri, p["ld"])
                p["tot"] = _dot_xl(allm, p["ld"])

        def pair_products():
            for p in ps:
                cs, tot = p["cs"], p["tot"]
                a_t = p["a"] * jnp.exp(cs - p["ld"])
                r_t = p["r"] * jnp.exp(cs)
                inv = jnp.exp(-cs)
                fin = jnp.exp(tot - cs)
                lhs = jnp.concatenate([(p["b"] * inv) * hm[h] for h in range(N_HEADS)]
                                      + [(p["k"] * inv) * hm[h] for h in range(N_HEADS)], axis=0)
                rhs = jnp.concatenate([a_t, r_t], axis=0)
                p["small"] = _dg(lhs.astype(BF16), rhs.astype(BF16), 1, 1)
                ar_t = jnp.concatenate([a_t.T, r_t.T], axis=1)
                p["v_t"] = p["v"].T.astype(BF16)
                p["ar_b"] = [jnp.where(col2b == bb, ar_t, 0.0).astype(BF16) for bb in range(nb)]
                p["bk_b"] = [jnp.concatenate([jnp.where(rowb == bb, p["b"] * fin, 0.0),
                                              jnp.where(rowb == bb, p["k"] * fin, 0.0)], axis=0).astype(BF16)
                             for bb in range(nb)]
                p["g_rows"] = [jnp.exp(tot[bb * c:bb * c + 1, :]) for bb in range(nb)]
                p["bonus"] = _dot_lx(p["r"] * p["k"] * rk_ref[...], ones_bd) * p["v"]

        def masks():
            for p in ps:
                small = p["small"]
                p["pw"], p["tinv"], p["brs"], p["zs"], p["y0s"] = [], [], [], [], []
                for h in range(N_HEADS):
                    ba = jnp.where(upper, small[h * n:(h + 1) * n, 0:n], 0.0)
                    br = jnp.where(upper_eq, small[h * n:(h + 1) * n, n:2 * n], 0.0).astype(BF16)
                    ka = jnp.where(upper, small[(N_HEADS + h) * n:(N_HEADS + h + 1) * n, 0:n], 0.0).astype(BF16)
                    kr = jnp.where(upper_eq, small[(N_HEADS + h) * n:(N_HEADS + h + 1) * n, n:2 * n],
                                   0.0).astype(BF16)
                    vh = p["v_t"][h * HEAD_DIM:(h + 1) * HEAD_DIM, :]
                    p["pw"].append(ba)
                    p["tinv"].append(eye + ba)
                    p["brs"].append(br)
                    p["zs"].append(_dg(vh, ka))
                    p["y0s"].append(_dg(vh, kr))

        def inverse_level():
            for p in ps:
                for h in range(N_HEADS):
                    pw_bf = p["pw"][h].astype(BF16)
                    p["pw"][h] = _dg(pw_bf, pw_bf)
            for p in ps:
                for h in range(N_HEADS):
                    p["tinv"][h] = p["tinv"][h] + _dg(p["tinv"][h].astype(BF16), p["pw"][h].astype(BF16))

        def finish():
            for p in ps:
                p["tinvs"] = [t.astype(BF16) for t in p["tinv"]]

        levels = max(int(math.log2(c)) - 1, 0)
        return ps, [slices, pair_products, masks] + [inverse_level] * levels + [finish]

    def advance_stages(p):
        w = {}

        def state_products():
            xy_t = jnp.zeros((D_BRANCH, 2 * n), F32)
            for bb in range(nb):
                xy_t = xy_t + _dg(st_ref[bb].astype(BF16), p["ar_b"][bb])
            w["xy_t"] = xy_t

        def solve():
            u_rows, y_rows = [], []
            for h in range(N_HEADS):
                hs = slice(h * HEAD_DIM, (h + 1) * HEAD_DIM)
                uh = _dg((w["xy_t"][hs, 0:n] + p["zs"][h]).astype(BF16), p["tinvs"][h]).astype(BF16)
                u_rows.append(uh)
                y_rows.append(w["xy_t"][hs, n:2 * n] + _dg(uh, p["brs"][h]) + p["y0s"][h])
            w["u_t"] = jnp.concatenate(u_rows, axis=0)
            w["y_rows"] = y_rows

        def update_state():
            uv = jnp.concatenate([w["u_t"], p["v_t"]], axis=1)
            for bb in range(nb):
                st_ref[bb] = st_ref[bb] * p["g_rows"][bb] + bd * _dg(uv, p["bk_b"][bb])

        def epilogue():
            y = jnp.concatenate(w["y_rows"], axis=0).T
            mean = _dot_lx(y, mean_m)
            yc = y - mean
            var = _dot_lx(yc * yc, mean_m)
            out = yc * lax.rsqrt(var + RWKV_GN_EPS) * lng_ref[...] + lnb_ref[...]
            j = p["j"]
            o_ref[:, j * c:(j + 1) * c, :] = (out + p["bonus"]).reshape(nb, c, D_BRANCH)

        return [state_products, solve, update_state, epilogue]

    nchunk = blk // c
    groups = [list(range(g, min(g + RWKV_GROUP, nchunk))) for g in range(0, nchunk, RWKV_GROUP)]
    ps_prev, stages = prepare_stages(groups[0])
    for stage in stages:
        stage()
    for grp in groups[1:]:
        ps_next, prep = prepare_stages(grp)
        adv = [stage for p in ps_prev for stage in advance_stages(p)]
        for idx in range(max(len(prep), len(adv))):
            if idx < len(prep):
                prep[idx]()
            if idx < len(adv):
                adv[idx]()
        ps_prev = ps_next
    for p in ps_prev:
        for stage in advance_stages(p):
            stage()


def _rwkv_mixer(proj, mu, w0, w_up, a0, a_up, k_k, k_a, r_k, ln_g, ln_b):
    bsz, s, _ = proj.shape
    c = min(RWKV_CHUNK, s)
    tb = min(RWKV_BLOCK, s)
    mu_p = jnp.zeros((4, D_BRANCH), F32)
    mu_p = mu_p.at[0:3, :].set(mu[:3 * D_BRANCH].reshape(3, D_BRANCH))
    mu_p = mu_p.at[3, :2 * RWKV_RANK].set(mu[3 * D_BRANCH:])
    wup_h, wup_l = _split2(jnp.zeros((D_BRANCH, D_BRANCH), F32).at[:RWKV_RANK, :].set(w_up))
    aup_h, aup_l = _split2(jnp.zeros((D_BRANCH, D_BRANCH), F32).at[RWKV_RANK:2 * RWKV_RANK, :].set(a_up))
    row = lambda a: a.reshape(1, D_BRANCH)
    blk = lambda cidx: pl.BlockSpec((bsz, tb, D_BRANCH), lambda i, cidx=cidx: (0, i, cidx))
    rowspec = pl.BlockSpec((1, D_BRANCH), lambda i: (0, 0))
    matspec = pl.BlockSpec((D_BRANCH, D_BRANCH), lambda i: (0, 0))
    return pl.pallas_call(
        functools.partial(_rwkv_kernel, c),
        grid=(s // tb,),
        in_specs=[blk(8), blk(9), blk(10), blk(11),
                  pl.BlockSpec((4, D_BRANCH), lambda i: (0, 0)),
                  rowspec, matspec, matspec, rowspec, matspec, matspec,
                  rowspec, rowspec, rowspec, rowspec, rowspec],
        out_specs=pl.BlockSpec((bsz, tb, D_BRANCH), lambda i: (0, i, 0)),
        out_shape=jax.ShapeDtypeStruct((bsz, s, D_BRANCH), F32),
        scratch_shapes=[pltpu.VMEM((4, 8, D_BRANCH), F32), pltpu.VMEM((bsz, D_BRANCH, D_BRANCH), F32)],
        compiler_params=_params(("arbitrary",)),
        name="rwkv7",
    )(proj, proj, proj, proj, mu_p, row(w0), wup_h, wup_l, row(a0), aup_h, aup_l, row(k_k), row(k_a),
      row(r_k), row(ln_g), row(ln_b))


def _lane_sum_selector():
    r = lax.broadcasted_iota(jnp.int32, (128, 128), 0)
    return jnp.where(r == HEAD_DIM, 1.0, 0.0).astype(BF16)


def _placement(h):
    r = lax.broadcasted_iota(jnp.int32, (128, D_BRANCH), 0)
    c = lax.broadcasted_iota(jnp.int32, (128, D_BRANCH), 1)
    return jnp.where((r < HEAD_DIM) & (c == r + h * HEAD_DIM), 1.0, 0.0).astype(BF16)


def _diff_kernel(lambda_init, q_ref, k_ref, v_ref, lam_ref, g_ref, o_ref, acc_ref, m_ref, s_ref):
    qb = q_ref.shape[1]
    tk = min(DIFF_KV_TILE, k_ref.shape[1])
    rep = tk // 128
    q0 = pl.program_id(1) * qb
    lane = lax.broadcasted_iota(jnp.int32, (1, 128), 1)
    q = q_ref[0] * (DIFF_QK_DIM ** -0.5 * LOG2E)
    qs = []
    for h in range(N_HEADS):
        pair, half = divmod(h, 2)
        qp = q[:, pair * 128:(pair + 1) * 128]
        both = []
        for cc in range(2):
            lo = half * HEAD_DIM + cc * DIFF_QK_DIM
            both.append(jnp.where((lane >= lo) & (lane < lo + DIFF_QK_DIM), qp, 0.0).astype(BF16))
        qs.append(jnp.concatenate(both, axis=0))
    t = q0 + lax.broadcasted_iota(jnp.int32, (qb, 1), 0)
    t2 = jnp.concatenate([t, t], axis=0)
    m_ref[...] = jnp.full(m_ref.shape, NEG_INF, F32)
    acc_ref[...] = jnp.zeros(acc_ref.shape, F32)

    def scores(j, h):
        start = pl.multiple_of(j * tk, tk)
        return _dg(qs[h], k_ref[(h // 2) * 128:(h // 2 + 1) * 128, pl.ds(start, tk)])

    def absorb(j, h, s):
        start = pl.multiple_of(j * tk, tk)
        m_old = m_ref[h]
        m_new = jnp.maximum(m_old, jnp.max(s, axis=-1, keepdims=True))
        p = jnp.exp2(s - jnp.concatenate([m_new] * rep, axis=1))
        acc_ref[h] = jnp.exp2(m_old - m_new) * acc_ref[h] + _dg(p.astype(BF16), v_ref[pl.ds(start, tk), h * 128:(h + 1) * 128])
        m_ref[h] = m_new

    n_full = q0 // tk
    n_diag = max(qb // tk, 1)
    for h in range(N_HEADS):
        s_ref[h] = scores(0, h)

    def body(j, carry):
        for h in range(N_HEADS):
            s = s_ref[h]
            s_ref[h] = scores(j + 1, h)
            absorb(j, h, s)
        return carry

    lax.fori_loop(0, n_full, body, 0)
    for u in range(n_diag):
        j = n_full + u
        allow = (pl.multiple_of(j * tk, tk) + lax.broadcasted_iota(jnp.int32, (1, tk), 1)) <= t2
        for h in range(N_HEADS):
            s = s_ref[h]
            if u + 1 < n_diag:
                s_ref[h] = scores(j + 1, h)
            absorb(j, h, jnp.where(allow, s, NEG_INF))

    lam = lam_ref[...]
    lam_full = (jnp.exp(jnp.sum(lam[0:1] * lam[1:2], axis=-1, keepdims=True))
                - jnp.exp(jnp.sum(lam[2:3] * lam[3:4], axis=-1, keepdims=True)) + lambda_init)
    lsel = _lane_sum_selector()
    out = jnp.zeros((qb, D_BRANCH), F32)
    for h in range(N_HEADS):
        a = acc_ref[h]
        a = a / _dot_lx2(a, lsel)
        d = a[0:qb] - lam_full * a[qb:2 * qb]
        out = out + _dot_lx2(d, _placement(h))
    ms = _dot_lx(out * out, _block_ones(1.0 / HEAD_DIM))
    o_ref[0] = out * lax.rsqrt(ms + DIFF_EPS) * g_ref[...] * (1.0 - lambda_init)


def _diff_mixer(proj, kt_all, v_aug, lam, subln_g, lambda_init):
    bsz, s, _ = proj.shape
    qb = min(DIFF_Q_BLOCK, s)
    g_rep = jnp.tile(subln_g, N_HEADS).reshape(1, D_BRANCH)
    return pl.pallas_call(
        functools.partial(_diff_kernel, lambda_init),
        grid=(bsz, s // qb),
        in_specs=[pl.BlockSpec((1, qb, D_BRANCH), lambda b, i: (b, i, 12)),
                  pl.BlockSpec((D_BRANCH, s), lambda b, i: (0, b)),
                  pl.BlockSpec((s, N_HEADS * 128), lambda b, i: (b, 0)),
                  pl.BlockSpec(lam.shape, lambda b, i: (0, 0)),
                  pl.BlockSpec((1, D_BRANCH), lambda b, i: (0, 0))],
        out_specs=pl.BlockSpec((1, qb, D_BRANCH), lambda b, i: (b, i, 0)),
        out_shape=jax.ShapeDtypeStruct((bsz, s, D_BRANCH), F32),
        scratch_shapes=[pltpu.VMEM((N_HEADS, 2 * qb, 128), F32),
                        pltpu.VMEM((N_HEADS, 2 * qb, 128), F32),
                        pltpu.VMEM((N_HEADS, 2 * qb, min(DIFF_KV_TILE, s)), F32)],
        compiler_params=_params(("parallel", "arbitrary")),
        name="diff_attention",
    )(proj, kt_all, v_aug, lam, g_rep)


def _block_diag(w):
    n = w.shape[0] * w.shape[1]
    out = jnp.zeros((n, n), F32)
    for h in range(w.shape[0]):
        out = out.at[h * HEAD_DIM:(h + 1) * HEAD_DIM, h * HEAD_DIM:(h + 1) * HEAD_DIM].set(w[h])
    return out


def _layout_w_in(w):
    d = w.shape[0]
    nsa_end = 5 * D_BRANCH + D_BRANCH + 6 * HEAD_DIM + N_NSA_BRANCH * N_HEADS
    rwkv_end = nsa_end + 3 * D_BRANCH + 2 * RWKV_RANK
    pad1 = 8 * D_BRANCH - nsa_end
    pad2 = 12 * D_BRANCH - (rwkv_end + pad1)
    return jnp.concatenate([w[:, :nsa_end], jnp.zeros((d, pad1), w.dtype), w[:, nsa_end:rwkv_end],
                            jnp.zeros((d, pad2), w.dtype), w[:, rwkv_end:rwkv_end + D_BRANCH]], axis=1)


def _layout_wt_in(w):
    nsa0 = 5 * D_BRANCH
    ks0 = nsa0 + D_BRANCH + 2 * HEAD_DIM
    kw0 = ks0 + 2 * HEAD_DIM
    dk0 = w.shape[1] - 2 * D_BRANCH
    ks, kw = w[:, ks0:ks0 + HEAD_DIM], w[:, kw0:kw0 + HEAD_DIM]
    return jnp.concatenate([w[:, dk0:dk0 + D_BRANCH], ks, ks, kw, kw], axis=1).T


def _layout_wv_in(w):
    nsa0 = 5 * D_BRANCH
    vs0 = nsa0 + D_BRANCH + 3 * HEAD_DIM
    vw0 = vs0 + 2 * HEAD_DIM
    dv0 = w.shape[1] - D_BRANCH
    pad = jnp.zeros((w.shape[0], 128 - HEAD_DIM), w.dtype)
    cols = [w[:, dv0 + h * HEAD_DIM:dv0 + (h + 1) * HEAD_DIM] for h in range(N_HEADS)]
    cols += [w[:, vs0:vs0 + HEAD_DIM], w[:, vw0:vw0 + HEAD_DIM]]
    return jnp.concatenate([piece for c in cols for piece in (c, pad)], axis=1)


def kernel(x, norm_gain, w_in, w_out, final_gain, lru_conv_w, lru_conv_b, lru_wa, lru_ba, lru_wx, lru_bx, lru_lambda, nsa_cmp_pos, nsa_cmp_w1, nsa_cmp_w2, nsa_gate_b, rwkv_mu, rwkv_w0, rwkv_w_up, rwkv_a0, rwkv_a_up, rwkv_k_k, rwkv_k_a, rwkv_r_k, rwkv_ln_g, rwkv_ln_b, diff_lambda, diff_subln_g):
    bsz, s, d = x.shape
    depth = w_in.shape[0]
    t = bsz * s
    x2d = x.reshape(t, d)
    row = lambda a: a.reshape(1, -1)
    w_in_bf = w_in.astype(BF16)
    for l in range(depth):
        w_bf = w_in_bf[l]
        proj2d, kt_all, v_aug = _in_proj(x2d, row(norm_gain[l]), _layout_w_in(w_bf), _layout_wt_in(w_bf),
                                         _layout_wv_in(w_bf))
        proj = proj2d.reshape(bsz, s, N_PROJ_BLOCKS * D_BRANCH)
        o_lru = _lru_mixer(proj, lru_conv_w[l], row(lru_conv_b[l]), _block_diag(lru_wa[l]), row(lru_ba[l]),
                           _block_diag(lru_wx[l]), row(lru_bx[l]), row(lru_lambda[l]))
        o_nsa = _nsa_mixer(proj, kt_all, v_aug, nsa_cmp_pos[l], nsa_cmp_w1[l], nsa_cmp_w2[l], nsa_gate_b[l])
        o_rwkv = _rwkv_mixer(proj, rwkv_mu[l], rwkv_w0[l], rwkv_w_up[l], rwkv_a0[l], rwkv_a_up[l],
                             rwkv_k_k[l], rwkv_k_a[l], rwkv_r_k[l].reshape(-1), rwkv_ln_g[l], rwkv_ln_b[l])
        lambda_init = 0.8 - 0.6 * math.exp(-0.3 * l)
        o_diff = _diff_mixer(proj, kt_all, v_aug, diff_lambda[l], diff_subln_g[l], lambda_init)
        mixers = [o.reshape(t, D_BRANCH) for o in (o_lru, o_nsa, o_rwkv, o_diff)]
        x2d = _out_proj(mixers, proj2d, x2d, w_out[l].astype(BF16), row(final_gain), final=(l == depth - 1))
    return x2d.reshape(bsz, s, d)
```

```python
import functools
import math

import jax
import jax.numpy as jnp
from jax import lax
from jax.experimental import pallas as pl
from jax.experimental.pallas import tpu as pltpu

F32 = jnp.float32
BF16 = jnp.bfloat16

N_MIX = 4
HEAD_DIM = 64
N_HEADS = 4
D_BRANCH = N_HEADS * HEAD_DIM
NORM_EPS = 1e-6
NEG_INF = -1e30
LRU_CONV = 4
LRU_C = 8.0
CMP_BLOCK = 32
CMP_STRIDE = 16
SEL_BLOCK = 64
N_SELECT = 16
WINDOW = 512
N_NSA_BRANCH = 3
RWKV_RANK = 32
RWKV_GN_EPS = 64e-5
DIFF_QK_DIM = HEAD_DIM // 2
DIFF_EPS = 1e-5
N_PROJ_BLOCKS = 13
Q_BLOCK = 128
KV_TILE = 512
RWKV_CHUNK = 64
RWKV_BLOCK = 512
RWKV_GROUP = 4
VMEM_LIMIT = 56 * 1024 * 1024
LOG2E = 1.4426950408889634
DIFF_Q_BLOCK = 256
DIFF_KV_TILE = 512
NSA_Q_BLOCK = 512
NSA_KV_TILE = 512
MASK_BIG = 1e30


def _dg(a, b, ca=1, cb=0):
    return lax.dot_general(a, b, (((ca,), (cb,)), ((), ())), preferred_element_type=F32)


def _split2(a):
    hi = a.astype(BF16)
    lo = (a - hi.astype(F32)).astype(BF16)
    return hi, lo


def _dot3(a, b, ca=1, cb=0):
    ah, al = _split2(a)
    bh, bl = _split2(b)
    return _dg(ah, bh, ca, cb) + (_dg(ah, bl, ca, cb) + _dg(al, bh, ca, cb))


def _dot3_pre(a, bh, bl, ca=1, cb=0):
    ah, al = _split2(a)
    return _dg(ah, bh, ca, cb) + (_dg(ah, bl, ca, cb) + _dg(al, bh, ca, cb))


def _dot_lx(a, b_exact, ca=1, cb=0):
    a1 = a.astype(BF16)
    r = a - a1.astype(F32)
    a2 = r.astype(BF16)
    a3 = (r - a2.astype(F32)).astype(BF16)
    return _dg(a1, b_exact, ca, cb) + (_dg(a2, b_exact, ca, cb) + _dg(a3, b_exact, ca, cb))


def _dot_lx2(a, b_exact, ca=1, cb=0):
    a1, a2 = _split2(a)
    return _dg(a1, b_exact, ca, cb) + _dg(a2, b_exact, ca, cb)


def _dot_xl(a_exact, b, ca=1, cb=0):
    b1 = b.astype(BF16)
    r = b - b1.astype(F32)
    b2 = r.astype(BF16)
    b3 = (r - b2.astype(F32)).astype(BF16)
    return _dg(a_exact, b1, ca, cb) + (_dg(a_exact, b2, ca, cb) + _dg(a_exact, b3, ca, cb))


def _head_masks(width=D_BRANCH, group=HEAD_DIM):
    lane = lax.broadcasted_iota(jnp.int32, (1, width), 1)
    return [((lane >= h * group) & (lane < (h + 1) * group)).astype(F32) for h in range(width // group)]


def _block_ones(scale=1.0):
    r = lax.broadcasted_iota(jnp.int32, (D_BRANCH, D_BRANCH), 0) // HEAD_DIM
    c = lax.broadcasted_iota(jnp.int32, (D_BRANCH, D_BRANCH), 1) // HEAD_DIM
    return jnp.where(r == c, scale, 0.0).astype(BF16)


def _sigmoid(x):
    return 1.0 / (1.0 + jnp.exp(-x))


def _softplus(x):
    return jnp.maximum(x, 0.0) + jnp.log1p(jnp.exp(-jnp.abs(x)))


def _params(sem):
    return pltpu.CompilerParams(dimension_semantics=sem, vmem_limit_bytes=VMEM_LIMIT)


def _proj_kernel(x_ref, g_ref, w_ref, wt_ref, wv_ref, o_ref, ot_ref, ov_ref):
    x = x_ref[...]
    ms = jnp.mean(x * x, axis=-1, keepdims=True)
    h = (x * lax.rsqrt(ms + NORM_EPS) * g_ref[...]).astype(BF16)
    o_ref[...] = _dg(h, w_ref[...])

    @pl.when(pl.program_id(1) == 0)
    def _():
        ot_ref[...] = _dg(wt_ref[...], h, 1, 1).astype(BF16)
        lane = lax.broadcasted_iota(jnp.int32, (1, wv_ref.shape[1]), 1)
        ov_ref[...] = jnp.where(lane % 128 == HEAD_DIM, 1.0, _dg(h, wv_ref[...])).astype(BF16)


def _in_proj(x2d, gain, w_bf, wt_bf, wv_bf, tm=1024, tn=N_PROJ_BLOCKS * 128):
    t, d = x2d.shape
    n = w_bf.shape[1]
    nt = wt_bf.shape[0]
    nv = wv_bf.shape[1]
    tm = min(tm, t)
    return pl.pallas_call(
        _proj_kernel,
        grid=(t // tm, n // tn),
        in_specs=[pl.BlockSpec((tm, d), lambda i, j: (i, 0)),
                  pl.BlockSpec((1, d), lambda i, j: (0, 0)),
                  pl.BlockSpec((d, tn), lambda i, j: (0, j)),
                  pl.BlockSpec((nt, d), lambda i, j: (0, 0)),
                  pl.BlockSpec((d, nv), lambda i, j: (0, 0))],
        out_specs=[pl.BlockSpec((tm, tn), lambda i, j: (i, j)),
                   pl.BlockSpec((nt, tm), lambda i, j: (0, i)),
                   pl.BlockSpec((tm, nv), lambda i, j: (i, 0))],
        out_shape=[jax.ShapeDtypeStruct((t, n), F32), jax.ShapeDtypeStruct((nt, t), BF16),
                   jax.ShapeDtypeStruct((t, nv), BF16)],
        compiler_params=_params(("parallel", "arbitrary")),
        name="in_proj",
    )(x2d, gain, w_bf, wt_bf, wv_bf)


def _out_kernel(final, o1_ref, o2_ref, o3_ref, o4_ref, gate_ref, x_ref, w_ref, fg_ref, y_ref):
    mix = jnp.concatenate([o1_ref[...], o2_ref[...], o3_ref[...], o4_ref[...]], axis=-1)
    g = gate_ref[...]
    z = mix * (g * _sigmoid(g))
    y = x_ref[...] + _dg(z.astype(BF16), w_ref[...])
    if final:
        ms = jnp.mean(y * y, axis=-1, keepdims=True)
        y = y * lax.rsqrt(ms + NORM_EPS) * fg_ref[...]
    y_ref[...] = y


def _out_proj(mixers, proj2d, x2d, w_bf, final_gain, final, tm=512):
    t, d = x2d.shape
    dm = w_bf.shape[0]
    tm = min(tm, t)
    branch = pl.BlockSpec((tm, D_BRANCH), lambda i: (i, 0))
    return pl.pallas_call(
        functools.partial(_out_kernel, final),
        grid=(t // tm,),
        in_specs=[branch, branch, branch, branch,
                  pl.BlockSpec((tm, dm), lambda i: (i, 0)),
                  pl.BlockSpec((tm, d), lambda i: (i, 0)),
                  pl.BlockSpec((dm, d), lambda i: (0, 0)),
                  pl.BlockSpec((1, d), lambda i: (0, 0))],
        out_specs=pl.BlockSpec((tm, d), lambda i: (i, 0)),
        out_shape=jax.ShapeDtypeStruct((t, d), F32),
        compiler_params=_params(("parallel",)),
        name="out_proj",
    )(*mixers, proj2d, x2d, w_bf, final_gain)


def _lru_kernel(u_ref, cw_ref, cb_ref, wah_ref, wal_ref, ba_ref, wxh_ref, wxl_ref, bx_ref, lam_ref,
                o_ref, ubuf, hcar):
    ts = u_ref.shape[1]

    @pl.when(pl.program_id(1) == 0)
    def _():
        ubuf[0:8, :] = jnp.zeros((8, D_BRANCH), F32)
        hcar[...] = jnp.zeros_like(hcar)

    u = u_ref[0]
    ubuf[8:8 + ts, :] = u
    xc = cb_ref[...] + cw_ref[0:1, :] * ubuf[5:5 + ts, :]
    for j in range(1, LRU_CONV):
        xc = xc + cw_ref[j:j + 1, :] * ubuf[5 + j:5 + j + ts, :]
    ubuf[0:8, :] = u[ts - 8:ts, :]

    r = _sigmoid(_dot3_pre(xc, wah_ref[...], wal_ref[...]) + ba_ref[...])
    gi = _sigmoid(_dot3_pre(xc, wxh_ref[...], wxl_ref[...]) + bx_ref[...])
    log_a = (-LRU_C) * r * _softplus(-lam_ref[...])
    a = jnp.exp(log_a)
    b = jnp.sqrt(jnp.tanh(-log_a) * (a * a + 1.0)) * (gi * xc)

    row = lax.broadcasted_iota(jnp.int32, (ts, 1), 0)
    d = 1
    while d < ts:
        keep = row >= d
        a_s = jnp.where(keep, pltpu.roll(a, d, axis=0), 1.0)
        b_s = jnp.where(keep, pltpu.roll(b, d, axis=0), 0.0)
        b = a * b_s + b
        a = a * a_s
        d *= 2
    h = b + a * hcar[...]
    o_ref[0] = h
    hcar[...] = h[ts - 1:ts, :]


def _lru_mixer(proj, conv_w, conv_b, wa_bd, ba, wx_bd, bx, lam, ts=512):
    bsz, s, _ = proj.shape
    wah, wal = _split2(wa_bd)
    wxh, wxl = _split2(wx_bd)
    row = pl.BlockSpec((1, D_BRANCH), lambda b, j: (0, 0))
    mat = pl.BlockSpec((D_BRANCH, D_BRANCH), lambda b, j: (0, 0))
    return pl.pallas_call(
        _lru_kernel,
        grid=(bsz, s // ts),
        in_specs=[pl.BlockSpec((1, ts, D_BRANCH), lambda b, j: (b, j, 4)),
                  pl.BlockSpec((LRU_CONV, D_BRANCH), lambda b, j: (0, 0)),
                  row, mat, mat, row, mat, mat, row, row],
        out_specs=pl.BlockSpec((1, ts, D_BRANCH), lambda b, j: (b, j, 0)),
        out_shape=jax.ShapeDtypeStruct((bsz, s, D_BRANCH), F32),
        scratch_shapes=[pltpu.VMEM((ts + 8, D_BRANCH), F32), pltpu.VMEM((1, D_BRANCH), F32)],
        compiler_params=_params(("parallel", "arbitrary")),
        name="rg_lru",
    )(proj, conv_w, conv_b, wah, wal, ba, wxh, wxl, bx, lam)


def _cmp_kernel(rk_ref, rv_ref, pos_ref, w1_ref, w2_ref, kc_ref, vct_ref):
    half = (CMP_BLOCK // 2) * HEAD_DIM
    nrow = rk_ref.shape[1]
    outs = []
    for idx, r_ref in enumerate((rk_ref, rv_ref)):
        r = r_ref[0]
        w1 = w1_ref[idx]
        lo_half = _dot3(r, w1[0:half, :])
        hi_half = _dot3(r, w1[half:2 * half, :])
        bias = _dot3(jnp.broadcast_to(pos_ref[idx], (8, 2 * half)), w1)[0:1, :]
        hid = lo_half + pltpu.roll(hi_half, nrow - 1, axis=0) + bias
        outs.append(_dot3(jax.nn.gelu(hid), w2_ref[idx]))
    kc_ref[0] = outs[0].astype(BF16)
    vct_ref[0] = outs[1].T.astype(BF16)


def _nsa_compress(rk, rv, pos_flat, w1, w2_pair):
    bsz, nrow, width = rk.shape
    blk = pl.BlockSpec((1, nrow, width), lambda b: (b, 0, 0))
    out = pl.BlockSpec((1, nrow, 128), lambda b: (b, 0, 0))
    return pl.pallas_call(
        _cmp_kernel,
        grid=(bsz,),
        in_specs=[blk, blk,
                  pl.BlockSpec(pos_flat.shape, lambda b: (0, 0, 0)),
                  pl.BlockSpec(w1.shape, lambda b: (0, 0, 0)),
                  pl.BlockSpec(w2_pair.shape, lambda b: (0, 0, 0))],
        out_specs=[out, pl.BlockSpec((1, 128, nrow), lambda b: (b, 0, 0))],
        out_shape=[jax.ShapeDtypeStruct((bsz, nrow, 128), BF16), jax.ShapeDtypeStruct((bsz, 128, nrow), BF16)],
        compiler_params=_params(("parallel",)),
        name="nsa_compress",
    )(rk, rv, pos_flat, w1, w2_pair)


def _nsa_kernel(n_top, q_ref, g_ref, kc_ref, vct_ref, ks_ref, pen_ref, vs_ref, kw_ref, vw_ref,
                mt_ref, gb_ref, o_ref, acc_ref, m_ref, s_ref):
    qb = q_ref.shape[1]
    n_cmp = kc_ref.shape[1]
    n_sel = mt_ref.shape[0]
    tk = min(NSA_KV_TILE, vs_ref.shape[0])
    rep = tk // 128
    q0 = pl.program_id(1) * qb
    lane = lax.broadcasted_iota(jnp.int32, (1, 128), 1)
    q = q_ref[0] * (HEAD_DIM ** -0.5)
    parts = []
    for h in range(N_HEADS):
        pair, half = divmod(h, 2)
        keep = (lane >= half * HEAD_DIM) & (lane < (half + 1) * HEAD_DIM)
        parts.append(jnp.where(keep, q[:, pair * 128:(pair + 1) * 128], 0.0))
    q4 = jnp.concatenate(parts, axis=0)
    q4_log2 = (q4 * LOG2E).astype(BF16)
    t = q0 + lax.broadcasted_iota(jnp.int32, (qb, 1), 0)
    t4 = jnp.concatenate([t] * N_HEADS, axis=0)
    t_row = q0 + lax.broadcasted_iota(jnp.int32, (1, qb), 1)

    cend = lax.broadcasted_iota(jnp.int32, (n_cmp, 1), 0) * CMP_STRIDE + (CMP_BLOCK - 1)
    valid = cend <= t_row
    anyv = (t_row >= CMP_BLOCK - 1).astype(F32)
    q4_bf = q4.astype(BF16)
    oc_heads = []
    imp_t = jnp.zeros((n_cmp, qb), F32)
    for h in range(N_HEADS):
        sct = jnp.where(valid, _dg(kc_ref[0], q4_bf[h * qb:(h + 1) * qb], 1, 1), NEG_INF)
        e = jnp.exp(sct - jnp.max(sct, axis=0, keepdims=True))
        p_ct = e / jnp.sum(e, axis=0, keepdims=True)
        oc_heads.append((_dg(vct_ref[0], p_ct.astype(BF16)) * anyv).T)
        imp_t = imp_t + jnp.where(valid, p_ct, 0.0)
    pslc_t = _dot_xl(mt_ref[...], imp_t)

    blk = lax.broadcasted_iota(jnp.int32, (n_sel, 1), 0)
    blk_f = blk.astype(F32)
    cur = t_row // SEL_BLOCK
    forced = (blk == 0) | (blk == cur) | (blk == cur - 1)
    n_forced = 3
    score = jnp.where(blk > cur, -1.0, jnp.where(forced, -3e38, pslc_t))
    sub = min(qb, 256)
    nw = sub + WINDOW
    lsel = _lane_sum_selector()
    slabs = []
    for a in range(qb // sub):
        start_a = pl.multiple_of(jnp.maximum(q0 + a * sub - WINDOW, 0), 128)
        wpos = start_a + lax.broadcasted_iota(jnp.int32, (1, nw), 1)
        t_a = t[a * sub:(a + 1) * sub]
        slabs.append((start_a, (wpos <= t_a) & (wpos > t_a - WINDOW)))

    def window_head(h):
        parts = []
        for a, (start_a, allow_a) in enumerate(slabs):
            rows = slice(h * qb + a * sub, h * qb + (a + 1) * sub)
            s_w = jnp.where(allow_a, _dg(q4_log2[rows], kw_ref[:, pl.ds(start_a, nw)]), NEG_INF)
            p_w = jnp.exp2(s_w - jnp.max(s_w, axis=-1, keepdims=True))
            parts.append(_dg(p_w.astype(BF16), vw_ref[pl.ds(start_a, nw), :]))
        aw = jnp.concatenate(parts, axis=0)
        return aw / _dot_lx2(aw, lsel)

    ow_heads = []
    n_rounds = max(n_top - n_forced, 0)
    every = max(n_rounds // N_HEADS, 1)
    sel_t = jnp.where(forced, 1.0, 0.0)
    for rnd in range(n_rounds):
        best = jnp.max(score, axis=0, keepdims=True)
        idx = jnp.min(jnp.where(score == best, blk_f, float(n_sel)), axis=0, keepdims=True)
        pick = blk_f == idx
        sel_t = jnp.where(pick, 1.0, sel_t)
        score = jnp.where(pick, -3e38, score)
        if rnd % every == every - 1 and len(ow_heads) < N_HEADS:
            ow_heads.append(window_head(len(ow_heads)))
    while len(ow_heads) < N_HEADS:
        ow_heads.append(window_head(len(ow_heads)))
    notsel = (1.0 - sel_t).T.astype(BF16)
    lhs = jnp.concatenate([q4_log2, jnp.concatenate([notsel] * N_HEADS, axis=0)], axis=1)

    m_ref[...] = jnp.full(m_ref.shape, NEG_INF, F32)
    acc_ref[...] = jnp.zeros(acc_ref.shape, F32)

    def scores(j):
        start = pl.multiple_of(j * tk, tk)
        rhs = jnp.concatenate([ks_ref[:, pl.ds(start, tk)], pen_ref[:, pl.ds(start, tk)]], axis=0)
        return _dg(lhs, rhs)

    def absorb(j, slot, masked=False):
        start = pl.multiple_of(j * tk, tk)
        s = s_ref[slot]
        if masked:
            s = jnp.where((start + lax.broadcasted_iota(jnp.int32, (1, tk), 1)) <= t4, s, NEG_INF)
        m_old = m_ref[...]
        m_new = jnp.maximum(m_old, jnp.max(s, axis=-1, keepdims=True))
        p = jnp.exp2(s - jnp.concatenate([m_new] * rep, axis=1))
        acc_ref[...] = jnp.exp2(m_old - m_new) * acc_ref[...] + _dg(p.astype(BF16), vs_ref[pl.ds(start, tk), :])
        m_ref[...] = m_new

    assert qb <= tk
    n_full = q0 // tk
    s_ref[0] = scores(0)

    def body(pair, carry):
        j = 2 * pair
        s_ref[1] = scores(j + 1)
        absorb(j, 0)
        s_ref[0] = scores(j + 2)
        absorb(j + 1, 1)
        return carry

    lax.fori_loop(0, n_full // 2, body, 0)

    @pl.when(n_full % 2 == 1)
    def _():
        s_ref[1] = scores(n_full)
        absorb(n_full - 1, 0)
        absorb(n_full, 1, masked=True)

    @pl.when(n_full % 2 == 0)
    def _():
        absorb(n_full, 0, masked=True)
    acc = acc_ref[...]
    os4 = acc / _dot_lx2(acc, lsel)

    gates = _sigmoid(g_ref[0] + gb_ref[...])
    out = jnp.zeros((qb, D_BRANCH), F32)
    for h in range(N_HEADS):
        rows = slice(h * qb, (h + 1) * qb)
        base = 128 + h * N_NSA_BRANCH
        mixed = (gates[:, base:base + 1] * oc_heads[h]
                 + gates[:, base + 1:base + 2] * os4[rows]
                 + gates[:, base + 2:base + 3] * ow_heads[h])
        out = out + _dot_lx2(mixed, _placement(h))
    o_ref[0] = out


def _nsa_mixer(proj, kt_all, v_aug, cmp_pos, cmp_w1, cmp_w2, gate_b):
    bsz, s, _ = proj.shape
    n_row = s // CMP_STRIDE
    n_sel = s // SEL_BLOCK
    n_top = min(N_SELECT, n_sel)
    qb = min(NSA_Q_BLOCK, s)
    base = 6 * D_BRANCH
    col = lambda k: proj[:, :, base + k * HEAD_DIM: base + (k + 1) * HEAD_DIM]
    rk = col(0).reshape(bsz, n_row, CMP_STRIDE * HEAD_DIM)
    rv = col(1).reshape(bsz, n_row, CMP_STRIDE * HEAD_DIM)
    pos_flat = cmp_pos.reshape(2, 1, CMP_BLOCK * HEAD_DIM)
    w2_pair = jnp.stack([jnp.concatenate([cmp_w2[0], cmp_w2[0]], axis=1),
                         jnp.concatenate([cmp_w2[1], jnp.zeros_like(cmp_w2[1])], axis=1)])
    kc_bf, vc_t = _nsa_compress(rk, rv, pos_flat, cmp_w1, w2_pair)

    key_blk = jnp.arange(s, dtype=jnp.int32) // SEL_BLOCK
    penalty = jnp.where(jnp.arange(n_sel, dtype=jnp.int32)[:, None] == key_blk[None, :],
                        -MASK_BIG, 0.0).astype(BF16)
    ratio = SEL_BLOCK // CMP_STRIDE
    off = jnp.arange(n_row, dtype=jnp.int32)[None, :] - ratio * jnp.arange(n_sel, dtype=jnp.int32)[:, None]
    m_t = jnp.where((off == -1) | (off == 3), 1.0, jnp.where((off >= 0) & (off <= 2), 2.0, 0.0)).astype(BF16)
    gate_b_pad = jnp.zeros((1, D_BRANCH), F32).at[0, 128:128 + N_NSA_BRANCH * N_HEADS].set(gate_b)
    full = lambda arr: pl.BlockSpec((1,) + arr.shape[1:], lambda b, i: (b, 0, 0))
    return pl.pallas_call(
        functools.partial(_nsa_kernel, n_top),
        grid=(bsz, s // qb),
        in_specs=[pl.BlockSpec((1, qb, D_BRANCH), lambda b, i: (b, i, 5)),
                  pl.BlockSpec((1, qb, D_BRANCH), lambda b, i: (b, i, 7)),
                  full(kc_bf), full(vc_t),
                  pl.BlockSpec((128, s), lambda b, i: (2, b)),
                  pl.BlockSpec(penalty.shape, lambda b, i: (0, 0)),
                  pl.BlockSpec((s, 128), lambda b, i: (b, N_HEADS)),
                  pl.BlockSpec((128, s), lambda b, i: (3, b)),
                  pl.BlockSpec((s, 128), lambda b, i: (b, N_HEADS + 1)),
                  pl.BlockSpec(m_t.shape, lambda b, i: (0, 0)),
                  pl.BlockSpec((1, D_BRANCH), lambda b, i: (0, 0))],
        out_specs=pl.BlockSpec((1, qb, D_BRANCH), lambda b, i: (b, i, 0)),
        out_shape=jax.ShapeDtypeStruct((bsz, s, D_BRANCH), F32),
        scratch_shapes=[pltpu.VMEM((N_HEADS * qb, 128), F32), pltpu.VMEM((N_HEADS * qb, 128), F32),
                        pltpu.VMEM((2, N_HEADS * qb, min(NSA_KV_TILE, s)), F32)],
        compiler_params=_params(("parallel", "arbitrary")),
        name="nsa_attention",
    )(proj, proj, kc_bf, vc_t, kt_all, penalty, v_aug, kt_all, v_aug, m_t, gate_b_pad)


def _rwkv_kernel(c, pr_ref, pk_ref, pv_ref, pw_ref, mu_ref, w0_ref, wuph_ref, wupl_ref, a0_ref, auph_ref, aupl_ref,
                 kk_ref, ka_ref, rk_ref, lng_ref, lnb_ref, o_ref, prev_ref, st_ref):
    nb, blk, _ = pr_ref.shape
    rows = nb * blk
    n = nb * c

    @pl.when(pl.program_id(0) == 0)
    def _():
        prev_ref[...] = jnp.zeros_like(prev_ref)
        st_ref[...] = jnp.zeros_like(st_ref)

    brow = lax.broadcasted_iota(jnp.int32, (rows, 1), 0)

    def shifted(ref, k):
        x = ref[...].reshape(rows, D_BRANCH)
        prev = pltpu.roll(x, 1, axis=0)
        for b in range(nb):
            prev = jnp.where(brow == b * blk, prev_ref[k, b:b + 1, :], prev)
        for b in range(nb):
            prev_ref[k, b:b + 1, :] = x[(b + 1) * blk - 1:(b + 1) * blk, :]
        return x + (prev - x) * mu_ref[k:k + 1, :]

    r_all = shifted(pr_ref, 0)
    k_all = shifted(pk_ref, 1)
    v_all = shifted(pv_ref, 2)
    xw = shifted(pw_ref, 3)

    ones_bd = _block_ones()
    wlog = w0_ref[...] + _dot3_pre(jnp.tanh(xw), wuph_ref[...], wupl_ref[...])
    ld_all = -jnp.exp(-_softplus(-wlog) - 0.5)
    a_lr = _sigmoid(a0_ref[...] + _dot3_pre(xw, auph_ref[...], aupl_ref[...]))
    kk = k_all * kk_ref[...]
    kk = kk / jnp.maximum(jnp.sqrt(_dot_lx(kk * kk, ones_bd)), 1e-12)
    k_all = k_all * (1.0 + (a_lr - 1.0) * ka_ref[...])
    a_all = -kk
    b_all = kk * a_lr

    row = lax.broadcasted_iota(jnp.int32, (n, 1), 0)
    col = lax.broadcasted_iota(jnp.int32, (1, n), 1)
    rowb = row // c
    colb = col // c
    same = rowb == colb
    tri = jnp.where(same & (col <= row), 1.0, 0.0).astype(BF16)
    allm = jnp.where(same, 1.0, 0.0).astype(BF16)
    upper = same & (row < col)
    upper_eq = same & (row <= col)
    eye = jnp.where(row == col, 1.0, 0.0)
    col2b = jnp.concatenate([colb, colb], axis=1)
    hm = _head_masks()
    bd = ones_bd.astype(F32)
    mean_m = _block_ones(1.0 / HEAD_DIM)

    def chunk_rows(x, j):
        return jnp.concatenate([x[b * blk + j * c:b * blk + (j + 1) * c] for b in range(nb)], axis=0)

    def prepare_stages(js):
        ps = [dict(j=j) for j in js]

        def slices():
            for p in ps:
                p["r"], p["k"], p["v"], p["ld"], p["a"], p["b"] = (
                    chunk_rows(x, p["j"]) for x in (r_all, k_all, v_all, ld_all, a_all, b_all))
                p["cs"] = _dot_xl(tri, p["ld"])
                p["tot"] = _dot_xl(allm, p["ld"])

        def pair_products():
            for p in ps:
                cs, tot = p["cs"], p["tot"]
                a_t = p["a"] * jnp.exp(cs - p["ld"])
                r_t = p["r"] * jnp.exp(cs)
                inv = jnp.exp(-cs)
                fin = jnp.exp(tot - cs)
                lhs = jnp.concatenate([(p["b"] * inv) * hm[h] for h in range(N_HEADS)]
                                      + [(p["k"] * inv) * hm[h] for h in range(N_HEADS)], axis=0)
                rhs = jnp.concatenate([a_t, r_t], axis=0)
                p["small"] = _dg(lhs.astype(BF16), rhs.astype(BF16), 1, 1)
                ar_t = jnp.concatenate([a_t.T, r_t.T], axis=1)
                p["v_t"] = p["v"].T.astype(BF16)
                p["ar_b"] = [jnp.where(col2b == bb, ar_t, 0.0).astype(BF16) for bb in range(nb)]
                p["bk_b"] = [jnp.concatenate([jnp.where(rowb == bb, p["b"] * fin, 0.0),
                                              jnp.where(rowb == bb, p["k"] * fin, 0.0)], axis=0).astype(BF16)
                             for bb in range(nb)]
                p["g_rows"] = [jnp.exp(tot[bb * c:bb * c + 1, :]) for bb in range(nb)]
                p["bonus"] = _dot_lx(p["r"] * p["k"] * rk_ref[...], ones_bd) * p["v"]

        def masks():
            for p in ps:
                small = p["small"]
                p["pw"], p["tinv"], p["brs"], p["zs"], p["y0s"] = [], [], [], [], []
                for h in range(N_HEADS):
                    ba = jnp.where(upper, small[h * n:(h + 1) * n, 0:n], 0.0)
                    br = jnp.where(upper_eq, small[h * n:(h + 1) * n, n:2 * n], 0.0).astype(BF16)
                    ka = jnp.where(upper, small[(N_HEADS + h) * n:(N_HEADS + h + 1) * n, 0:n], 0.0).astype(BF16)
                    kr = jnp.where(upper_eq, small[(N_HEADS + h) * n:(N_HEADS + h + 1) * n, n:2 * n],
                                   0.0).astype(BF16)
                    vh = p["v_t"][h * HEAD_DIM:(h + 1) * HEAD_DIM, :]
                    p["pw"].append(ba)
                    p["tinv"].append(eye + ba)
                    p["brs"].append(br)
                    p["zs"].append(_dg(vh, ka))
                    p["y0s"].append(_dg(vh, kr))

        def inverse_level():
            for p in ps:
                for h in range(N_HEADS):
                    pw_bf = p["pw"][h].astype(BF16)
                    p["pw"][h] = _dg(pw_bf, pw_bf)
            for p in ps:
                for h in range(N_HEADS):
                    p["tinv"][h] = p["tinv"][h] + _dg(p["tinv"][h].astype(BF16), p["pw"][h].astype(BF16))

        def finish():
            for p in ps:
                p["tinvs"] = [t.astype(BF16) for t in p["tinv"]]

        levels = max(int(math.log2(c)) - 1, 0)
        return ps, [slices, pair_products, masks] + [inverse_level] * levels + [finish]

    def advance_stages(p):
        w = {}

        def state_products():
            xy_t = jnp.zeros((D_BRANCH, 2 * n), F32)
            for bb in range(nb):
                xy_t = xy_t + _dg(st_ref[bb].astype(BF16), p["ar_b"][bb])
            w["xy_t"] = xy_t

        def solve():
            u_rows, y_rows = [], []
            for h in range(N_HEADS):
                hs = slice(h * HEAD_DIM, (h + 1) * HEAD_DIM)
                uh = _dg((w["xy_t"][hs, 0:n] + p["zs"][h]).astype(BF16), p["tinvs"][h]).astype(BF16)
                u_rows.append(uh)
                y_rows.append(w["xy_t"][hs, n:2 * n] + _dg(uh, p["brs"][h]) + p["y0s"][h])
            w["u_t"] = jnp.concatenate(u_rows, axis=0)
            w["y_rows"] = y_rows

        def update_state():
            uv = jnp.concatenate([w["u_t"], p["v_t"]], axis=1)
            for bb in range(nb):
                st_ref[bb] = st_ref[bb] * p["g_rows"][bb] + bd * _dg(uv, p["bk_b"][bb])

        def epilogue():
            y = jnp.concatenate(w["y_rows"], axis=0).T
            mean = _dot_lx(y, mean_m)
            yc = y - mean
            var = _dot_lx(yc * yc, mean_m)
            out = yc * lax.rsqrt(var + RWKV_GN_EPS) * lng_ref[...] + lnb_ref[...]
            j = p["j"]
            o_ref[:, j * c:(j + 1) * c, :] = (out + p["bonus"]).reshape(nb, c, D_BRANCH)

        return [state_products, solve, update_state, epilogue]

    nchunk = blk // c
    groups = [list(range(g, min(g + RWKV_GROUP, nchunk))) for g in range(0, nchunk, RWKV_GROUP)]
    ps_prev, stages = prepare_stages(groups[0])
    for stage in stages:
        stage()
    for grp in groups[1:]:
        ps_next, prep = prepare_stages(grp)
        adv = [stage for p in ps_prev for stage in advance_stages(p)]
        for idx in range(max(len(prep), len(adv))):
            if idx < len(prep):
                prep[idx]()
            if idx < len(adv):
                adv[idx]()
        ps_prev = ps_next
    for p in ps_prev:
        for stage in advance_stages(p):
            stage()


def _rwkv_mixer(proj, mu, w0, w_up, a0, a_up, k_k, k_a, r_k, ln_g, ln_b):
    bsz, s, _ = proj.shape
    c = min(RWKV_CHUNK, s)
    tb = min(RWKV_BLOCK, s)
    mu_p = jnp.zeros((4, D_BRANCH), F32)
    mu_p = mu_p.at[0:3, :].set(mu[:3 * D_BRANCH].reshape(3, D_BRANCH))
    mu_p = mu_p.at[3, :2 * RWKV_RANK].set(mu[3 * D_BRANCH:])
    wup_h, wup_l = _split2(jnp.zeros((D_BRANCH, D_BRANCH), F32).at[:RWKV_RANK, :].set(w_up))
    aup_h, aup_l = _split2(jnp.zeros((D_BRANCH, D_BRANCH), F32).at[RWKV_RANK:2 * RWKV_RANK, :].set(a_up))
    row = lambda a: a.reshape(1, D_BRANCH)
    blk = lambda cidx: pl.BlockSpec((bsz, tb, D_BRANCH), lambda i, cidx=cidx: (0, i, cidx))
    rowspec = pl.BlockSpec((1, D_BRANCH), lambda i: (0, 0))
    matspec = pl.BlockSpec((D_BRANCH, D_BRANCH), lambda i: (0, 0))
    return pl.pallas_call(
        functools.partial(_rwkv_kernel, c),
        grid=(s // tb,),
        in_specs=[blk(8), blk(9), blk(10), blk(11),
                  pl.BlockSpec((4, D_BRANCH), lambda i: (0, 0)),
                  rowspec, matspec, matspec, rowspec, matspec, matspec,
                  rowspec, rowspec, rowspec, rowspec, rowspec],
        out_specs=pl.BlockSpec((bsz, tb, D_BRANCH), lambda i: (0, i, 0)),
        out_shape=jax.ShapeDtypeStruct((bsz, s, D_BRANCH), F32),
        scratch_shapes=[pltpu.VMEM((4, 8, D_BRANCH), F32), pltpu.VMEM((bsz, D_BRANCH, D_BRANCH), F32)],
        compiler_params=_params(("arbitrary",)),
        name="rwkv7",
    )(proj, proj, proj, proj, mu_p, row(w0), wup_h, wup_l, row(a0), aup_h, aup_l, row(k_k), row(k_a),
      row(r_k), row(ln_g), row(ln_b))


def _lane_sum_selector():
    r = lax.broadcasted_iota(jnp.int32, (128, 128), 0)
    return jnp.where(r == HEAD_DIM, 1.0, 0.0).astype(BF16)


def _placement(h):
    r = lax.broadcasted_iota(jnp.int32, (128, D_BRANCH), 0)
    c = lax.broadcasted_iota(jnp.int32, (128, D_BRANCH), 1)
    return jnp.where((r < HEAD_DIM) & (c == r + h * HEAD_DIM), 1.0, 0.0).astype(BF16)


def _diff_kernel(lambda_init, q_ref, k_ref, v_ref, lam_ref, g_ref, o_ref, acc_ref, m_ref, s_ref):
    qb = q_ref.shape[1]
    tk = min(DIFF_KV_TILE, k_ref.shape[1])
    rep = tk // 128
    q0 = pl.program_id(1) * qb
    lane = lax.broadcasted_iota(jnp.int32, (1, 128), 1)
    q = q_ref[0] * (DIFF_QK_DIM ** -0.5 * LOG2E)
    qs = []
    for h in range(N_HEADS):
        pair, half = divmod(h, 2)
        qp = q[:, pair * 128:(pair + 1) * 128]
        both = []
        for cc in range(2):
            lo = half * HEAD_DIM + cc * DIFF_QK_DIM
            both.append(jnp.where((lane >= lo) & (lane < lo + DIFF_QK_DIM), qp, 0.0).astype(BF16))
        qs.append(jnp.concatenate(both, axis=0))
    t = q0 + lax.broadcasted_iota(jnp.int32, (qb, 1), 0)
    t2 = jnp.concatenate([t, t], axis=0)
    m_ref[...] = jnp.full(m_ref.shape, NEG_INF, F32)
    acc_ref[...] = jnp.zeros(acc_ref.shape, F32)

    def scores(j, h):
        start = pl.multiple_of(j * tk, tk)
        return _dg(qs[h], k_ref[(h // 2) * 128:(h // 2 + 1) * 128, pl.ds(start, tk)])

    def absorb(j, h, slot, masked=False):
        start = pl.multiple_of(j * tk, tk)
        s = s_ref[slot, h]
        if masked:
            s = jnp.where((start + lax.broadcasted_iota(jnp.int32, (1, tk), 1)) <= t2, s, NEG_INF)
        m_old = m_ref[h]
        m_new = jnp.maximum(m_old, jnp.max(s, axis=-1, keepdims=True))
        p = jnp.exp2(s - jnp.concatenate([m_new] * rep, axis=1))
        acc_ref[h] = jnp.exp2(m_old - m_new) * acc_ref[h] + _dg(p.astype(BF16), v_ref[pl.ds(start, tk), h * 128:(h + 1) * 128])
        m_ref[h] = m_new

    def step(j, slot, prefetch=True, masked=False):
        for h in range(N_HEADS):
            if prefetch:
                s_ref[1 - slot, h] = scores(j + 1, h)
            absorb(j, h, slot, masked)

    assert qb <= tk
    n_full = q0 // tk
    for h in range(N_HEADS):
        s_ref[0, h] = scores(0, h)

    def body(pair, carry):
        step(2 * pair, 0)
        step(2 * pair + 1, 1)
        return carry

    lax.fori_loop(0, n_full // 2, body, 0)

    @pl.when(n_full % 2 == 1)
    def _():
        step(n_full - 1, 0)
        step(n_full, 1, prefetch=False, masked=True)

    @pl.when(n_full % 2 == 0)
    def _():
        step(n_full, 0, prefetch=False, masked=True)

    lam = lam_ref[...]
    lam_full = (jnp.exp(jnp.sum(lam[0:1] * lam[1:2], axis=-1, keepdims=True))
                - jnp.exp(jnp.sum(lam[2:3] * lam[3:4], axis=-1, keepdims=True)) + lambda_init)
    lsel = _lane_sum_selector()
    out = jnp.zeros((qb, D_BRANCH), F32)
    for h in range(N_HEADS):
        a = acc_ref[h]
        a = a / _dot_lx2(a, lsel)
        d = a[0:qb] - lam_full * a[qb:2 * qb]
        out = out + _dot_lx2(d, _placement(h))
    ms = _dot_lx(out * out, _block_ones(1.0 / HEAD_DIM))
    o_ref[0] = out * lax.rsqrt(ms + DIFF_EPS) * g_ref[...] * (1.0 - lambda_init)


def _diff_mixer(proj, kt_all, v_aug, lam, subln_g, lambda_init):
    bsz, s, _ = proj.shape
    qb = min(DIFF_Q_BLOCK, s)
    g_rep = jnp.tile(subln_g, N_HEADS).reshape(1, D_BRANCH)
    return pl.pallas_call(
        functools.partial(_diff_kernel, lambda_init),
        grid=(bsz, s // qb),
        in_specs=[pl.BlockSpec((1, qb, D_BRANCH), lambda b, i: (b, i, 12)),
                  pl.BlockSpec((D_BRANCH, s), lambda b, i: (0, b)),
                  pl.BlockSpec((s, N_HEADS * 128), lambda b, i: (b, 0)),
                  pl.BlockSpec(lam.shape, lambda b, i: (0, 0)),
                  pl.BlockSpec((1, D_BRANCH), lambda b, i: (0, 0))],
        out_specs=pl.BlockSpec((1, qb, D_BRANCH), lambda b, i: (b, i, 0)),
        out_shape=jax.ShapeDtypeStruct((bsz, s, D_BRANCH), F32),
        scratch_shapes=[pltpu.VMEM((N_HEADS, 2 * qb, 128), F32),
                        pltpu.VMEM((N_HEADS, 2 * qb, 128), F32),
                        pltpu.VMEM((2, N_HEADS, 2 * qb, min(DIFF_KV_TILE, s)), F32)],
        compiler_params=_params(("parallel", "arbitrary")),
        name="diff_attention",
    )(proj, kt_all, v_aug, lam, g_rep)


def _block_diag(w):
    n = w.shape[0] * w.shape[1]
    out = jnp.zeros((n, n), F32)
    for h in range(w.shape[0]):
        out = out.at[h * HEAD_DIM:(h + 1) * HEAD_DIM, h * HEAD_DIM:(h + 1) * HEAD_DIM].set(w[h])
    return out


def _layout_w_in(w):
    d = w.shape[0]
    nsa_end = 5 * D_BRANCH + D_BRANCH + 6 * HEAD_DIM + N_NSA_BRANCH * N_HEADS
    rwkv_end = nsa_end + 3 * D_BRANCH + 2 * RWKV_RANK
    pad1 = 8 * D_BRANCH - nsa_end
    pad2 = 12 * D_BRANCH - (rwkv_end + pad1)
    return jnp.concatenate([w[:, :nsa_end], jnp.zeros((d, pad1), w.dtype), w[:, nsa_end:rwkv_end],
                            jnp.zeros((d, pad2), w.dtype), w[:, rwkv_end:rwkv_end + D_BRANCH]], axis=1)


def _layout_wt_in(w):
    nsa0 = 5 * D_BRANCH
    ks0 = nsa0 + D_BRANCH + 2 * HEAD_DIM
    kw0 = ks0 + 2 * HEAD_DIM
    dk0 = w.shape[1] - 2 * D_BRANCH
    ks, kw = w[:, ks0:ks0 + HEAD_DIM], w[:, kw0:kw0 + HEAD_DIM]
    return jnp.concatenate([w[:, dk0:dk0 + D_BRANCH], ks, ks, kw, kw], axis=1).T


def _layout_wv_in(w):
    nsa0 = 5 * D_BRANCH
    vs0 = nsa0 + D_BRANCH + 3 * HEAD_DIM
    vw0 = vs0 + 2 * HEAD_DIM
    dv0 = w.shape[1] - D_BRANCH
    pad = jnp.zeros((w.shape[0], 128 - HEAD_DIM), w.dtype)
    cols = [w[:, dv0 + h * HEAD_DIM:dv0 + (h + 1) * HEAD_DIM] for h in range(N_HEADS)]
    cols += [w[:, vs0:vs0 + HEAD_DIM], w[:, vw0:vw0 + HEAD_DIM]]
    return jnp.concatenate([piece for c in cols for piece in (c, pad)], axis=1)


def kernel(x, norm_gain, w_in, w_out, final_gain, lru_conv_w, lru_conv_b, lru_wa, lru_ba, lru_wx, lru_bx, lru_lambda, nsa_cmp_pos, nsa_cmp_w1, nsa_cmp_w2, nsa_gate_b, rwkv_mu, rwkv_w0, rwkv_w_up, rwkv_a0, rwkv_a_up, rwkv_k_k, rwkv_k_a, rwkv_r_k, rwkv_ln_g, rwkv_ln_b, diff_lambda, diff_subln_g):
    bsz, s, d = x.shape
    depth = w_in.shape[0]
    t = bsz * s
    x2d = x.reshape(t, d)
    row = lambda a: a.reshape(1, -1)
    w_in_bf = w_in.astype(BF16)
    for l in range(depth):
        w_bf = w_in_bf[l]
        proj2d, kt_all, v_aug = _in_proj(x2d, row(norm_gain[l]), _layout_w_in(w_bf), _layout_wt_in(w_bf),
                                         _layout_wv_in(w_bf))
        proj = proj2d.reshape(bsz, s, N_PROJ_BLOCKS * D_BRANCH)
        o_lru = _lru_mixer(proj, lru_conv_w[l], row(lru_conv_b[l]), _block_diag(lru_wa[l]), row(lru_ba[l]),
                           _block_diag(lru_wx[l]), row(lru_bx[l]), row(lru_lambda[l]))
        o_nsa = _nsa_mixer(proj, kt_all, v_aug, nsa_cmp_pos[l], nsa_cmp_w1[l], nsa_cmp_w2[l], nsa_gate_b[l])
        o_rwkv = _rwkv_mixer(proj, rwkv_mu[l], rwkv_w0[l], rwkv_w_up[l], rwkv_a0[l], rwkv_a_up[l],
                             rwkv_k_k[l], rwkv_k_a[l], rwkv_r_k[l].reshape(-1), rwkv_ln_g[l], rwkv_ln_b[l])
        lambda_init = 0.8 - 0.6 * math.exp(-0.3 * l)
        o_diff = _diff_mixer(proj, kt_all, v_aug, diff_lambda[l], diff_subln_g[l], lambda_init)
        mixers = [o.reshape(t, D_BRANCH) for o in (o_lru, o_nsa, o_rwkv, o_diff)]
        x2d = _out_proj(mixers, proj2d, x2d, w_out[l].astype(BF16), row(final_gain), final=(l == depth - 1))
    return x2d.reshape(bsz, s, d)
```

```python
import functools
import math

import jax
import jax.numpy as jnp
from jax import lax
from jax.experimental import pallas as pl
from jax.experimental.pallas import tpu as pltpu

F32 = jnp.float32
BF16 = jnp.bfloat16

HEAD_DIM = 64
N_HEADS = 4
D_BRANCH = N_HEADS * HEAD_DIM
NORM_EPS = 1e-6
NEG_INF = -1e30
LRU_CONV = 4
LRU_C = 8.0
CMP_BLOCK = 32
CMP_STRIDE = 16
SEL_BLOCK = 64
N_SELECT = 16
WINDOW = 512
N_NSA_BRANCH = 3
RWKV_RANK = 32
RWKV_GN_EPS = 64e-5
DIFF_QK_DIM = HEAD_DIM // 2
DIFF_EPS = 1e-5
N_PROJ_BLOCKS = 13
RWKV_CHUNK = 64
RWKV_BLOCK = 512
RWKV_GROUP = 4
VMEM_LIMIT = 56 * 1024 * 1024
LOG2E = 1.4426950408889634
DIFF_Q_BLOCK = 256
DIFF_KV_TILE = 512
NSA_Q_BLOCK = 512
NSA_KV_TILE = 512
MASK_BIG = 1e30


def _dg(a, b, ca=1, cb=0):
    return lax.dot_general(a, b, (((ca,), (cb,)), ((), ())), preferred_element_type=F32)


def _split2(a):
    hi = a.astype(BF16)
    lo = (a - hi.astype(F32)).astype(BF16)
    return hi, lo


def _dot3(a, b, ca=1, cb=0):
    ah, al = _split2(a)
    bh, bl = _split2(b)
    return _dg(ah, bh, ca, cb) + (_dg(ah, bl, ca, cb) + _dg(al, bh, ca, cb))


def _dot3_pre(a, bh, bl, ca=1, cb=0):
    ah, al = _split2(a)
    return _dg(ah, bh, ca, cb) + (_dg(ah, bl, ca, cb) + _dg(al, bh, ca, cb))


def _dot_lx(a, b_exact, ca=1, cb=0):
    a1 = a.astype(BF16)
    r = a - a1.astype(F32)
    a2 = r.astype(BF16)
    a3 = (r - a2.astype(F32)).astype(BF16)
    return _dg(a1, b_exact, ca, cb) + (_dg(a2, b_exact, ca, cb) + _dg(a3, b_exact, ca, cb))


def _dot_lx2(a, b_exact, ca=1, cb=0):
    a1, a2 = _split2(a)
    return _dg(a1, b_exact, ca, cb) + _dg(a2, b_exact, ca, cb)


def _dot_xl(a_exact, b, ca=1, cb=0):
    b1 = b.astype(BF16)
    r = b - b1.astype(F32)
    b2 = r.astype(BF16)
    b3 = (r - b2.astype(F32)).astype(BF16)
    return _dg(a_exact, b1, ca, cb) + (_dg(a_exact, b2, ca, cb) + _dg(a_exact, b3, ca, cb))


def _head_masks(width=D_BRANCH, group=HEAD_DIM):
    lane = lax.broadcasted_iota(jnp.int32, (1, width), 1)
    return [((lane >= h * group) & (lane < (h + 1) * group)).astype(F32) for h in range(width // group)]


def _block_ones(scale=1.0):
    r = lax.broadcasted_iota(jnp.int32, (D_BRANCH, D_BRANCH), 0) // HEAD_DIM
    c = lax.broadcasted_iota(jnp.int32, (D_BRANCH, D_BRANCH), 1) // HEAD_DIM
    return jnp.where(r == c, scale, 0.0).astype(BF16)


def _sigmoid(x):
    return 1.0 / (1.0 + jnp.exp(-x))


def _softplus(x):
    return jnp.maximum(x, 0.0) + jnp.log1p(jnp.exp(-jnp.abs(x)))


def _params(sem):
    return pltpu.CompilerParams(dimension_semantics=sem, vmem_limit_bytes=VMEM_LIMIT)


def _proj_kernel(x_ref, g_ref, w_ref, wt_ref, wv_ref, o_ref, ot_ref, ov_ref):
    x = x_ref[...]
    ms = jnp.mean(x * x, axis=-1, keepdims=True)
    h = (x * lax.rsqrt(ms + NORM_EPS) * g_ref[...]).astype(BF16)
    o_ref[...] = _dg(h, w_ref[...])

    @pl.when(pl.program_id(1) == 0)
    def _():
        ot_ref[...] = _dg(wt_ref[...], h, 1, 1).astype(BF16)
        lane = lax.broadcasted_iota(jnp.int32, (1, wv_ref.shape[1]), 1)
        ov_ref[...] = jnp.where(lane % 128 == HEAD_DIM, 1.0, _dg(h, wv_ref[...])).astype(BF16)


def _in_proj(x2d, gain, w_bf, wt_bf, wv_bf, tm=1024, tn=N_PROJ_BLOCKS * 128):
    t, d = x2d.shape
    n = w_bf.shape[1]
    nt = wt_bf.shape[0]
    nv = wv_bf.shape[1]
    tm = min(tm, t)
    return pl.pallas_call(
        _proj_kernel,
        grid=(t // tm, n // tn),
        in_specs=[pl.BlockSpec((tm, d), lambda i, j: (i, 0)),
                  pl.BlockSpec((1, d), lambda i, j: (0, 0)),
                  pl.BlockSpec((d, tn), lambda i, j: (0, j)),
                  pl.BlockSpec((nt, d), lambda i, j: (0, 0)),
                  pl.BlockSpec((d, nv), lambda i, j: (0, 0))],
        out_specs=[pl.BlockSpec((tm, tn), lambda i, j: (i, j)),
                   pl.BlockSpec((nt, tm), lambda i, j: (0, i)),
                   pl.BlockSpec((tm, nv), lambda i, j: (i, 0))],
        out_shape=[jax.ShapeDtypeStruct((t, n), F32), jax.ShapeDtypeStruct((nt, t), BF16),
                   jax.ShapeDtypeStruct((t, nv), BF16)],
        compiler_params=_params(("parallel", "arbitrary")),
        name="in_proj",
    )(x2d, gain, w_bf, wt_bf, wv_bf)


def _out_kernel(final, o1_ref, o2_ref, o3_ref, o4_ref, gate_ref, x_ref, w_ref, fg_ref, y_ref):
    mix = jnp.concatenate([o1_ref[...], o2_ref[...], o3_ref[...], o4_ref[...]], axis=-1)
    g = gate_ref[...]
    z = mix * (g * _sigmoid(g))
    y = x_ref[...] + _dg(z.astype(BF16), w_ref[...])
    if final:
        ms = jnp.mean(y * y, axis=-1, keepdims=True)
        y = y * lax.rsqrt(ms + NORM_EPS) * fg_ref[...]
    y_ref[...] = y


def _out_proj(mixers, proj2d, x2d, w_bf, final_gain, final, tm=1024):
    t, d = x2d.shape
    dm = w_bf.shape[0]
    tm = min(tm, t)
    branch = pl.BlockSpec((tm, D_BRANCH), lambda i: (i, 0))
    return pl.pallas_call(
        functools.partial(_out_kernel, final),
        grid=(t // tm,),
        in_specs=[branch, branch, branch, branch,
                  pl.BlockSpec((tm, dm), lambda i: (i, 0)),
                  pl.BlockSpec((tm, d), lambda i: (i, 0)),
                  pl.BlockSpec((dm, d), lambda i: (0, 0)),
                  pl.BlockSpec((1, d), lambda i: (0, 0))],
        out_specs=pl.BlockSpec((tm, d), lambda i: (i, 0)),
        out_shape=jax.ShapeDtypeStruct((t, d), F32),
        compiler_params=_params(("parallel",)),
        name="out_proj",
    )(*mixers, proj2d, x2d, w_bf, final_gain)


def _lru_kernel(u_ref, cw_ref, cb_ref, wah_ref, wal_ref, ba_ref, wxh_ref, wxl_ref, bx_ref, lam_ref,
                o_ref, ubuf, hcar):
    ts = u_ref.shape[1]

    @pl.when(pl.program_id(1) == 0)
    def _():
        ubuf[0:8, :] = jnp.zeros((8, D_BRANCH), F32)
        hcar[...] = jnp.zeros_like(hcar)

    u = u_ref[0]
    ubuf[8:8 + ts, :] = u
    xc = cb_ref[...] + cw_ref[0:1, :] * ubuf[5:5 + ts, :]
    for j in range(1, LRU_CONV):
        xc = xc + cw_ref[j:j + 1, :] * ubuf[5 + j:5 + j + ts, :]
    ubuf[0:8, :] = u[ts - 8:ts, :]

    r = _sigmoid(_dot3_pre(xc, wah_ref[...], wal_ref[...]) + ba_ref[...])
    gi = _sigmoid(_dot3_pre(xc, wxh_ref[...], wxl_ref[...]) + bx_ref[...])
    log_a = (-LRU_C) * r * _softplus(-lam_ref[...])
    a = jnp.exp(log_a)
    b = jnp.sqrt(jnp.tanh(-log_a) * (a * a + 1.0)) * (gi * xc)

    row = lax.broadcasted_iota(jnp.int32, (ts, 1), 0)
    d = 1
    while d < ts:
        keep = row >= d
        a_s = jnp.where(keep, pltpu.roll(a, d, axis=0), 1.0)
        b_s = jnp.where(keep, pltpu.roll(b, d, axis=0), 0.0)
        b = a * b_s + b
        a = a * a_s
        d *= 2
    h = b + a * hcar[...]
    o_ref[0] = h
    hcar[...] = h[ts - 1:ts, :]


def _lru_mixer(proj, conv_w, conv_b, wa_bd, ba, wx_bd, bx, lam, ts=512):
    bsz, s, _ = proj.shape
    wah, wal = _split2(wa_bd)
    wxh, wxl = _split2(wx_bd)
    row = pl.BlockSpec((1, D_BRANCH), lambda b, j: (0, 0))
    mat = pl.BlockSpec((D_BRANCH, D_BRANCH), lambda b, j: (0, 0))
    return pl.pallas_call(
        _lru_kernel,
        grid=(bsz, s // ts),
        in_specs=[pl.BlockSpec((1, ts, D_BRANCH), lambda b, j: (b, j, 4)),
                  pl.BlockSpec((LRU_CONV, D_BRANCH), lambda b, j: (0, 0)),
                  row, mat, mat, row, mat, mat, row, row],
        out_specs=pl.BlockSpec((1, ts, D_BRANCH), lambda b, j: (b, j, 0)),
        out_shape=jax.ShapeDtypeStruct((bsz, s, D_BRANCH), F32),
        scratch_shapes=[pltpu.VMEM((ts + 8, D_BRANCH), F32), pltpu.VMEM((1, D_BRANCH), F32)],
        compiler_params=_params(("parallel", "arbitrary")),
        name="rg_lru",
    )(proj, conv_w, conv_b, wah, wal, ba, wxh, wxl, bx, lam)


def _cmp_kernel(rk_ref, rv_ref, pos_ref, w1_ref, w2_ref, kc_ref, vct_ref):
    half = (CMP_BLOCK // 2) * HEAD_DIM
    nrow = rk_ref.shape[1]
    outs = []
    for idx, r_ref in enumerate((rk_ref, rv_ref)):
        r = r_ref[0]
        w1 = w1_ref[idx]
        lo_half = _dot3(r, w1[0:half, :])
        hi_half = _dot3(r, w1[half:2 * half, :])
        bias = _dot3(jnp.broadcast_to(pos_ref[idx], (8, 2 * half)), w1)[0:1, :]
        hid = lo_half + pltpu.roll(hi_half, nrow - 1, axis=0) + bias
        outs.append(_dot3(jax.nn.gelu(hid), w2_ref[idx]))
    kc_ref[0] = outs[0].astype(BF16)
    vct_ref[0] = outs[1].T.astype(BF16)


def _nsa_compress(rk, rv, pos_flat, w1, w2_pair):
    bsz, nrow, width = rk.shape
    blk = pl.BlockSpec((1, nrow, width), lambda b: (b, 0, 0))
    out = pl.BlockSpec((1, nrow, 128), lambda b: (b, 0, 0))
    return pl.pallas_call(
        _cmp_kernel,
        grid=(bsz,),
        in_specs=[blk, blk,
                  pl.BlockSpec(pos_flat.shape, lambda b: (0, 0, 0)),
                  pl.BlockSpec(w1.shape, lambda b: (0, 0, 0)),
                  pl.BlockSpec(w2_pair.shape, lambda b: (0, 0, 0))],
        out_specs=[out, pl.BlockSpec((1, 128, nrow), lambda b: (b, 0, 0))],
        out_shape=[jax.ShapeDtypeStruct((bsz, nrow, 128), BF16), jax.ShapeDtypeStruct((bsz, 128, nrow), BF16)],
        compiler_params=_params(("parallel",)),
        name="nsa_compress",
    )(rk, rv, pos_flat, w1, w2_pair)


def _nsa_kernel(n_top, q_ref, g_ref, kc_ref, vct_ref, ks_ref, pen_ref, vs_ref, kw_ref, vw_ref,
                mt_ref, gb_ref, o_ref, acc_ref, m_ref, s_ref):
    qb = q_ref.shape[1]
    n_cmp = kc_ref.shape[1]
    n_sel = mt_ref.shape[0]
    tk = min(NSA_KV_TILE, vs_ref.shape[0])
    rep = tk // 128
    q0 = pl.program_id(1) * qb
    lane = lax.broadcasted_iota(jnp.int32, (1, 128), 1)
    q = q_ref[0] * (HEAD_DIM ** -0.5)
    parts = []
    for h in range(N_HEADS):
        pair, half = divmod(h, 2)
        keep = (lane >= half * HEAD_DIM) & (lane < (half + 1) * HEAD_DIM)
        parts.append(jnp.where(keep, q[:, pair * 128:(pair + 1) * 128], 0.0))
    q4 = jnp.concatenate(parts, axis=0)
    q4_log2 = (q4 * LOG2E).astype(BF16)
    t = q0 + lax.broadcasted_iota(jnp.int32, (qb, 1), 0)
    t4 = jnp.concatenate([t] * N_HEADS, axis=0)
    t_row = q0 + lax.broadcasted_iota(jnp.int32, (1, qb), 1)

    cend = lax.broadcasted_iota(jnp.int32, (n_cmp, 1), 0) * CMP_STRIDE + (CMP_BLOCK - 1)
    valid = cend <= t_row
    anyv = (t_row >= CMP_BLOCK - 1).astype(F32)
    q4_bf = q4.astype(BF16)
    oc_heads = []
    imp_t = jnp.zeros((n_cmp, qb), F32)
    for h in range(N_HEADS):
        sct = jnp.where(valid, _dg(kc_ref[0], q4_bf[h * qb:(h + 1) * qb], 1, 1), NEG_INF)
        e = jnp.exp(sct - jnp.max(sct, axis=0, keepdims=True))
        p_ct = e / jnp.sum(e, axis=0, keepdims=True)
        oc_heads.append((_dg(vct_ref[0], p_ct.astype(BF16)) * anyv).T)
        imp_t = imp_t + jnp.where(valid, p_ct, 0.0)
    pslc_t = _dot_xl(mt_ref[...], imp_t)

    blk = lax.broadcasted_iota(jnp.int32, (n_sel, 1), 0)
    blk_f = blk.astype(F32)
    cur = t_row // SEL_BLOCK
    forced = (blk == 0) | (blk == cur) | (blk == cur - 1)
    n_forced = 3
    score = jnp.where(blk > cur, -1.0, jnp.where(forced, -3e38, pslc_t))
    sub = min(qb, 256)
    nw = sub + WINDOW
    lsel = _lane_sum_selector()
    slabs = []
    for a in range(qb // sub):
        start_a = pl.multiple_of(jnp.maximum(q0 + a * sub - WINDOW, 0), 128)
        wpos = start_a + lax.broadcasted_iota(jnp.int32, (1, nw), 1)
        t_a = t[a * sub:(a + 1) * sub]
        slabs.append((start_a, (wpos <= t_a) & (wpos > t_a - WINDOW)))

    def window_head(h):
        parts = []
        for a, (start_a, allow_a) in enumerate(slabs):
            rows = slice(h * qb + a * sub, h * qb + (a + 1) * sub)
            s_w = jnp.where(allow_a, _dg(q4_log2[rows], kw_ref[:, pl.ds(start_a, nw)]), NEG_INF)
            p_w = jnp.exp2(s_w - jnp.max(s_w, axis=-1, keepdims=True))
            parts.append(_dg(p_w.astype(BF16), vw_ref[pl.ds(start_a, nw), :]))
        aw = jnp.concatenate(parts, axis=0)
        return aw / _dot_lx2(aw, lsel)

    ow_heads = []
    n_rounds = max(n_top - n_forced, 0)
    every = max(n_rounds // N_HEADS, 1)
    sel_t = jnp.where(forced, 1.0, 0.0)
    for rnd in range(n_rounds):
        best = jnp.max(score, axis=0, keepdims=True)
        idx = jnp.min(jnp.where(score == best, blk_f, float(n_sel)), axis=0, keepdims=True)
        pick = blk_f == idx
        sel_t = jnp.where(pick, 1.0, sel_t)
        score = jnp.where(pick, -3e38, score)
        if rnd % every == every - 1 and len(ow_heads) < N_HEADS:
            ow_heads.append(window_head(len(ow_heads)))
    while len(ow_heads) < N_HEADS:
        ow_heads.append(window_head(len(ow_heads)))
    notsel = (1.0 - sel_t).T.astype(BF16)
    lhs = jnp.concatenate([q4_log2, jnp.concatenate([notsel] * N_HEADS, axis=0)], axis=1)

    m_ref[...] = jnp.full(m_ref.shape, NEG_INF, F32)
    acc_ref[...] = jnp.zeros(acc_ref.shape, F32)

    def scores(j):
        start = pl.multiple_of(j * tk, tk)
        rhs = jnp.concatenate([ks_ref[:, pl.ds(start, tk)], pen_ref[:, pl.ds(start, tk)]], axis=0)
        return _dg(lhs, rhs)

    def absorb(j, slot, masked=False):
        start = pl.multiple_of(j * tk, tk)
        s = s_ref[slot]
        if masked:
            s = jnp.where((start + lax.broadcasted_iota(jnp.int32, (1, tk), 1)) <= t4, s, NEG_INF)
        m_old = m_ref[...]
        m_new = jnp.maximum(m_old, jnp.max(s, axis=-1, keepdims=True))
        p = jnp.exp2(s - jnp.concatenate([m_new] * rep, axis=1))
        acc_ref[...] = jnp.exp2(m_old - m_new) * acc_ref[...] + _dg(p.astype(BF16), vs_ref[pl.ds(start, tk), :])
        m_ref[...] = m_new

    assert qb <= tk
    n_full = q0 // tk
    s_ref[0] = scores(0)

    def body(pair, carry):
        j = 2 * pair
        s_ref[1] = scores(j + 1)
        absorb(j, 0)
        s_ref[0] = scores(j + 2)
        absorb(j + 1, 1)
        return carry

    lax.fori_loop(0, n_full // 2, body, 0)

    @pl.when(n_full % 2 == 1)
    def _():
        s_ref[1] = scores(n_full)
        absorb(n_full - 1, 0)
        absorb(n_full, 1, masked=True)

    @pl.when(n_full % 2 == 0)
    def _():
        absorb(n_full, 0, masked=True)
    acc = acc_ref[...]
    os4 = acc / _dot_lx2(acc, lsel)

    gates = _sigmoid(g_ref[0] + gb_ref[...])
    out = jnp.zeros((qb, D_BRANCH), F32)
    for h in range(N_HEADS):
        rows = slice(h * qb, (h + 1) * qb)
        base = 128 + h * N_NSA_BRANCH
        mixed = (gates[:, base:base + 1] * oc_heads[h]
                 + gates[:, base + 1:base + 2] * os4[rows]
                 + gates[:, base + 2:base + 3] * ow_heads[h])
        out = out + _dot_lx2(mixed, _placement(h))
    o_ref[0] = out


def _nsa_mixer(proj, kt_all, v_aug, cmp_pos, cmp_w1, cmp_w2, gate_b):
    bsz, s, _ = proj.shape
    n_row = s // CMP_STRIDE
    n_sel = s // SEL_BLOCK
    n_top = min(N_SELECT, n_sel)
    qb = min(NSA_Q_BLOCK, s)
    base = 6 * D_BRANCH
    col = lambda k: proj[:, :, base + k * HEAD_DIM: base + (k + 1) * HEAD_DIM]
    rk = col(0).reshape(bsz, n_row, CMP_STRIDE * HEAD_DIM)
    rv = col(1).reshape(bsz, n_row, CMP_STRIDE * HEAD_DIM)
    pos_flat = cmp_pos.reshape(2, 1, CMP_BLOCK * HEAD_DIM)
    w2_pair = jnp.stack([jnp.concatenate([cmp_w2[0], cmp_w2[0]], axis=1),
                         jnp.concatenate([cmp_w2[1], jnp.zeros_like(cmp_w2[1])], axis=1)])
    kc_bf, vc_t = _nsa_compress(rk, rv, pos_flat, cmp_w1, w2_pair)

    key_blk = jnp.arange(s, dtype=jnp.int32) // SEL_BLOCK
    penalty = jnp.where(jnp.arange(n_sel, dtype=jnp.int32)[:, None] == key_blk[None, :],
                        -MASK_BIG, 0.0).astype(BF16)
    ratio = SEL_BLOCK // CMP_STRIDE
    off = jnp.arange(n_row, dtype=jnp.int32)[None, :] - ratio * jnp.arange(n_sel, dtype=jnp.int32)[:, None]
    m_t = jnp.where((off == -1) | (off == 3), 1.0, jnp.where((off >= 0) & (off <= 2), 2.0, 0.0)).astype(BF16)
    gate_b_pad = jnp.zeros((1, D_BRANCH), F32).at[0, 128:128 + N_NSA_BRANCH * N_HEADS].set(gate_b)
    full = lambda arr: pl.BlockSpec((1,) + arr.shape[1:], lambda b, i: (b, 0, 0))
    return pl.pallas_call(
        functools.partial(_nsa_kernel, n_top),
        grid=(bsz, s // qb),
        in_specs=[pl.BlockSpec((1, qb, D_BRANCH), lambda b, i: (b, i, 5)),
                  pl.BlockSpec((1, qb, D_BRANCH), lambda b, i: (b, i, 7)),
                  full(kc_bf), full(vc_t),
                  pl.BlockSpec((128, s), lambda b, i: (2, b)),
                  pl.BlockSpec(penalty.shape, lambda b, i: (0, 0)),
                  pl.BlockSpec((s, 128), lambda b, i: (b, N_HEADS)),
                  pl.BlockSpec((128, s), lambda b, i: (3, b)),
                  pl.BlockSpec((s, 128), lambda b, i: (b, N_HEADS + 1)),
                  pl.BlockSpec(m_t.shape, lambda b, i: (0, 0)),
                  pl.BlockSpec((1, D_BRANCH), lambda b, i: (0, 0))],
        out_specs=pl.BlockSpec((1, qb, D_BRANCH), lambda b, i: (b, i, 0)),
        out_shape=jax.ShapeDtypeStruct((bsz, s, D_BRANCH), F32),
        scratch_shapes=[pltpu.VMEM((N_HEADS * qb, 128), F32), pltpu.VMEM((N_HEADS * qb, 128), F32),
                        pltpu.VMEM((2, N_HEADS * qb, min(NSA_KV_TILE, s)), F32)],
        compiler_params=_params(("parallel", "arbitrary")),
        name="nsa_attention",
    )(proj, proj, kc_bf, vc_t, kt_all, penalty, v_aug, kt_all, v_aug, m_t, gate_b_pad)


def _rwkv_kernel(c, pr_ref, pk_ref, pv_ref, pw_ref, mu_ref, w0_ref, wuph_ref, wupl_ref, a0_ref, auph_ref, aupl_ref,
                 kk_ref, ka_ref, rk_ref, lng_ref, lnb_ref, o_ref, prev_ref, st_ref):
    nb, blk, _ = pr_ref.shape
    rows = nb * blk
    n = nb * c

    @pl.when(pl.program_id(0) == 0)
    def _():
        prev_ref[...] = jnp.zeros_like(prev_ref)
        st_ref[...] = jnp.zeros_like(st_ref)

    brow = lax.broadcasted_iota(jnp.int32, (rows, 1), 0)

    def shifted(ref, k):
        x = ref[...].reshape(rows, D_BRANCH)
        prev = pltpu.roll(x, 1, axis=0)
        for b in range(nb):
            prev = jnp.where(brow == b * blk, prev_ref[k, b:b + 1, :], prev)
        for b in range(nb):
            prev_ref[k, b:b + 1, :] = x[(b + 1) * blk - 1:(b + 1) * blk, :]
        return x + (prev - x) * mu_ref[k:k + 1, :]

    r_all = shifted(pr_ref, 0)
    k_all = shifted(pk_ref, 1)
    v_all = shifted(pv_ref, 2)
    xw = shifted(pw_ref, 3)

    ones_bd = _block_ones()
    wlog = w0_ref[...] + _dot3_pre(jnp.tanh(xw), wuph_ref[...], wupl_ref[...])
    ld_all = -jnp.exp(-_softplus(-wlog) - 0.5)
    a_lr = _sigmoid(a0_ref[...] + _dot3_pre(xw, auph_ref[...], aupl_ref[...]))
    kk = k_all * kk_ref[...]
    kk = kk / jnp.maximum(jnp.sqrt(_dot_lx(kk * kk, ones_bd)), 1e-12)
    k_all = k_all * (1.0 + (a_lr - 1.0) * ka_ref[...])
    a_all = -kk
    b_all = kk * a_lr

    row = lax.broadcasted_iota(jnp.int32, (n, 1), 0)
    col = lax.broadcasted_iota(jnp.int32, (1, n), 1)
    rowb = row // c
    colb = col // c
    same = rowb == colb
    tri = jnp.where(same & (col <= row), 1.0, 0.0).astype(BF16)
    allm = jnp.where(same, 1.0, 0.0).astype(BF16)
    upper = same & (row < col)
    upper_eq = same & (row <= col)
    eye = jnp.where(row == col, 1.0, 0.0)
    col2b = jnp.concatenate([colb, colb], axis=1)
    hm = _head_masks()
    bd = ones_bd.astype(F32)
    mean_m = _block_ones(1.0 / HEAD_DIM)

    def chunk_rows(x, j):
        return jnp.concatenate([x[b * blk + j * c:b * blk + (j + 1) * c] for b in range(nb)], axis=0)

    def prepare_stages(js):
        ps = [dict(j=j) for j in js]

        def slices():
            for p in ps:
                p["r"], p["k"], p["v"], p["ld"], p["a"], p["b"] = (
                    chunk_rows(x, p["j"]) for x in (r_all, k_all, v_all, ld_all, a_all, b_all))
                p["cs"] = _dot_xl(tri, p["ld"])
                p["tot"] = _dot_xl(allm, p["ld"])

        def pair_products():
            for p in ps:
                cs, tot = p["cs"], p["tot"]
                a_t = p["a"] * jnp.exp(cs - p["ld"])
                r_t = p["r"] * jnp.exp(cs)
                inv = jnp.exp(-cs)
                fin = jnp.exp(tot - cs)
                lhs = jnp.concatenate([(p["b"] * inv) * hm[h] for h in range(N_HEADS)]
                                      + [(p["k"] * inv) * hm[h] for h in range(N_HEADS)], axis=0)
                rhs = jnp.concatenate([a_t, r_t], axis=0)
                p["small"] = _dg(lhs.astype(BF16), rhs.astype(BF16), 1, 1)
                ar_t = jnp.concatenate([a_t.T, r_t.T], axis=1)
                p["v_t"] = p["v"].T.astype(BF16)
                p["ar_b"] = [jnp.where(col2b == bb, ar_t, 0.0).astype(BF16) for bb in range(nb)]
                p["bk_b"] = [jnp.concatenate([jnp.where(rowb == bb, p["b"] * fin, 0.0),
                                              jnp.where(rowb == bb, p["k"] * fin, 0.0)], axis=0).astype(BF16)
                             for bb in range(nb)]
                p["g_rows"] = [jnp.exp(tot[bb * c:bb * c + 1, :]) for bb in range(nb)]
                p["bonus"] = _dot_lx(p["r"] * p["k"] * rk_ref[...], ones_bd) * p["v"]

        def masks():
            for p in ps:
                small = p["small"]
                p["pw"], p["tinv"], p["brs"], p["zs"], p["y0s"] = [], [], [], [], []
                for h in range(N_HEADS):
                    ba = jnp.where(upper, small[h * n:(h + 1) * n, 0:n], 0.0)
                    br = jnp.where(upper_eq, small[h * n:(h + 1) * n, n:2 * n], 0.0).astype(BF16)
                    ka = jnp.where(upper, small[(N_HEADS + h) * n:(N_HEADS + h + 1) * n, 0:n], 0.0).astype(BF16)
                    kr = jnp.where(upper_eq, small[(N_HEADS + h) * n:(N_HEADS + h + 1) * n, n:2 * n],
                                   0.0).astype(BF16)
                    vh = p["v_t"][h * HEAD_DIM:(h + 1) * HEAD_DIM, :]
                    p["pw"].append(ba)
                    p["tinv"].append(eye + ba)
                    p["brs"].append(br)
                    p["zs"].append(_dg(vh, ka))
                    p["y0s"].append(_dg(vh, kr))

        def inverse_level():
            for p in ps:
                for h in range(N_HEADS):
                    pw_bf = p["pw"][h].astype(BF16)
                    p["pw"][h] = _dg(pw_bf, pw_bf)
            for p in ps:
                for h in range(N_HEADS):
                    p["tinv"][h] = p["tinv"][h] + _dg(p["tinv"][h].astype(BF16), p["pw"][h].astype(BF16))

        def finish():
            for p in ps:
                p["tinvs"] = [t.astype(BF16) for t in p["tinv"]]

        levels = max(int(math.log2(c)) - 1, 0)
        return ps, [slices, pair_products, masks] + [inverse_level] * levels + [finish]

    def advance_stages(p):
        w = {}

        def state_products():
            xy_t = jnp.zeros((D_BRANCH, 2 * n), F32)
            for bb in range(nb):
                xy_t = xy_t + _dg(st_ref[bb].astype(BF16), p["ar_b"][bb])
            w["xy_t"] = xy_t

        def solve():
            u_rows, y_rows = [], []
            for h in range(N_HEADS):
                hs = slice(h * HEAD_DIM, (h + 1) * HEAD_DIM)
                uh = _dg((w["xy_t"][hs, 0:n] + p["zs"][h]).astype(BF16), p["tinvs"][h]).astype(BF16)
                u_rows.append(uh)
                y_rows.append(w["xy_t"][hs, n:2 * n] + _dg(uh, p["brs"][h]) + p["y0s"][h])
            w["u_t"] = jnp.concatenate(u_rows, axis=0)
            w["y_rows"] = y_rows

        def update_state():
            uv = jnp.concatenate([w["u_t"], p["v_t"]], axis=1)
            for bb in range(nb):
                st_ref[bb] = st_ref[bb] * p["g_rows"][bb] + bd * _dg(uv, p["bk_b"][bb])

        def epilogue():
            y = jnp.concatenate(w["y_rows"], axis=0).T
            mean = _dot_lx(y, mean_m)
            yc = y - mean
            var = _dot_lx(yc * yc, mean_m)
            out = yc * lax.rsqrt(var + RWKV_GN_EPS) * lng_ref[...] + lnb_ref[...]
            j = p["j"]
            o_ref[:, j * c:(j + 1) * c, :] = (out + p["bonus"]).reshape(nb, c, D_BRANCH)

        return [state_products, solve, update_state, epilogue]

    nchunk = blk // c
    groups = [list(range(g, min(g + RWKV_GROUP, nchunk))) for g in range(0, nchunk, RWKV_GROUP)]
    ps_prev, stages = prepare_stages(groups[0])
    for stage in stages:
        stage()
    for grp in groups[1:]:
        ps_next, prep = prepare_stages(grp)
        adv = [stage for p in ps_prev for stage in advance_stages(p)]
        for idx in range(max(len(prep), len(adv))):
            if idx < len(prep):
                prep[idx]()
            if idx < len(adv):
                adv[idx]()
        ps_prev = ps_next
    for p in ps_prev:
        for stage in advance_stages(p):
            stage()


def _rwkv_mixer(proj, mu, w0, w_up, a0, a_up, k_k, k_a, r_k, ln_g, ln_b):
    bsz, s, _ = proj.shape
    c = min(RWKV_CHUNK, s)
    tb = min(RWKV_BLOCK, s)
    mu_p = jnp.zeros((4, D_BRANCH), F32)
    mu_p = mu_p.at[0:3, :].set(mu[:3 * D_BRANCH].reshape(3, D_BRANCH))
    mu_p = mu_p.at[3, :2 * RWKV_RANK].set(mu[3 * D_BRANCH:])
    wup_h, wup_l = _split2(jnp.zeros((D_BRANCH, D_BRANCH), F32).at[:RWKV_RANK, :].set(w_up))
    aup_h, aup_l = _split2(jnp.zeros((D_BRANCH, D_BRANCH), F32).at[RWKV_RANK:2 * RWKV_RANK, :].set(a_up))
    row = lambda a: a.reshape(1, D_BRANCH)
    blk = lambda cidx: pl.BlockSpec((bsz, tb, D_BRANCH), lambda i, cidx=cidx: (0, i, cidx))
    rowspec = pl.BlockSpec((1, D_BRANCH), lambda i: (0, 0))
    matspec = pl.BlockSpec((D_BRANCH, D_BRANCH), lambda i: (0, 0))
    return pl.pallas_call(
        functools.partial(_rwkv_kernel, c),
        grid=(s // tb,),
        in_specs=[blk(8), blk(9), blk(10), blk(11),
                  pl.BlockSpec((4, D_BRANCH), lambda i: (0, 0)),
                  rowspec, matspec, matspec, rowspec, matspec, matspec,
                  rowspec, rowspec, rowspec, rowspec, rowspec],
        out_specs=pl.BlockSpec((bsz, tb, D_BRANCH), lambda i: (0, i, 0)),
        out_shape=jax.ShapeDtypeStruct((bsz, s, D_BRANCH), F32),
        scratch_shapes=[pltpu.VMEM((4, 8, D_BRANCH), F32), pltpu.VMEM((bsz, D_BRANCH, D_BRANCH), F32)],
        compiler_params=_params(("arbitrary",)),
        name="rwkv7",
    )(proj, proj, proj, proj, mu_p, row(w0), wup_h, wup_l, row(a0), aup_h, aup_l, row(k_k), row(k_a),
      row(r_k), row(ln_g), row(ln_b))


def _lane_sum_selector():
    r = lax.broadcasted_iota(jnp.int32, (128, 128), 0)
    return jnp.where(r == HEAD_DIM, 1.0, 0.0).astype(BF16)


def _placement(h):
    r = lax.broadcasted_iota(jnp.int32, (128, D_BRANCH), 0)
    c = lax.broadcasted_iota(jnp.int32, (128, D_BRANCH), 1)
    return jnp.where((r < HEAD_DIM) & (c == r + h * HEAD_DIM), 1.0, 0.0).astype(BF16)


def _diff_kernel(lambda_init, q_ref, k_ref, v_ref, lam_ref, g_ref, o_ref, acc_ref, m_ref, s_ref):
    qb = q_ref.shape[1]
    tk = min(DIFF_KV_TILE, k_ref.shape[1])
    rep = tk // 128
    q0 = pl.program_id(1) * qb
    lane = lax.broadcasted_iota(jnp.int32, (1, 128), 1)
    q = q_ref[0] * (DIFF_QK_DIM ** -0.5 * LOG2E)
    qs = []
    for h in range(N_HEADS):
        pair, half = divmod(h, 2)
        qp = q[:, pair * 128:(pair + 1) * 128]
        both = []
        for cc in range(2):
            lo = half * HEAD_DIM + cc * DIFF_QK_DIM
            both.append(jnp.where((lane >= lo) & (lane < lo + DIFF_QK_DIM), qp, 0.0).astype(BF16))
        qs.append(jnp.concatenate(both, axis=0))
    t = q0 + lax.broadcasted_iota(jnp.int32, (qb, 1), 0)
    t2 = jnp.concatenate([t, t], axis=0)
    m_ref[...] = jnp.full(m_ref.shape, NEG_INF, F32)
    acc_ref[...] = jnp.zeros(acc_ref.shape, F32)

    def scores(j, h):
        start = pl.multiple_of(j * tk, tk)
        return _dg(qs[h], k_ref[(h // 2) * 128:(h // 2 + 1) * 128, pl.ds(start, tk)])

    def absorb(j, h, slot, masked=False):
        start = pl.multiple_of(j * tk, tk)
        s = s_ref[slot, h]
        if masked:
            s = jnp.where((start + lax.broadcasted_iota(jnp.int32, (1, tk), 1)) <= t2, s, NEG_INF)
        m_old = m_ref[h]
        m_new = jnp.maximum(m_old, jnp.max(s, axis=-1, keepdims=True))
        p = jnp.exp2(s - jnp.concatenate([m_new] * rep, axis=1))
        acc_ref[h] = jnp.exp2(m_old - m_new) * acc_ref[h] + _dg(p.astype(BF16), v_ref[pl.ds(start, tk), h * 128:(h + 1) * 128])
        m_ref[h] = m_new

    def step(j, slot, prefetch=True, masked=False):
        for h in range(N_HEADS):
            if prefetch:
                s_ref[1 - slot, h] = scores(j + 1, h)
            absorb(j, h, slot, masked)

    assert qb <= tk
    n_full = q0 // tk
    for h in range(N_HEADS):
        s_ref[0, h] = scores(0, h)

    def body(pair, carry):
        step(2 * pair, 0)
        step(2 * pair + 1, 1)
        return carry

    lax.fori_loop(0, n_full // 2, body, 0)

    @pl.when(n_full % 2 == 1)
    def _():
        step(n_full - 1, 0)
        step(n_full, 1, prefetch=False, masked=True)

    @pl.when(n_full % 2 == 0)
    def _():
        step(n_full, 0, prefetch=False, masked=True)

    lam = lam_ref[...]
    lam_full = (jnp.exp(jnp.sum(lam[0:1] * lam[1:2], axis=-1, keepdims=True))
                - jnp.exp(jnp.sum(lam[2:3] * lam[3:4], axis=-1, keepdims=True)) + lambda_init)
    lsel = _lane_sum_selector()
    out = jnp.zeros((qb, D_BRANCH), F32)
    for h in range(N_HEADS):
        a = acc_ref[h]
        a = a / _dot_lx2(a, lsel)
        d = a[0:qb] - lam_full * a[qb:2 * qb]
        out = out + _dot_lx2(d, _placement(h))
    ms = _dot_lx(out * out, _block_ones(1.0 / HEAD_DIM))
    o_ref[0] = out * lax.rsqrt(ms + DIFF_EPS) * g_ref[...] * (1.0 - lambda_init)


def _diff_mixer(proj, kt_all, v_aug, lam, subln_g, lambda_init):
    bsz, s, _ = proj.shape
    qb = min(DIFF_Q_BLOCK, s)
    g_rep = jnp.tile(subln_g, N_HEADS).reshape(1, D_BRANCH)
    return pl.pallas_call(
        functools.partial(_diff_kernel, lambda_init),
        grid=(bsz, s // qb),
        in_specs=[pl.BlockSpec((1, qb, D_BRANCH), lambda b, i: (b, i, 12)),
                  pl.BlockSpec((D_BRANCH, s), lambda b, i: (0, b)),
                  pl.BlockSpec((s, N_HEADS * 128), lambda b, i: (b, 0)),
                  pl.BlockSpec(lam.shape, lambda b, i: (0, 0)),
                  pl.BlockSpec((1, D_BRANCH), lambda b, i: (0, 0))],
        out_specs=pl.BlockSpec((1, qb, D_BRANCH), lambda b, i: (b, i, 0)),
        out_shape=jax.ShapeDtypeStruct((bsz, s, D_BRANCH), F32),
        scratch_shapes=[pltpu.VMEM((N_HEADS, 2 * qb, 128), F32),
                        pltpu.VMEM((N_HEADS, 2 * qb, 128), F32),
                        pltpu.VMEM((2, N_HEADS, 2 * qb, min(DIFF_KV_TILE, s)), F32)],
        compiler_params=_params(("parallel", "arbitrary")),
        name="diff_attention",
    )(proj, kt_all, v_aug, lam, g_rep)


def _block_diag(w):
    n = w.shape[0] * w.shape[1]
    out = jnp.zeros((n, n), F32)
    for h in range(w.shape[0]):
        out = out.at[h * HEAD_DIM:(h + 1) * HEAD_DIM, h * HEAD_DIM:(h + 1) * HEAD_DIM].set(w[h])
    return out


def _layout_w_in(w):
    d = w.shape[0]
    nsa_end = 5 * D_BRANCH + D_BRANCH + 6 * HEAD_DIM + N_NSA_BRANCH * N_HEADS
    rwkv_end = nsa_end + 3 * D_BRANCH + 2 * RWKV_RANK
    pad1 = 8 * D_BRANCH - nsa_end
    pad2 = 12 * D_BRANCH - (rwkv_end + pad1)
    return jnp.concatenate([w[:, :nsa_end], jnp.zeros((d, pad1), w.dtype), w[:, nsa_end:rwkv_end],
                            jnp.zeros((d, pad2), w.dtype), w[:, rwkv_end:rwkv_end + D_BRANCH]], axis=1)


def _layout_wt_in(w):
    nsa0 = 5 * D_BRANCH
    ks0 = nsa0 + D_BRANCH + 2 * HEAD_DIM
    kw0 = ks0 + 2 * HEAD_DIM
    dk0 = w.shape[1] - 2 * D_BRANCH
    ks, kw = w[:, ks0:ks0 + HEAD_DIM], w[:, kw0:kw0 + HEAD_DIM]
    return jnp.concatenate([w[:, dk0:dk0 + D_BRANCH], ks, ks, kw, kw], axis=1).T


def _layout_wv_in(w):
    nsa0 = 5 * D_BRANCH
    vs0 = nsa0 + D_BRANCH + 3 * HEAD_DIM
    vw0 = vs0 + 2 * HEAD_DIM
    dv0 = w.shape[1] - D_BRANCH
    pad = jnp.zeros((w.shape[0], 128 - HEAD_DIM), w.dtype)
    cols = [w[:, dv0 + h * HEAD_DIM:dv0 + (h + 1) * HEAD_DIM] for h in range(N_HEADS)]
    cols += [w[:, vs0:vs0 + HEAD_DIM], w[:, vw0:vw0 + HEAD_DIM]]
    return jnp.concatenate([piece for c in cols for piece in (c, pad)], axis=1)


def kernel(x, norm_gain, w_in, w_out, final_gain, lru_conv_w, lru_conv_b, lru_wa, lru_ba, lru_wx, lru_bx, lru_lambda, nsa_cmp_pos, nsa_cmp_w1, nsa_cmp_w2, nsa_gate_b, rwkv_mu, rwkv_w0, rwkv_w_up, rwkv_a0, rwkv_a_up, rwkv_k_k, rwkv_k_a, rwkv_r_k, rwkv_ln_g, rwkv_ln_b, diff_lambda, diff_subln_g):
    bsz, s, d = x.shape
    depth = w_in.shape[0]
    t = bsz * s
    x2d = x.reshape(t, d)
    row = lambda a: a.reshape(1, -1)
    w_in_bf = w_in.astype(BF16)
    for l in range(depth):
        w_bf = w_in_bf[l]
        proj2d, kt_all, v_aug = _in_proj(x2d, row(norm_gain[l]), _layout_w_in(w_bf), _layout_wt_in(w_bf),
                                         _layout_wv_in(w_bf))
        proj = proj2d.reshape(bsz, s, N_PROJ_BLOCKS * D_BRANCH)
        o_lru = _lru_mixer(proj, lru_conv_w[l], row(lru_conv_b[l]), _block_diag(lru_wa[l]), row(lru_ba[l]),
                           _block_diag(lru_wx[l]), row(lru_bx[l]), row(lru_lambda[l]))
        o_nsa = _nsa_mixer(proj, kt_all, v_aug, nsa_cmp_pos[l], nsa_cmp_w1[l], nsa_cmp_w2[l], nsa_gate_b[l])
        o_rwkv = _rwkv_mixer(proj, rwkv_mu[l], rwkv_w0[l], rwkv_w_up[l], rwkv_a0[l], rwkv_a_up[l],
                             rwkv_k_k[l], rwkv_k_a[l], rwkv_r_k[l].reshape(-1), rwkv_ln_g[l], rwkv_ln_b[l])
        lambda_init = 0.8 - 0.6 * math.exp(-0.3 * l)
        o_diff = _diff_mixer(proj, kt_all, v_aug, diff_lambda[l], diff_subln_g[l], lambda_init)
        mixers = [o.reshape(t, D_BRANCH) for o in (o_lru, o_nsa, o_rwkv, o_diff)]
        x2d = _out_proj(mixers, proj2d, x2d, w_out[l].astype(BF16), row(final_gain), final=(l == depth - 1))
    return x2d.reshape(bsz, s, d)
```

```python
import functools
import math

import jax
import jax.numpy as jnp
from jax import lax
from jax.experimental import pallas as pl
from jax.experimental.pallas import tpu as pltpu

F32 = jnp.float32
BF16 = jnp.bfloat16

HEAD_DIM = 64
N_HEADS = 4
D_BRANCH = N_HEADS * HEAD_DIM
NORM_EPS = 1e-6
NEG_INF = -1e30
LRU_CONV = 4
LRU_C = 8.0
CMP_BLOCK = 32
CMP_STRIDE = 16
SEL_BLOCK = 64
N_SELECT = 16
WINDOW = 512
N_NSA_BRANCH = 3
RWKV_RANK = 32
RWKV_GN_EPS = 64e-5
DIFF_QK_DIM = HEAD_DIM // 2
DIFF_EPS = 1e-5
N_PROJ_BLOCKS = 13
RWKV_CHUNK = 64
RWKV_BLOCK = 512
RWKV_GROUP = 4
VMEM_LIMIT = 56 * 1024 * 1024
LOG2E = 1.4426950408889634
DIFF_Q_BLOCK = 256
DIFF_KV_TILE = 512
NSA_Q_BLOCK = 512
NSA_KV_TILE = 512
RESIDENT = pl.Buffered(1)
MASK_BIG = 1e30


def _dg(a, b, ca=1, cb=0):
    return lax.dot_general(a, b, (((ca,), (cb,)), ((), ())), preferred_element_type=F32)


def _split2(a):
    hi = a.astype(BF16)
    lo = (a - hi.astype(F32)).astype(BF16)
    return hi, lo


def _dot3(a, b, ca=1, cb=0):
    ah, al = _split2(a)
    bh, bl = _split2(b)
    return _dg(ah, bh, ca, cb) + (_dg(ah, bl, ca, cb) + _dg(al, bh, ca, cb))


def _dot3_pre(a, bh, bl, ca=1, cb=0):
    ah, al = _split2(a)
    return _dg(ah, bh, ca, cb) + (_dg(ah, bl, ca, cb) + _dg(al, bh, ca, cb))


def _dot_lx(a, b_exact, ca=1, cb=0):
    a1 = a.astype(BF16)
    r = a - a1.astype(F32)
    a2 = r.astype(BF16)
    a3 = (r - a2.astype(F32)).astype(BF16)
    return _dg(a1, b_exact, ca, cb) + (_dg(a2, b_exact, ca, cb) + _dg(a3, b_exact, ca, cb))


def _dot_lx2(a, b_exact, ca=1, cb=0):
    a1, a2 = _split2(a)
    return _dg(a1, b_exact, ca, cb) + _dg(a2, b_exact, ca, cb)


def _dot_xl(a_exact, b, ca=1, cb=0):
    b1 = b.astype(BF16)
    r = b - b1.astype(F32)
    b2 = r.astype(BF16)
    b3 = (r - b2.astype(F32)).astype(BF16)
    return _dg(a_exact, b1, ca, cb) + (_dg(a_exact, b2, ca, cb) + _dg(a_exact, b3, ca, cb))


def _head_masks(width=D_BRANCH, group=HEAD_DIM):
    lane = lax.broadcasted_iota(jnp.int32, (1, width), 1)
    return [((lane >= h * group) & (lane < (h + 1) * group)).astype(F32) for h in range(width // group)]


def _block_ones(scale=1.0):
    r = lax.broadcasted_iota(jnp.int32, (D_BRANCH, D_BRANCH), 0) // HEAD_DIM
    c = lax.broadcasted_iota(jnp.int32, (D_BRANCH, D_BRANCH), 1) // HEAD_DIM
    return jnp.where(r == c, scale, 0.0).astype(BF16)


def _sigmoid(x):
    return 1.0 / (1.0 + jnp.exp(-x))


def _softplus(x):
    return jnp.maximum(x, 0.0) + jnp.log1p(jnp.exp(-jnp.abs(x)))


def _params(sem):
    return pltpu.CompilerParams(dimension_semantics=sem, vmem_limit_bytes=VMEM_LIMIT)


def _proj_kernel(x_ref, g_ref, w_ref, wt_ref, wv_ref, o_ref, ot_ref, ov_ref):
    x = x_ref[...]
    ms = jnp.mean(x * x, axis=-1, keepdims=True)
    h = (x * lax.rsqrt(ms + NORM_EPS) * g_ref[...]).astype(BF16)
    o_ref[...] = _dg(h, w_ref[...])

    @pl.when(pl.program_id(1) == 0)
    def _():
        ot_ref[...] = _dg(wt_ref[...], h, 1, 1).astype(BF16)
        lane = lax.broadcasted_iota(jnp.int32, (1, wv_ref.shape[1]), 1)
        ov_ref[...] = jnp.where(lane % 128 == HEAD_DIM, 1.0, _dg(h, wv_ref[...])).astype(BF16)


def _in_proj(x2d, gain, w_bf, wt_bf, wv_bf, tm=1024, tn=N_PROJ_BLOCKS * 128):
    t, d = x2d.shape
    n = w_bf.shape[1]
    nt = wt_bf.shape[0]
    nv = wv_bf.shape[1]
    tm = min(tm, t)
    return pl.pallas_call(
        _proj_kernel,
        grid=(t // tm, n // tn),
        in_specs=[pl.BlockSpec((tm, d), lambda i, j: (i, 0)),
                  pl.BlockSpec((1, d), lambda i, j: (0, 0)),
                  pl.BlockSpec((d, tn), lambda i, j: (0, j)),
                  pl.BlockSpec((nt, d), lambda i, j: (0, 0)),
                  pl.BlockSpec((d, nv), lambda i, j: (0, 0))],
        out_specs=[pl.BlockSpec((tm, tn), lambda i, j: (i, j)),
                   pl.BlockSpec((nt, tm), lambda i, j: (0, i)),
                   pl.BlockSpec((tm, nv), lambda i, j: (i, 0))],
        out_shape=[jax.ShapeDtypeStruct((t, n), F32), jax.ShapeDtypeStruct((nt, t), BF16),
                   jax.ShapeDtypeStruct((t, nv), BF16)],
        compiler_params=_params(("parallel", "arbitrary")),
        name="in_proj",
    )(x2d, gain, w_bf, wt_bf, wv_bf)


def _out_kernel(final, o1_ref, o2_ref, o3_ref, o4_ref, gate_ref, x_ref, w_ref, fg_ref, y_ref):
    mix = jnp.concatenate([o1_ref[...], o2_ref[...], o3_ref[...], o4_ref[...]], axis=-1)
    g = gate_ref[...]
    z = mix * (g * _sigmoid(g))
    y = x_ref[...] + _dg(z.astype(BF16), w_ref[...])
    if final:
        ms = jnp.mean(y * y, axis=-1, keepdims=True)
        y = y * lax.rsqrt(ms + NORM_EPS) * fg_ref[...]
    y_ref[...] = y


def _out_proj(mixers, proj2d, x2d, w_bf, final_gain, final, tm=1024):
    t, d = x2d.shape
    dm = w_bf.shape[0]
    tm = min(tm, t)
    branch = pl.BlockSpec((tm, D_BRANCH), lambda i: (i, 0))
    return pl.pallas_call(
        functools.partial(_out_kernel, final),
        grid=(t // tm,),
        in_specs=[branch, branch, branch, branch,
                  pl.BlockSpec((tm, dm), lambda i: (i, 0)),
                  pl.BlockSpec((tm, d), lambda i: (i, 0)),
                  pl.BlockSpec((dm, d), lambda i: (0, 0)),
                  pl.BlockSpec((1, d), lambda i: (0, 0))],
        out_specs=pl.BlockSpec((tm, d), lambda i: (i, 0)),
        out_shape=jax.ShapeDtypeStruct((t, d), F32),
        compiler_params=_params(("parallel",)),
        name="out_proj",
    )(*mixers, proj2d, x2d, w_bf, final_gain)


def _lru_kernel(u_ref, cw_ref, cb_ref, wah_ref, wal_ref, ba_ref, wxh_ref, wxl_ref, bx_ref, lam_ref,
                o_ref, ubuf, hcar):
    ts = u_ref.shape[1]

    @pl.when(pl.program_id(1) == 0)
    def _():
        ubuf[0:8, :] = jnp.zeros((8, D_BRANCH), F32)
        hcar[...] = jnp.zeros_like(hcar)

    u = u_ref[0]
    ubuf[8:8 + ts, :] = u
    xc = cb_ref[...] + cw_ref[0:1, :] * ubuf[5:5 + ts, :]
    for j in range(1, LRU_CONV):
        xc = xc + cw_ref[j:j + 1, :] * ubuf[5 + j:5 + j + ts, :]
    ubuf[0:8, :] = u[ts - 8:ts, :]

    r = _sigmoid(_dot3_pre(xc, wah_ref[...], wal_ref[...]) + ba_ref[...])
    gi = _sigmoid(_dot3_pre(xc, wxh_ref[...], wxl_ref[...]) + bx_ref[...])
    log_a = (-LRU_C) * r * _softplus(-lam_ref[...])
    a = jnp.exp(log_a)
    b = jnp.sqrt(jnp.tanh(-log_a) * (a * a + 1.0)) * (gi * xc)

    row = lax.broadcasted_iota(jnp.int32, (ts, 1), 0)
    d = 1
    while d < ts:
        keep = row >= d
        a_s = jnp.where(keep, pltpu.roll(a, d, axis=0), 1.0)
        b_s = jnp.where(keep, pltpu.roll(b, d, axis=0), 0.0)
        b = a * b_s + b
        a = a * a_s
        d *= 2
    h = b + a * hcar[...]
    o_ref[0] = h
    hcar[...] = h[ts - 1:ts, :]


def _lru_mixer(proj, conv_w, conv_b, wa_bd, ba, wx_bd, bx, lam, ts=512):
    bsz, s, _ = proj.shape
    wah, wal = _split2(wa_bd)
    wxh, wxl = _split2(wx_bd)
    row = pl.BlockSpec((1, D_BRANCH), lambda b, j: (0, 0))
    mat = pl.BlockSpec((D_BRANCH, D_BRANCH), lambda b, j: (0, 0))
    return pl.pallas_call(
        _lru_kernel,
        grid=(bsz, s // ts),
        in_specs=[pl.BlockSpec((1, ts, D_BRANCH), lambda b, j: (b, j, 4)),
                  pl.BlockSpec((LRU_CONV, D_BRANCH), lambda b, j: (0, 0)),
                  row, mat, mat, row, mat, mat, row, row],
        out_specs=pl.BlockSpec((1, ts, D_BRANCH), lambda b, j: (b, j, 0)),
        out_shape=jax.ShapeDtypeStruct((bsz, s, D_BRANCH), F32),
        scratch_shapes=[pltpu.VMEM((ts + 8, D_BRANCH), F32), pltpu.VMEM((1, D_BRANCH), F32)],
        compiler_params=_params(("parallel", "arbitrary")),
        name="rg_lru",
    )(proj, conv_w, conv_b, wah, wal, ba, wxh, wxl, bx, lam)


def _cmp_kernel(rk_ref, rv_ref, pos_ref, w1_ref, w2_ref, kc_ref, vct_ref):
    half = (CMP_BLOCK // 2) * HEAD_DIM
    nrow = rk_ref.shape[1]
    outs = []
    for idx, r_ref in enumerate((rk_ref, rv_ref)):
        r = r_ref[0]
        w1 = w1_ref[idx]
        lo_half = _dot3(r, w1[0:half, :])
        hi_half = _dot3(r, w1[half:2 * half, :])
        bias = _dot3(jnp.broadcast_to(pos_ref[idx], (8, 2 * half)), w1)[0:1, :]
        hid = lo_half + pltpu.roll(hi_half, nrow - 1, axis=0) + bias
        outs.append(_dot3(jax.nn.gelu(hid), w2_ref[idx]))
    kc_ref[0] = outs[0].astype(BF16)
    vct_ref[0] = outs[1].T.astype(BF16)


def _nsa_compress(rk, rv, pos_flat, w1, w2_pair):
    bsz, nrow, width = rk.shape
    blk = pl.BlockSpec((1, nrow, width), lambda b: (b, 0, 0))
    out = pl.BlockSpec((1, nrow, 128), lambda b: (b, 0, 0))
    return pl.pallas_call(
        _cmp_kernel,
        grid=(bsz,),
        in_specs=[blk, blk,
                  pl.BlockSpec(pos_flat.shape, lambda b: (0, 0, 0)),
                  pl.BlockSpec(w1.shape, lambda b: (0, 0, 0)),
                  pl.BlockSpec(w2_pair.shape, lambda b: (0, 0, 0))],
        out_specs=[out, pl.BlockSpec((1, 128, nrow), lambda b: (b, 0, 0))],
        out_shape=[jax.ShapeDtypeStruct((bsz, nrow, 128), BF16), jax.ShapeDtypeStruct((bsz, 128, nrow), BF16)],
        compiler_params=_params(("parallel",)),
        name="nsa_compress",
    )(rk, rv, pos_flat, w1, w2_pair)


def _nsa_kernel(n_top, q_ref, g_ref, kc_ref, vct_ref, ks_ref, pen_ref, vs_ref, kw_ref, vw_ref,
                mt_ref, gb_ref, o_ref, acc_ref, m_ref, s_ref):
    qb = q_ref.shape[1]
    n_cmp = kc_ref.shape[1]
    n_sel = mt_ref.shape[0]
    tk = min(NSA_KV_TILE, vs_ref.shape[0])
    rep = tk // 128
    q0 = pl.program_id(1) * qb
    lane = lax.broadcasted_iota(jnp.int32, (1, 128), 1)
    q = q_ref[0] * (HEAD_DIM ** -0.5)
    parts = []
    for h in range(N_HEADS):
        pair, half = divmod(h, 2)
        keep = (lane >= half * HEAD_DIM) & (lane < (half + 1) * HEAD_DIM)
        parts.append(jnp.where(keep, q[:, pair * 128:(pair + 1) * 128], 0.0))
    q4 = jnp.concatenate(parts, axis=0)
    q4_log2 = (q4 * LOG2E).astype(BF16)
    t = q0 + lax.broadcasted_iota(jnp.int32, (qb, 1), 0)
    t4 = jnp.concatenate([t] * N_HEADS, axis=0)
    t_row = q0 + lax.broadcasted_iota(jnp.int32, (1, qb), 1)

    cend = lax.broadcasted_iota(jnp.int32, (n_cmp, 1), 0) * CMP_STRIDE + (CMP_BLOCK - 1)
    valid = cend <= t_row
    anyv = (t_row >= CMP_BLOCK - 1).astype(F32)
    q4_bf = q4.astype(BF16)
    oc_heads = []
    imp_t = jnp.zeros((n_cmp, qb), F32)
    for h in range(N_HEADS):
        sct = jnp.where(valid, _dg(kc_ref[0], q4_bf[h * qb:(h + 1) * qb], 1, 1), NEG_INF)
        e = jnp.exp(sct - jnp.max(sct, axis=0, keepdims=True))
        p_ct = e / jnp.sum(e, axis=0, keepdims=True)
        oc_heads.append((_dg(vct_ref[0], p_ct.astype(BF16)) * anyv).T)
        imp_t = imp_t + jnp.where(valid, p_ct, 0.0)
    pslc_t = _dot_xl(mt_ref[...], imp_t)

    blk = lax.broadcasted_iota(jnp.int32, (n_sel, 1), 0)
    blk_f = blk.astype(F32)
    cur = t_row // SEL_BLOCK
    forced = (blk == 0) | (blk == cur) | (blk == cur - 1)
    n_forced = 3
    score = jnp.where(blk > cur, -1.0, jnp.where(forced, -3e38, pslc_t))
    sub = min(qb, 256)
    nw = sub + WINDOW
    lsel = _lane_sum_selector()
    slabs = []
    for a in range(qb // sub):
        start_a = pl.multiple_of(jnp.maximum(q0 + a * sub - WINDOW, 0), 128)
        wpos = start_a + lax.broadcasted_iota(jnp.int32, (1, nw), 1)
        t_a = t[a * sub:(a + 1) * sub]
        slabs.append((start_a, (wpos <= t_a) & (wpos > t_a - WINDOW)))

    def window_head(h):
        parts = []
        for a, (start_a, allow_a) in enumerate(slabs):
            rows = slice(h * qb + a * sub, h * qb + (a + 1) * sub)
            s_w = jnp.where(allow_a, _dg(q4_log2[rows], kw_ref[:, pl.ds(start_a, nw)]), NEG_INF)
            p_w = jnp.exp2(s_w - jnp.max(s_w, axis=-1, keepdims=True))
            parts.append(_dg(p_w.astype(BF16), vw_ref[pl.ds(start_a, nw), :]))
        aw = jnp.concatenate(parts, axis=0)
        return aw / _dot_lx2(aw, lsel)

    ow_heads = []
    n_rounds = max(n_top - n_forced, 0)
    every = max(n_rounds // N_HEADS, 1)
    sel_t = jnp.where(forced, 1.0, 0.0)
    for rnd in range(n_rounds):
        best = jnp.max(score, axis=0, keepdims=True)
        idx = jnp.min(jnp.where(score == best, blk_f, float(n_sel)), axis=0, keepdims=True)
        pick = blk_f == idx
        sel_t = jnp.where(pick, 1.0, sel_t)
        score = jnp.where(pick, -3e38, score)
        if rnd % every == every - 1 and len(ow_heads) < N_HEADS:
            ow_heads.append(window_head(len(ow_heads)))
    while len(ow_heads) < N_HEADS:
        ow_heads.append(window_head(len(ow_heads)))
    notsel = (1.0 - sel_t).T.astype(BF16)
    lhs = jnp.concatenate([q4_log2, jnp.concatenate([notsel] * N_HEADS, axis=0)], axis=1)

    m_ref[...] = jnp.full(m_ref.shape, NEG_INF, F32)
    acc_ref[...] = jnp.zeros(acc_ref.shape, F32)

    def scores(j):
        start = pl.multiple_of(j * tk, tk)
        rhs = jnp.concatenate([ks_ref[:, pl.ds(start, tk)], pen_ref[:, pl.ds(start, tk)]], axis=0)
        return _dg(lhs, rhs)

    def absorb(j, slot, masked=False):
        start = pl.multiple_of(j * tk, tk)
        s = s_ref[slot]
        if masked:
            s = jnp.where((start + lax.broadcasted_iota(jnp.int32, (1, tk), 1)) <= t4, s, NEG_INF)
        m_old = m_ref[...]
        m_new = jnp.maximum(m_old, jnp.max(s, axis=-1, keepdims=True))
        p = jnp.exp2(s - jnp.concatenate([m_new] * rep, axis=1))
        acc_ref[...] = jnp.exp2(m_old - m_new) * acc_ref[...] + _dg(p.astype(BF16), vs_ref[pl.ds(start, tk), :])
        m_ref[...] = m_new

    assert qb <= tk
    n_full = q0 // tk
    s_ref[0] = scores(0)

    def body(pair, carry):
        j = 2 * pair
        s_ref[1] = scores(j + 1)
        absorb(j, 0)
        s_ref[0] = scores(j + 2)
        absorb(j + 1, 1)
        return carry

    lax.fori_loop(0, n_full // 2, body, 0)

    @pl.when(n_full % 2 == 1)
    def _():
        s_ref[1] = scores(n_full)
        absorb(n_full - 1, 0)
        absorb(n_full, 1, masked=True)

    @pl.when(n_full % 2 == 0)
    def _():
        absorb(n_full, 0, masked=True)
    acc = acc_ref[...]
    os4 = acc / _dot_lx2(acc, lsel)

    gates = _sigmoid(g_ref[0] + gb_ref[...])
    out = jnp.zeros((qb, D_BRANCH), F32)
    for h in range(N_HEADS):
        rows = slice(h * qb, (h + 1) * qb)
        base = 128 + h * N_NSA_BRANCH
        mixed = (gates[:, base:base + 1] * oc_heads[h]
                 + gates[:, base + 1:base + 2] * os4[rows]
                 + gates[:, base + 2:base + 3] * ow_heads[h])
        out = out + _dot_lx2(mixed, _placement(h))
    o_ref[0] = out


def _nsa_mixer(proj, kt_all, v_aug, cmp_pos, cmp_w1, cmp_w2, gate_b):
    bsz, s, _ = proj.shape
    n_row = s // CMP_STRIDE
    n_sel = s // SEL_BLOCK
    n_top = min(N_SELECT, n_sel)
    qb = min(NSA_Q_BLOCK, s)
    base = 6 * D_BRANCH
    col = lambda k: proj[:, :, base + k * HEAD_DIM: base + (k + 1) * HEAD_DIM]
    rk = col(0).reshape(bsz, n_row, CMP_STRIDE * HEAD_DIM)
    rv = col(1).reshape(bsz, n_row, CMP_STRIDE * HEAD_DIM)
    pos_flat = cmp_pos.reshape(2, 1, CMP_BLOCK * HEAD_DIM)
    w2_pair = jnp.stack([jnp.concatenate([cmp_w2[0], cmp_w2[0]], axis=1),
                         jnp.concatenate([cmp_w2[1], jnp.zeros_like(cmp_w2[1])], axis=1)])
    kc_bf, vc_t = _nsa_compress(rk, rv, pos_flat, cmp_w1, w2_pair)

    key_blk = jnp.arange(s, dtype=jnp.int32) // SEL_BLOCK
    penalty = jnp.where(jnp.arange(n_sel, dtype=jnp.int32)[:, None] == key_blk[None, :],
                        -MASK_BIG, 0.0).astype(BF16)
    ratio = SEL_BLOCK // CMP_STRIDE
    off = jnp.arange(n_row, dtype=jnp.int32)[None, :] - ratio * jnp.arange(n_sel, dtype=jnp.int32)[:, None]
    m_t = jnp.where((off == -1) | (off == 3), 1.0, jnp.where((off >= 0) & (off <= 2), 2.0, 0.0)).astype(BF16)
    gate_b_pad = jnp.zeros((1, D_BRANCH), F32).at[0, 128:128 + N_NSA_BRANCH * N_HEADS].set(gate_b)
    full = lambda arr: pl.BlockSpec((1,) + arr.shape[1:], lambda b, i: (b, 0, 0))
    return pl.pallas_call(
        functools.partial(_nsa_kernel, n_top),
        grid=(bsz, s // qb),
        in_specs=[pl.BlockSpec((1, qb, D_BRANCH), lambda b, i: (b, i, 5)),
                  pl.BlockSpec((1, qb, D_BRANCH), lambda b, i: (b, i, 7)),
                  full(kc_bf), full(vc_t),
                  pl.BlockSpec((128, s), lambda b, i: (2, b), pipeline_mode=RESIDENT),
                  pl.BlockSpec(penalty.shape, lambda b, i: (0, 0), pipeline_mode=RESIDENT),
                  pl.BlockSpec((s, 128), lambda b, i: (b, N_HEADS), pipeline_mode=RESIDENT),
                  pl.BlockSpec((128, s), lambda b, i: (3, b), pipeline_mode=RESIDENT),
                  pl.BlockSpec((s, 128), lambda b, i: (b, N_HEADS + 1), pipeline_mode=RESIDENT),
                  pl.BlockSpec(m_t.shape, lambda b, i: (0, 0)),
                  pl.BlockSpec((1, D_BRANCH), lambda b, i: (0, 0))],
        out_specs=pl.BlockSpec((1, qb, D_BRANCH), lambda b, i: (b, i, 0)),
        out_shape=jax.ShapeDtypeStruct((bsz, s, D_BRANCH), F32),
        scratch_shapes=[pltpu.VMEM((N_HEADS * qb, 128), F32), pltpu.VMEM((N_HEADS * qb, 128), F32),
                        pltpu.VMEM((2, N_HEADS * qb, min(NSA_KV_TILE, s)), F32)],
        compiler_params=_params(("parallel", "arbitrary")),
        name="nsa_attention",
    )(proj, proj, kc_bf, vc_t, kt_all, penalty, v_aug, kt_all, v_aug, m_t, gate_b_pad)


def _rwkv_kernel(c, pr_ref, pk_ref, pv_ref, pw_ref, mu_ref, w0_ref, wuph_ref, wupl_ref, a0_ref, auph_ref, aupl_ref,
                 kk_ref, ka_ref, rk_ref, lng_ref, lnb_ref, o_ref, prev_ref, st_ref):
    nb, blk, _ = pr_ref.shape
    rows = nb * blk
    n = nb * c

    @pl.when(pl.program_id(0) == 0)
    def _():
        prev_ref[...] = jnp.zeros_like(prev_ref)
        st_ref[...] = jnp.zeros_like(st_ref)

    brow = lax.broadcasted_iota(jnp.int32, (rows, 1), 0)

    def shifted(ref, k):
        x = ref[...].reshape(rows, D_BRANCH)
        prev = pltpu.roll(x, 1, axis=0)
        for b in range(nb):
            prev = jnp.where(brow == b * blk, prev_ref[k, b:b + 1, :], prev)
        for b in range(nb):
            prev_ref[k, b:b + 1, :] = x[(b + 1) * blk - 1:(b + 1) * blk, :]
        return x + (prev - x) * mu_ref[k:k + 1, :]

    r_all = shifted(pr_ref, 0)
    k_all = shifted(pk_ref, 1)
    v_all = shifted(pv_ref, 2)
    xw = shifted(pw_ref, 3)

    ones_bd = _block_ones()
    wlog = w0_ref[...] + _dot3_pre(jnp.tanh(xw), wuph_ref[...], wupl_ref[...])
    ld_all = -jnp.exp(-_softplus(-wlog) - 0.5)
    a_lr = _sigmoid(a0_ref[...] + _dot3_pre(xw, auph_ref[...], aupl_ref[...]))
    kk = k_all * kk_ref[...]
    kk = kk / jnp.maximum(jnp.sqrt(_dot_lx(kk * kk, ones_bd)), 1e-12)
    k_all = k_all * (1.0 + (a_lr - 1.0) * ka_ref[...])
    a_all = -kk
    b_all = kk * a_lr

    row = lax.broadcasted_iota(jnp.int32, (n, 1), 0)
    col = lax.broadcasted_iota(jnp.int32, (1, n), 1)
    rowb = row // c
    colb = col // c
    same = rowb == colb
    tri = jnp.where(same & (col <= row), 1.0, 0.0).astype(BF16)
    allm = jnp.where(same, 1.0, 0.0).astype(BF16)
    upper = same & (row < col)
    upper_eq = same & (row <= col)
    eye = jnp.where(row == col, 1.0, 0.0)
    col2b = jnp.concatenate([colb, colb], axis=1)
    hm = _head_masks()
    bd = ones_bd.astype(F32)
    mean_m = _block_ones(1.0 / HEAD_DIM)

    def chunk_rows(x, j):
        return jnp.concatenate([x[b * blk + j * c:b * blk + (j + 1) * c] for b in range(nb)], axis=0)

    def prepare_stages(js):
        ps = [dict(j=j) for j in js]

        def slices():
            for p in ps:
                p["r"], p["k"], p["v"], p["ld"], p["a"], p["b"] = (
                    chunk_rows(x, p["j"]) for x in (r_all, k_all, v_all, ld_all, a_all, b_all))
                p["cs"] = _dot_xl(tri, p["ld"])
                p["tot"] = _dot_xl(allm, p["ld"])

        def pair_products():
            for p in ps:
                cs, tot = p["cs"], p["tot"]
                a_t = p["a"] * jnp.exp(cs - p["ld"])
                r_t = p["r"] * jnp.exp(cs)
                inv = jnp.exp(-cs)
                fin = jnp.exp(tot - cs)
                lhs = jnp.concatenate([(p["b"] * inv) * hm[h] for h in range(N_HEADS)]
                                      + [(p["k"] * inv) * hm[h] for h in range(N_HEADS)], axis=0)
                rhs = jnp.concatenate([a_t, r_t], axis=0)
                p["small"] = _dg(lhs.astype(BF16), rhs.astype(BF16), 1, 1)
                ar_t = jnp.concatenate([a_t.T, r_t.T], axis=1)
                p["v_t"] = p["v"].T.astype(BF16)
                p["ar_b"] = [jnp.where(col2b == bb, ar_t, 0.0).astype(BF16) for bb in range(nb)]
                p["bk_b"] = [jnp.concatenate([jnp.where(rowb == bb, p["b"] * fin, 0.0),
                                              jnp.where(rowb == bb, p["k"] * fin, 0.0)], axis=0).astype(BF16)
                             for bb in range(nb)]
                p["g_rows"] = [jnp.exp(tot[bb * c:bb * c + 1, :]) for bb in range(nb)]
                p["bonus"] = _dot_lx(p["r"] * p["k"] * rk_ref[...], ones_bd) * p["v"]

        def masks():
            for p in ps:
                small = p["small"]
                p["pw"], p["tinv"], p["brs"], p["zs"], p["y0s"] = [], [], [], [], []
                for h in range(N_HEADS):
                    ba = jnp.where(upper, small[h * n:(h + 1) * n, 0:n], 0.0)
                    br = jnp.where(upper_eq, small[h * n:(h + 1) * n, n:2 * n], 0.0).astype(BF16)
                    ka = jnp.where(upper, small[(N_HEADS + h) * n:(N_HEADS + h + 1) * n, 0:n], 0.0).astype(BF16)
                    kr = jnp.where(upper_eq, small[(N_HEADS + h) * n:(N_HEADS + h + 1) * n, n:2 * n],
                                   0.0).astype(BF16)
                    vh = p["v_t"][h * HEAD_DIM:(h + 1) * HEAD_DIM, :]
                    p["pw"].append(ba)
                    p["tinv"].append(eye + ba)
                    p["brs"].append(br)
                    p["zs"].append(_dg(vh, ka))
                    p["y0s"].append(_dg(vh, kr))

        def inverse_level():
            for p in ps:
                for h in range(N_HEADS):
                    pw_bf = p["pw"][h].astype(BF16)
                    p["pw"][h] = _dg(pw_bf, pw_bf)
            for p in ps:
                for h in range(N_HEADS):
                    p["tinv"][h] = p["tinv"][h] + _dg(p["tinv"][h].astype(BF16), p["pw"][h].astype(BF16))

        def finish():
            for p in ps:
                p["tinvs"] = [t.astype(BF16) for t in p["tinv"]]

        levels = max(int(math.log2(c)) - 1, 0)
        return ps, [slices, pair_products, masks] + [inverse_level] * levels + [finish]

    def advance_stages(p):
        w = {}

        def state_products():
            xy_t = jnp.zeros((D_BRANCH, 2 * n), F32)
            for bb in range(nb):
                xy_t = xy_t + _dg(st_ref[bb].astype(BF16), p["ar_b"][bb])
            w["xy_t"] = xy_t

        def solve():
            u_rows, y_rows = [], []
            for h in range(N_HEADS):
                hs = slice(h * HEAD_DIM, (h + 1) * HEAD_DIM)
                uh = _dg((w["xy_t"][hs, 0:n] + p["zs"][h]).astype(BF16), p["tinvs"][h]).astype(BF16)
                u_rows.append(uh)
                y_rows.append(w["xy_t"][hs, n:2 * n] + _dg(uh, p["brs"][h]) + p["y0s"][h])
            w["u_t"] = jnp.concatenate(u_rows, axis=0)
            w["y_rows"] = y_rows

        def update_state():
            uv = jnp.concatenate([w["u_t"], p["v_t"]], axis=1)
            for bb in range(nb):
                st_ref[bb] = st_ref[bb] * p["g_rows"][bb] + bd * _dg(uv, p["bk_b"][bb])

        def epilogue():
            y = jnp.concatenate(w["y_rows"], axis=0).T
            mean = _dot_lx(y, mean_m)
            yc = y - mean
            var = _dot_lx(yc * yc, mean_m)
            out = yc * lax.rsqrt(var + RWKV_GN_EPS) * lng_ref[...] + lnb_ref[...]
            j = p["j"]
            o_ref[:, j * c:(j + 1) * c, :] = (out + p["bonus"]).reshape(nb, c, D_BRANCH)

        return [state_products, solve, update_state, epilogue]

    nchunk = blk // c
    groups = [list(range(g, min(g + RWKV_GROUP, nchunk))) for g in range(0, nchunk, RWKV_GROUP)]
    ps_prev, stages = prepare_stages(groups[0])
    for stage in stages:
        stage()
    for grp in groups[1:]:
        ps_next, prep = prepare_stages(grp)
        adv = [stage for p in ps_prev for stage in advance_stages(p)]
        for idx in range(max(len(prep), len(adv))):
            if idx < len(prep):
                prep[idx]()
            if idx < len(adv):
                adv[idx]()
        ps_prev = ps_next
    for p in ps_prev:
        for stage in advance_stages(p):
            stage()


def _rwkv_mixer(proj, mu, w0, w_up, a0, a_up, k_k, k_a, r_k, ln_g, ln_b):
    bsz, s, _ = proj.shape
    c = min(RWKV_CHUNK, s)
    tb = min(RWKV_BLOCK, s)
    mu_p = jnp.zeros((4, D_BRANCH), F32)
    mu_p = mu_p.at[0:3, :].set(mu[:3 * D_BRANCH].reshape(3, D_BRANCH))
    mu_p = mu_p.at[3, :2 * RWKV_RANK].set(mu[3 * D_BRANCH:])
    wup_h, wup_l = _split2(jnp.zeros((D_BRANCH, D_BRANCH), F32).at[:RWKV_RANK, :].set(w_up))
    aup_h, aup_l = _split2(jnp.zeros((D_BRANCH, D_BRANCH), F32).at[RWKV_RANK:2 * RWKV_RANK, :].set(a_up))
    row = lambda a: a.reshape(1, D_BRANCH)
    blk = lambda cidx: pl.BlockSpec((bsz, tb, D_BRANCH), lambda i, cidx=cidx: (0, i, cidx))
    rowspec = pl.BlockSpec((1, D_BRANCH), lambda i: (0, 0))
    matspec = pl.BlockSpec((D_BRANCH, D_BRANCH), lambda i: (0, 0))
    return pl.pallas_call(
        functools.partial(_rwkv_kernel, c),
        grid=(s // tb,),
        in_specs=[blk(8), blk(9), blk(10), blk(11),
                  pl.BlockSpec((4, D_BRANCH), lambda i: (0, 0)),
                  rowspec, matspec, matspec, rowspec, matspec, matspec,
                  rowspec, rowspec, rowspec, rowspec, rowspec],
        out_specs=pl.BlockSpec((bsz, tb, D_BRANCH), lambda i: (0, i, 0)),
        out_shape=jax.ShapeDtypeStruct((bsz, s, D_BRANCH), F32),
        scratch_shapes=[pltpu.VMEM((4, 8, D_BRANCH), F32), pltpu.VMEM((bsz, D_BRANCH, D_BRANCH), F32)],
        compiler_params=_params(("arbitrary",)),
        name="rwkv7",
    )(proj, proj, proj, proj, mu_p, row(w0), wup_h, wup_l, row(a0), aup_h, aup_l, row(k_k), row(k_a),
      row(r_k), row(ln_g), row(ln_b))


def _lane_sum_selector():
    r = lax.broadcasted_iota(jnp.int32, (128, 128), 0)
    return jnp.where(r == HEAD_DIM, 1.0, 0.0).astype(BF16)


def _placement(h):
    r = lax.broadcasted_iota(jnp.int32, (128, D_BRANCH), 0)
    c = lax.broadcasted_iota(jnp.int32, (128, D_BRANCH), 1)
    return jnp.where((r < HEAD_DIM) & (c == r + h * HEAD_DIM), 1.0, 0.0).astype(BF16)


def _diff_kernel(lambda_init, q_ref, k_ref, v_ref, lam_ref, g_ref, o_ref, acc_ref, m_ref, s_ref):
    qb = q_ref.shape[1]
    tk = min(DIFF_KV_TILE, k_ref.shape[1])
    rep = tk // 128
    q0 = pl.program_id(1) * qb
    lane = lax.broadcasted_iota(jnp.int32, (1, 128), 1)
    q = q_ref[0] * (DIFF_QK_DIM ** -0.5 * LOG2E)
    qs = []
    for h in range(N_HEADS):
        pair, half = divmod(h, 2)
        qp = q[:, pair * 128:(pair + 1) * 128]
        both = []
        for cc in range(2):
            lo = half * HEAD_DIM + cc * DIFF_QK_DIM
            both.append(jnp.where((lane >= lo) & (lane < lo + DIFF_QK_DIM), qp, 0.0).astype(BF16))
        qs.append(jnp.concatenate(both, axis=0))
    t = q0 + lax.broadcasted_iota(jnp.int32, (qb, 1), 0)
    t2 = jnp.concatenate([t, t], axis=0)
    m_ref[...] = jnp.full(m_ref.shape, NEG_INF, F32)
    acc_ref[...] = jnp.zeros(acc_ref.shape, F32)

    def scores(j, h):
        start = pl.multiple_of(j * tk, tk)
        return _dg(qs[h], k_ref[(h // 2) * 128:(h // 2 + 1) * 128, pl.ds(start, tk)])

    def absorb(j, h, slot, masked=False):
        start = pl.multiple_of(j * tk, tk)
        s = s_ref[slot, h]
        if masked:
            s = jnp.where((start + lax.broadcasted_iota(jnp.int32, (1, tk), 1)) <= t2, s, NEG_INF)
        m_old = m_ref[h]
        m_new = jnp.maximum(m_old, jnp.max(s, axis=-1, keepdims=True))
        p = jnp.exp2(s - jnp.concatenate([m_new] * rep, axis=1))
        acc_ref[h] = jnp.exp2(m_old - m_new) * acc_ref[h] + _dg(p.astype(BF16), v_ref[pl.ds(start, tk), h * 128:(h + 1) * 128])
        m_ref[h] = m_new

    def step(j, slot, prefetch=True, masked=False):
        for h in range(N_HEADS):
            if prefetch:
                s_ref[1 - slot, h] = scores(j + 1, h)
            absorb(j, h, slot, masked)

    assert qb <= tk
    n_full = q0 // tk
    for h in range(N_HEADS):
        s_ref[0, h] = scores(0, h)

    def body(pair, carry):
        step(2 * pair, 0)
        step(2 * pair + 1, 1)
        return carry

    lax.fori_loop(0, n_full // 2, body, 0)

    @pl.when(n_full % 2 == 1)
    def _():
        step(n_full - 1, 0)
        step(n_full, 1, prefetch=False, masked=True)

    @pl.when(n_full % 2 == 0)
    def _():
        step(n_full, 0, prefetch=False, masked=True)

    lam = lam_ref[...]
    lam_full = (jnp.exp(jnp.sum(lam[0:1] * lam[1:2], axis=-1, keepdims=True))
                - jnp.exp(jnp.sum(lam[2:3] * lam[3:4], axis=-1, keepdims=True)) + lambda_init)
    lsel = _lane_sum_selector()
    out = jnp.zeros((qb, D_BRANCH), F32)
    for h in range(N_HEADS):
        a = acc_ref[h]
        a = a / _dot_lx2(a, lsel)
        d = a[0:qb] - lam_full * a[qb:2 * qb]
        out = out + _dot_lx2(d, _placement(h))
    ms = _dot_lx(out * out, _block_ones(1.0 / HEAD_DIM))
    o_ref[0] = out * lax.rsqrt(ms + DIFF_EPS) * g_ref[...] * (1.0 - lambda_init)


def _diff_mixer(proj, kt_all, v_aug, lam, subln_g, lambda_init):
    bsz, s, _ = proj.shape
    qb = min(DIFF_Q_BLOCK, s)
    g_rep = jnp.tile(subln_g, N_HEADS).reshape(1, D_BRANCH)
    return pl.pallas_call(
        functools.partial(_diff_kernel, lambda_init),
        grid=(bsz, s // qb),
        in_specs=[pl.BlockSpec((1, qb, D_BRANCH), lambda b, i: (b, i, 12)),
                  pl.BlockSpec((D_BRANCH, s), lambda b, i: (0, b), pipeline_mode=RESIDENT),
                  pl.BlockSpec((s, N_HEADS * 128), lambda b, i: (b, 0), pipeline_mode=RESIDENT),
                  pl.BlockSpec(lam.shape, lambda b, i: (0, 0)),
                  pl.BlockSpec((1, D_BRANCH), lambda b, i: (0, 0))],
        out_specs=pl.BlockSpec((1, qb, D_BRANCH), lambda b, i: (b, i, 0)),
        out_shape=jax.ShapeDtypeStruct((bsz, s, D_BRANCH), F32),
        scratch_shapes=[pltpu.VMEM((N_HEADS, 2 * qb, 128), F32),
                        pltpu.VMEM((N_HEADS, 2 * qb, 128), F32),
                        pltpu.VMEM((2, N_HEADS, 2 * qb, min(DIFF_KV_TILE, s)), F32)],
        compiler_params=_params(("parallel", "arbitrary")),
        name="diff_attention",
    )(proj, kt_all, v_aug, lam, g_rep)


def _block_diag(w):
    n = w.shape[0] * w.shape[1]
    out = jnp.zeros((n, n), F32)
    for h in range(w.shape[0]):
        out = out.at[h * HEAD_DIM:(h + 1) * HEAD_DIM, h * HEAD_DIM:(h + 1) * HEAD_DIM].set(w[h])
    return out


def _layout_w_in(w):
    d = w.shape[0]
    nsa_end = 5 * D_BRANCH + D_BRANCH + 6 * HEAD_DIM + N_NSA_BRANCH * N_HEADS
    rwkv_end = nsa_end + 3 * D_BRANCH + 2 * RWKV_RANK
    pad1 = 8 * D_BRANCH - nsa_end
    pad2 = 12 * D_BRANCH - (rwkv_end + pad1)
    return jnp.concatenate([w[:, :nsa_end], jnp.zeros((d, pad1), w.dtype), w[:, nsa_end:rwkv_end],
                            jnp.zeros((d, pad2), w.dtype), w[:, rwkv_end:rwkv_end + D_BRANCH]], axis=1)


def _layout_wt_in(w):
    nsa0 = 5 * D_BRANCH
    ks0 = nsa0 + D_BRANCH + 2 * HEAD_DIM
    kw0 = ks0 + 2 * HEAD_DIM
    dk0 = w.shape[1] - 2 * D_BRANCH
    ks, kw = w[:, ks0:ks0 + HEAD_DIM], w[:, kw0:kw0 + HEAD_DIM]
    return jnp.concatenate([w[:, dk0:dk0 + D_BRANCH], ks, ks, kw, kw], axis=1).T


def _layout_wv_in(w):
    nsa0 = 5 * D_BRANCH
    vs0 = nsa0 + D_BRANCH + 3 * HEAD_DIM
    vw0 = vs0 + 2 * HEAD_DIM
    dv0 = w.shape[1] - D_BRANCH
    pad = jnp.zeros((w.shape[0], 128 - HEAD_DIM), w.dtype)
    cols = [w[:, dv0 + h * HEAD_DIM:dv0 + (h + 1) * HEAD_DIM] for h in range(N_HEADS)]
    cols += [w[:, vs0:vs0 + HEAD_DIM], w[:, vw0:vw0 + HEAD_DIM]]
    return jnp.concatenate([piece for c in cols for piece in (c, pad)], axis=1)


def kernel(x, norm_gain, w_in, w_out, final_gain, lru_conv_w, lru_conv_b, lru_wa, lru_ba, lru_wx, lru_bx, lru_lambda, nsa_cmp_pos, nsa_cmp_w1, nsa_cmp_w2, nsa_gate_b, rwkv_mu, rwkv_w0, rwkv_w_up, rwkv_a0, rwkv_a_up, rwkv_k_k, rwkv_k_a, rwkv_r_k, rwkv_ln_g, rwkv_ln_b, diff_lambda, diff_subln_g):
    bsz, s, d = x.shape
    depth = w_in.shape[0]
    t = bsz * s
    x2d = x.reshape(t, d)
    row = lambda a: a.reshape(1, -1)
    w_in_bf = w_in.astype(BF16)
    for l in range(depth):
        w_bf = w_in_bf[l]
        proj2d, kt_all, v_aug = _in_proj(x2d, row(norm_gain[l]), _layout_w_in(w_bf), _layout_wt_in(w_bf),
                                         _layout_wv_in(w_bf))
        proj = proj2d.reshape(bsz, s, N_PROJ_BLOCKS * D_BRANCH)
        o_lru = _lru_mixer(proj, lru_conv_w[l], row(lru_conv_b[l]), _block_diag(lru_wa[l]), row(lru_ba[l]),
                           _block_diag(lru_wx[l]), row(lru_bx[l]), row(lru_lambda[l]))
        o_nsa = _nsa_mixer(proj, kt_all, v_aug, nsa_cmp_pos[l], nsa_cmp_w1[l], nsa_cmp_w2[l], nsa_gate_b[l])
        o_rwkv = _rwkv_mixer(proj, rwkv_mu[l], rwkv_w0[l], rwkv_w_up[l], rwkv_a0[l], rwkv_a_up[l],
                             rwkv_k_k[l], rwkv_k_a[l], rwkv_r_k[l].reshape(-1), rwkv_ln_g[l], rwkv_ln_b[l])
        lambda_init = 0.8 - 0.6 * math.exp(-0.3 * l)
        o_diff = _diff_mixer(proj, kt_all, v_aug, diff_lambda[l], diff_subln_g[l], lambda_init)
        mixers = [o.reshape(t, D_BRANCH) for o in (o_lru, o_nsa, o_rwkv, o_diff)]
        x2d = _out_proj(mixers, proj2d, x2d, w_out[l].astype(BF16), row(final_gain), final=(l == depth - 1))
    return x2d.reshape(bsz, s, d)
```
